```python
import math
import numpy as np
import jax
import jax.numpy as jnp
from jax import lax

D_MODEL = 1024
BATCH = 4
SEQ = 4096
DEPTH = 4

GRID_W = 64
CTX_LEN = 256
MIX_HALF = D_MODEL // 2
S5_GROUP = 16
S5_GROUPS = MIX_HALF // S5_GROUP
S5_STATE = 64
NA_HEAD_DIM = 64
NA_HEADS = MIX_HALF // NA_HEAD_DIM
NA_WIN_ROWS = 8
NA_WIN_COLS = 16
GLA_HEADS = 4
GLA_DK = MIX_HALF // (2 * GLA_HEADS)
GLA_DV = MIX_HALF // GLA_HEADS
GLA_QK = GLA_HEADS * GLA_DK
GLA_RANK = 16
GLA_TAU = 16.0
GLA_CHUNK = 64
DIFF_HEADS = 4
DIFF_DV = MIX_HALF // DIFF_HEADS
DIFF_DH = DIFF_DV // 2
DIFF_BLOCK = 128
N_GROUPS = 4
EXP_PER_GROUP = 8
N_EXPERTS = N_GROUPS * EXP_PER_GROUP
TOP_K = 2
D_EXPERT = D_MODEL // 4
ROPE_BASE = 10000.0
EPS = 1e-5
EVEN_SPLITS = (MIX_HALF, MIX_HALF, MIX_HALF, MIX_HALF)
ODD_SPLITS = (GLA_QK, GLA_QK, MIX_HALF, MIX_HALF, 2 * GLA_RANK, MIX_HALF, MIX_HALF, MIX_HALF)
EVEN_IN = sum(EVEN_SPLITS)
ODD_IN = sum(ODD_SPLITS)

kernel_name = 'hybrid_s5_natten_gla_diffattn_hmoe_dit'

F32 = jnp.float32


def _cuts(splits):
    return [int(v) for v in np.cumsum(splits)[:-1]]


def layer_norm(x, g, b):
    x32 = x.astype(F32)
    mu = jnp.mean(x32, -1, keepdims=True)
    var = jnp.mean(jnp.square(x32 - mu), -1, keepdims=True)
    return ((x32 - mu) * lax.rsqrt(var + EPS) * g.astype(F32) + b.astype(F32)).astype(x.dtype)


def rms_norm(x, g):
    x32 = x.astype(F32)
    return x32 * lax.rsqrt(jnp.mean(jnp.square(x32), -1, keepdims=True) + EPS) * g.astype(F32)


def s5_discretize(lam_re, lam_im, log_dt, b_re, b_im):
    lam_re, lam_im = lam_re.astype(F32), lam_im.astype(F32)
    dt = jnp.exp(log_dt.astype(F32))[:, None]
    mag = jnp.exp(lam_re * dt)
    a_re = mag * jnp.cos(lam_im * dt)
    a_im = mag * jnp.sin(lam_im * dt)
    den = lam_re * lam_re + lam_im * lam_im
    f_re = ((a_re - 1.0) * lam_re + a_im * lam_im) / den
    f_im = (a_im * lam_re - (a_re - 1.0) * lam_im) / den
    b_re, b_im = b_re.astype(F32), b_im.astype(F32)
    bb_re = f_re[..., None] * b_re - f_im[..., None] * b_im
    bb_im = f_re[..., None] * b_im + f_im[..., None] * b_re
    return a_re, a_im, bb_re, bb_im


def _complex_linear_combine(e1, e2):
    a1r, a1i, b1r, b1i = e1
    a2r, a2i, b2r, b2i = e2
    return (a1r * a2r - a1i * a2i, a1r * a2i + a1i * a2r,
            a2r * b1r - a2i * b1i + b2r, a2r * b1i + a2i * b1r + b2i)


def s5_scan(u, a_re, a_im, bb_re, bb_im, h0, reverse):
    bu_re = jnp.einsum('btgh,gph->btgp', u, bb_re)
    bu_im = jnp.einsum('btgh,gph->btgp', u, bb_im)
    if h0 is not None:
        idx = -1 if reverse else 0
        h0_re, h0_im = h0
        bu_re = bu_re.at[:, idx].add(a_re * h0_re - a_im * h0_im)
        bu_im = bu_im.at[:, idx].add(a_re * h0_im + a_im * h0_re)
    shape = (1, u.shape[1]) + a_re.shape
    elems = (jnp.broadcast_to(a_re, shape), jnp.broadcast_to(a_im, shape), bu_re, bu_im)
    _, _, h_re, h_im = lax.associative_scan(_complex_linear_combine, elems, reverse=reverse, axis=1)
    return h_re, h_im


def s5_readout(h_re, h_im, c_re, c_im):
    return (jnp.einsum('btgp,ghp->btgh', h_re, c_re.astype(F32))
            - jnp.einsum('btgp,ghp->btgh', h_im, c_im.astype(F32)))


def s5_mixer(u_c, u_l, lam_re, lam_im, log_dt, b_re, b_im, c_re, c_im, d_skip, glu_w, glu_b, need_ctx):
    grp = lambda u: u.astype(F32).reshape(u.shape[0], u.shape[1], S5_GROUPS, S5_GROUP)
    uc, ul = grp(u_c), grp(u_l)
    d = d_skip.astype(F32).reshape(S5_GROUPS, S5_GROUP)
    y_l = d * ul
    y_c = d * uc if need_ctx else None
    for direction in range(2):
        rev = direction == 1
        a_re, a_im, bb_re, bb_im = s5_discretize(lam_re[direction], lam_im[direction], log_dt[direction],
                                                 b_re[direction], b_im[direction])
        hc_re, hc_im = s5_scan(uc, a_re, a_im, bb_re, bb_im, None, rev)
        end = 0 if rev else -1
        hl_re, hl_im = s5_scan(ul, a_re, a_im, bb_re, bb_im, (hc_re[:, end], hc_im[:, end]), rev)
        y_l = y_l + s5_readout(hl_re, hl_im, c_re[direction], c_im[direction])
        if need_ctx:
            y_c = y_c + s5_readout(hc_re, hc_im, c_re[direction], c_im[direction])

    def glu(y):
        z = jax.nn.gelu(y.reshape(y.shape[0], y.shape[1], MIX_HALF))
        val, gate = jnp.split(z @ glu_w.astype(F32) + glu_b.astype(F32), 2, axis=-1)
        return (val * jax.nn.sigmoid(gate)).astype(u_l.dtype)

    return (glu(y_c) if need_ctx else None), glu(y_l)


def natten_mixer(q_c, k_c, v_c, q_l, k_l, v_l, rpb, need_ctx):
    b, t, _ = q_l.shape
    rows = t // GRID_W
    wr = min(NA_WIN_ROWS, rows)
    wc = NA_WIN_COLS
    nk = wr * wc
    scale = NA_HEAD_DIM ** -0.5
    heads = lambda z: z.reshape(z.shape[0], z.shape[1], NA_HEADS, NA_HEAD_DIM).transpose(0, 2, 1, 3)
    kc, vc = heads(k_c), heads(v_c)
    grid = lambda z: heads(z).reshape(b, NA_HEADS, rows, GRID_W, NA_HEAD_DIM)
    qg, kg, vg = grid(q_l * scale), grid(k_l), grid(v_l)
    col_start = np.clip(np.arange(GRID_W) - wc // 2, 0, GRID_W - wc)
    col_idx = col_start[:, None] + np.arange(wc)[None, :]
    dc_idx = col_idx - np.arange(GRID_W)[:, None] + (NA_WIN_COLS - 1)

    def row_block(r):
        rs = jnp.clip(r - wr // 2, 0, rows - wr)
        q_row = lax.dynamic_index_in_dim(qg, r, axis=2, keepdims=False)
        k_rows = lax.dynamic_slice_in_dim(kg, rs, wr, axis=2)
        v_rows = lax.dynamic_slice_in_dim(vg, rs, wr, axis=2)
        k_win = k_rows[:, :, :, col_idx].transpose(0, 1, 3, 2, 4, 5).reshape(b, NA_HEADS, GRID_W, nk, NA_HEAD_DIM)
        v_win = v_rows[:, :, :, col_idx].transpose(0, 1, 3, 2, 4, 5).reshape(b, NA_HEADS, GRID_W, nk, NA_HEAD_DIM)
        dr_idx = rs + jnp.arange(wr) - r + (NA_WIN_ROWS - 1)
        bias = rpb[:, dr_idx[:, None, None], dc_idx[None, :, :]]
        bias = bias.transpose(0, 2, 1, 3).reshape(NA_HEADS, GRID_W, nk).astype(F32)
        s_loc = jnp.einsum('bhwd,bhwkd->bhwk', q_row, k_win).astype(F32) + bias
        s_ctx = jnp.einsum('bhwd,bhcd->bhwc', q_row, kc).astype(F32)
        p = jax.nn.softmax(jnp.concatenate([s_loc, s_ctx], axis=-1), axis=-1).astype(v_l.dtype)
        return (jnp.einsum('bhwk,bhwkd->bhwd', p[..., :nk], v_win)
                + jnp.einsum('bhwc,bhcd->bhwd', p[..., nk:], vc))

    o = lax.map(row_block, jnp.arange(rows))
    y_l = o.transpose(1, 0, 3, 2, 4).reshape(b, t, MIX_HALF)
    y_c = None
    if need_ctx:
        qc = heads(q_c * scale)
        p = jax.nn.softmax(jnp.einsum('bhqd,bhkd->bhqk', qc, kc).astype(F32), axis=-1).astype(v_c.dtype)
        y_c = jnp.einsum('bhqk,bhkd->bhqd', p, vc).transpose(0, 2, 1, 3).reshape(b, q_c.shape[1], MIX_HALF)
    return y_c, y_l


def even_mixer(a_c, a_l, w_in, w_out, lam_re, lam_im, log_dt, b_re, b_im, c_re, c_im, d_skip,
               glu_w, glu_b, rpb, need_ctx):
    u_c, q_c, k_c, v_c = jnp.split(a_c @ w_in, _cuts(EVEN_SPLITS), axis=-1)
    u_l, q_l, k_l, v_l = jnp.split(a_l @ w_in, _cuts(EVEN_SPLITS), axis=-1)
    s_c, s_l = s5_mixer(u_c, u_l, lam_re, lam_im, log_dt, b_re, b_im, c_re, c_im, d_skip, glu_w, glu_b, need_ctx)
    n_c, n_l = natten_mixer(q_c, k_c, v_c, q_l, k_l, v_l, rpb, need_ctx)
    y_l = jnp.concatenate([s_l, n_l], axis=-1) @ w_out
    y_c = jnp.concatenate([s_c, n_c], axis=-1) @ w_out if need_ctx else None
    return y_c, y_l


def gla_chunked(q, k, v, g, s0, need_out):
    b, h, t, dk = q.shape
    dv = v.shape[-1]
    n = t // GLA_CHUNK
    blk = lambda z: z.reshape(b, h, n, GLA_CHUNK, z.shape[-1])
    q, k, v, g = blk(q), blk(k), blk(v), blk(g)
    gc = jnp.cumsum(g, axis=3)
    g_end = gc[:, :, :, -1:, :]
    u = jnp.einsum('bhncd,bhncv->bhndv', k * jnp.exp(g_end - gc), v)
    decay = jnp.exp(g_end[:, :, :, 0, :])
    if s0 is None:
        s0 = jnp.zeros((b, h, dk, dv), q.dtype)

    def step(s, inp):
        dec, un = inp
        return dec[..., None] * s + un, s

    s_fin, s_prev = lax.scan(step, s0, (jnp.moveaxis(decay, 2, 0), jnp.moveaxis(u, 2, 0)))
    if not need_out:
        return None, s_fin
    s_prev = jnp.moveaxis(s_prev, 0, 2)
    q_dec = q * jnp.exp(gc)
    att = jnp.einsum('bhnid,bhnjd->bhnij', q_dec, k * jnp.exp(-gc))
    att = jnp.where(jnp.tril(jnp.ones((GLA_CHUNK, GLA_CHUNK), bool)), att, 0.0)
    o = jnp.einsum('bhnij,bhnjv->bhniv', att, v) + jnp.einsum('bhncd,bhndv->bhncv', q_dec, s_prev)
    return o.reshape(b, h, t, dv), s_fin


def gla_mixer(q_c, k_c, v_c, r_c, z_c, q_l, k_l, v_l, r_l, z_l, gate_w, gate_b, norm_g, need_ctx):
    def heads(z, dh):
        return z.astype(F32).reshape(z.shape[0], z.shape[1], GLA_HEADS, dh).transpose(0, 2, 1, 3)

    def gate(z, d):
        zz = z.astype(F32)[..., d * GLA_RANK:(d + 1) * GLA_RANK]
        gl = jax.nn.log_sigmoid(zz @ gate_w[d].astype(F32) + gate_b[d].astype(F32)) / GLA_TAU
        return heads(gl, GLA_DK)

    scale = GLA_DK ** -0.5
    qc, kc, vc = heads(q_c, GLA_DK) * scale, heads(k_c, GLA_DK), heads(v_c, GLA_DV)
    ql, kl, vl = heads(q_l, GLA_DK) * scale, heads(k_l, GLA_DK), heads(v_l, GLA_DV)
    fl = lambda z: jnp.flip(z, axis=2)
    oc_f, sc_f = gla_chunked(qc, kc, vc, gate(z_c, 0), None, need_ctx)
    ol_f, _ = gla_chunked(ql, kl, vl, gate(z_l, 0), sc_f, True)
    oc_b, sc_b = gla_chunked(fl(qc), fl(kc), fl(vc), fl(gate(z_c, 1)), None, need_ctx)
    ol_b, _ = gla_chunked(fl(ql), fl(kl), fl(vl), fl(gate(z_l, 1)), sc_b, True)

    def out(o, r):
        o = rms_norm(o.transpose(0, 2, 1, 3), norm_g.reshape(GLA_HEADS, GLA_DV))
        return (o.reshape(o.shape[0], o.shape[1], MIX_HALF) * jax.nn.silu(r.astype(F32))).astype(r.dtype)

    y_l = out(ol_f + fl(ol_b), r_l)
    y_c = out(oc_f + fl(oc_b), r_c) if need_ctx else None
    return y_c, y_l


def rope_1d(x, pos):
    n = x.shape[-1] // 2
    inv = ROPE_BASE ** (-jnp.arange(n, dtype=F32) / n)
    ang = pos.astype(F32)[:, None] * inv[None, :]
    cos = jnp.cos(ang)[:, None, None, :]
    sin = jnp.sin(ang)[:, None, None, :]
    x1, x2 = x[..., :n], x[..., n:]
    return jnp.concatenate([x1 * cos - x2 * sin, x1 * sin + x2 * cos], axis=-1)


def axial_rope(x, row_pos, col_pos):
    half = x.shape[-1] // 2
    return jnp.concatenate([rope_1d(x[..., :half], row_pos), rope_1d(x[..., half:], col_pos)],
                           axis=-1).astype(x.dtype)


def diff_mixer(q_c, k_c, v_c, q_l, k_l, v_l, lq1, lk1, lq2, lk2, norm_g, lam_init, row_pos, col_pos, need_ctx):
    b, t, _ = q_l.shape
    qk = lambda z: z.reshape(z.shape[0], z.shape[1], DIFF_HEADS, 2, DIFF_DH)
    to_bh = lambda z: z.transpose(0, 2, 3, 1, 4)
    vh = lambda z: z.reshape(z.shape[0], z.shape[1], DIFF_HEADS, DIFF_DV).transpose(0, 2, 1, 3)
    scale = DIFF_DH ** -0.5
    ql = to_bh(axial_rope(qk(q_l * scale), row_pos, col_pos))
    kl = to_bh(axial_rope(qk(k_l), row_pos, col_pos))
    kc, vc = to_bh(qk(k_c)), vh(v_c)
    k_all = jnp.concatenate([kc, kl], axis=3)
    v_all = jnp.concatenate([vc, vh(v_l)], axis=2)
    lam = (jnp.exp(jnp.sum(lq1.astype(F32) * lk1.astype(F32)))
           - jnp.exp(jnp.sum(lq2.astype(F32) * lk2.astype(F32))) + lam_init)

    def attend(qb, kk, vv):
        p = jax.nn.softmax(jnp.einsum('bhmqd,bhmkd->bhmqk', qb, kk).astype(F32), axis=-1)
        w = p[:, :, 0] - lam * p[:, :, 1]
        return jnp.einsum('bhqk,bhkd->bhqd', w.astype(vv.dtype), vv)

    nb = t // DIFF_BLOCK
    q_blocks = ql.reshape(b, DIFF_HEADS, 2, nb, DIFF_BLOCK, DIFF_DH).transpose(3, 0, 1, 2, 4, 5)
    o = lax.map(lambda qb: attend(qb, k_all, v_all), q_blocks)
    o_l = o.transpose(1, 0, 3, 2, 4).reshape(b, t, DIFF_HEADS, DIFF_DV)

    def post(o):
        o32 = rms_norm(o, norm_g) * (1.0 - lam_init)
        return o32.reshape(o.shape[0], o.shape[1], MIX_HALF).astype(q_l.dtype)

    y_l = post(o_l)
    y_c = post(attend(to_bh(qk(q_c * scale)), kc, vc).transpose(0, 2, 1, 3)) if need_ctx else None
    return y_c, y_l


def odd_mixer(a_c, a_l, w_in, w_out, gate_w, gate_b, gla_g, lq1, lk1, lq2, lk2, diff_g, lam_init,
              row_pos, col_pos, need_ctx):
    gq_c, gk_c, gv_c, gr_c, gz_c, dq_c, dk_c, dv_c = jnp.split(a_c @ w_in, _cuts(ODD_SPLITS), axis=-1)
    gq_l, gk_l, gv_l, gr_l, gz_l, dq_l, dk_l, dv_l = jnp.split(a_l @ w_in, _cuts(ODD_SPLITS), axis=-1)
    g_c, g_l = gla_mixer(gq_c, gk_c, gv_c, gr_c, gz_c, gq_l, gk_l, gv_l, gr_l, gz_l, gate_w, gate_b, gla_g, need_ctx)
    d_c, d_l = diff_mixer(dq_c, dk_c, dv_c, dq_l, dk_l, dv_l, lq1, lk1, lq2, lk2, diff_g, lam_init,
                          row_pos, col_pos, need_ctx)
    y_l = jnp.concatenate([g_l, d_l], axis=-1) @ w_out
    y_c = jnp.concatenate([g_c, d_c], axis=-1) @ w_out if need_ctx else None
    return y_c, y_l


def hier_moe(h, w_grp, b_grp, w_exp, b_exp, w_gate, w_up, w_down):
    lg = (h @ w_grp + b_grp).astype(F32)
    g_sel = jnp.argmax(lg, axis=-1)
    p_grp = jnp.max(jax.nn.softmax(lg, axis=-1), axis=-1, keepdims=True)
    le = (h @ w_exp + b_exp).astype(F32)
    le = le.reshape(le.shape[:-1] + (N_GROUPS, EXP_PER_GROUP))
    le_in = jnp.sum(le * jax.nn.one_hot(g_sel, N_GROUPS, dtype=F32)[..., None], axis=-2)
    top_v, top_i = lax.top_k(le_in, TOP_K)
    w_top = jax.nn.softmax(top_v, axis=-1) * p_grp
    eid = g_sel[..., None] * EXP_PER_GROUP + top_i
    combine = jnp.sum(jax.nn.one_hot(eid, N_EXPERTS, dtype=F32) * w_top[..., None], axis=-2).astype(h.dtype)
    y = jnp.zeros_like(h)
    for gi in range(N_GROUPS):
        e0, e1 = gi * EXP_PER_GROUP, (gi + 1) * EXP_PER_GROUP
        hg = jnp.einsum('btd,edf->btef', h, w_gate[e0:e1])
        hu = jnp.einsum('btd,edf->btef', h, w_up[e0:e1])
        act = jax.nn.silu(hg) * hu * combine[..., e0:e1, None]
        y = y + jnp.einsum('btef,efd->btd', act, w_down[e0:e1])
    return y


def setup_inputs(seed: int = 0) -> dict:
    key = jax.random.key(seed)
    keys = iter(jax.random.split(key, 64))

    def nrm(shape, std):
        return jax.random.normal(next(keys), shape, F32) * std

    D = D_MODEL
    ne, no = (DEPTH + 1) // 2, DEPTH // 2
    beta = (8.0 * DEPTH) ** -0.25
    n_idx = jnp.arange(S5_STATE, dtype=F32)
    return {
        'x': nrm((BATCH, SEQ, D), 1.0),
        'c': nrm((BATCH, D), 1.0),
        'ctx': nrm((BATCH, CTX_LEN, D), 1.0),
        'c_ctx': nrm((D,), 1.0),
        'mod_w': nrm((DEPTH, D, 6 * D), D ** -0.5),
        'mod_b': nrm((DEPTH, 6 * D), 0.01),
        'ln_g': 1.0 + nrm((DEPTH, 2, D), 0.01),
        'ln_b': nrm((DEPTH, 2, D), 0.01),
        'even_w_in': nrm((ne, D, EVEN_IN), D ** -0.5),
        'even_w_out': nrm((ne, D, D), beta * D ** -0.5),
        's5_lam_re': -0.5 + nrm((ne, 2, S5_GROUPS, S5_STATE), 0.01),
        's5_lam_im': math.pi * n_idx + nrm((ne, 2, S5_GROUPS, S5_STATE), 0.01),
        's5_log_dt': jax.random.uniform(next(keys), (ne, 2, S5_GROUPS), F32, math.log(1e-3), math.log(1e-1)),
        's5_b_re': nrm((ne, 2, S5_GROUPS, S5_STATE, S5_GROUP), (2 * S5_GROUP) ** -0.5),
        's5_b_im': nrm((ne, 2, S5_GROUPS, S5_STATE, S5_GROUP), (2 * S5_GROUP) ** -0.5),
        's5_c_re': nrm((ne, 2, S5_GROUPS, S5_GROUP, S5_STATE), S5_STATE ** -0.5),
        's5_c_im': nrm((ne, 2, S5_GROUPS, S5_GROUP, S5_STATE), S5_STATE ** -0.5),
        's5_d': nrm((ne, MIX_HALF), 1.0),
        's5_glu_w': nrm((ne, MIX_HALF, 2 * MIX_HALF), MIX_HALF ** -0.5),
        's5_glu_b': nrm((ne, 2 * MIX_HALF), 0.01),
        'na_rpb': nrm((ne, NA_HEADS, 2 * NA_WIN_ROWS - 1, 2 * NA_WIN_COLS - 1), 0.02),
        'odd_w_in': nrm((no, D, ODD_IN), D ** -0.5),
        'odd_w_out': nrm((no, D, D), beta * D ** -0.5),
        'gla_gate_w': nrm((no, 2, GLA_RANK, GLA_QK), GLA_RANK ** -0.5),
        'gla_gate_b': nrm((no, 2, GLA_QK), 0.01),
        'gla_norm_g': 1.0 + nrm((no, MIX_HALF), 0.01),
        'diff_lq1': nrm((no, DIFF_DH), 0.1),
        'diff_lk1': nrm((no, DIFF_DH), 0.1),
        'diff_lq2': nrm((no, DIFF_DH), 0.1),
        'diff_lk2': nrm((no, DIFF_DH), 0.1),
        'diff_norm_g': 1.0 + nrm((no, DIFF_DV), 0.01),
        'moe_w_grp': nrm((DEPTH, D, N_GROUPS), D ** -0.5),
        'moe_b_grp': nrm((DEPTH, N_GROUPS), 0.01),
        'moe_w_exp': nrm((DEPTH, D, N_EXPERTS), D ** -0.5),
        'moe_b_exp': nrm((DEPTH, N_EXPERTS), 0.01),
        'moe_w_gate': nrm((DEPTH, N_EXPERTS, D, D_EXPERT), D ** -0.5),
        'moe_w_up': nrm((DEPTH, N_EXPERTS, D, D_EXPERT), D ** -0.5),
        'moe_w_down': nrm((DEPTH, N_EXPERTS, D_EXPERT, D), beta * D_EXPERT ** -0.5),
    }


def reference(x, c, ctx, c_ctx, mod_w, mod_b, ln_g, ln_b, even_w_in, even_w_out, s5_lam_re, s5_lam_im,
              s5_log_dt, s5_b_re, s5_b_im, s5_c_re, s5_c_im, s5_d, s5_glu_w, s5_glu_b, na_rpb,
              odd_w_in, odd_w_out, gla_gate_w, gla_gate_b, gla_norm_g, diff_lq1, diff_lk1, diff_lq2,
              diff_lk2, diff_norm_g, moe_w_grp, moe_b_grp, moe_w_exp, moe_b_exp, moe_w_gate, moe_w_up,
              moe_w_down):
    t = jnp.arange(x.shape[1])
    row_pos, col_pos = t // GRID_W, t % GRID_W
    alpha = (2.0 * DEPTH) ** 0.25
    s_lat = jax.nn.silu(c)
    s_ctx = jax.nn.silu(c_ctx)
    h_l, h_c = x, ctx
    for i in range(DEPTH):
        need_ctx = i < DEPTH - 1
        m_l = jnp.split(s_lat @ mod_w[i] + mod_b[i], 6, axis=-1)
        sh1_l, sc1_l, g1_l, sh2_l, sc2_l, g2_l = [m[:, None, :] for m in m_l]
        sh1_c, sc1_c, g1_c, sh2_c, sc2_c, g2_c = jnp.split(s_ctx @ mod_w[i] + mod_b[i], 6, axis=-1)
        a_l = h_l * (1.0 + sc1_l) + sh1_l
        a_c = h_c * (1.0 + sc1_c) + sh1_c
        j = i // 2
        if i % 2 == 0:
            y_c, y_l = even_mixer(a_c, a_l, even_w_in[j], even_w_out[j], s5_lam_re[j], s5_lam_im[j],
                                  s5_log_dt[j], s5_b_re[j], s5_b_im[j], s5_c_re[j], s5_c_im[j], s5_d[j],
                                  s5_glu_w[j], s5_glu_b[j], na_rpb[j], need_ctx)
        else:
            lam_init = 0.8 - 0.6 * math.exp(-0.3 * i)
            y_c, y_l = odd_mixer(a_c, a_l, odd_w_in[j], odd_w_out[j], gla_gate_w[j], gla_gate_b[j],
                                 gla_norm_g[j], diff_lq1[j], diff_lk1[j], diff_lq2[j], diff_lk2[j],
                                 diff_norm_g[j], lam_init, row_pos, col_pos, need_ctx)
        h_l = layer_norm(alpha * h_l + g1_l * y_l, ln_g[i, 0], ln_b[i, 0])
        f_l = hier_moe(h_l * (1.0 + sc2_l) + sh2_l, moe_w_grp[i], moe_b_grp[i], moe_w_exp[i], moe_b_exp[i],
                       moe_w_gate[i], moe_w_up[i], moe_w_down[i])
        h_l = layer_norm(alpha * h_l + g2_l * f_l, ln_g[i, 1], ln_b[i, 1])
        if need_ctx:
            h_c = layer_norm(alpha * h_c + g1_c * y_c, ln_g[i, 0], ln_b[i, 0])
            f_c = hier_moe(h_c * (1.0 + sc2_c) + sh2_c, moe_w_grp[i], moe_b_grp[i], moe_w_exp[i], moe_b_exp[i],
                           moe_w_gate[i], moe_w_up[i], moe_w_down[i])
            h_c = layer_norm(alpha * h_c + g2_c * f_c, ln_g[i, 1], ln_b[i, 1])
    return h_l
```

```python
import functools
import math

import numpy as np
import jax
import jax.numpy as jnp
from jax import lax
from jax.experimental import pallas as pl
from jax.experimental.pallas import tpu as pltpu

F32 = jnp.float32
BF16 = jnp.bfloat16
HIGHEST = lax.Precision.HIGHEST

D_MODEL = 1024
DEPTH = 4
GRID_W = 64
CTX = 256
HALF = D_MODEL // 2
S5_H = 16
S5_G = HALF // S5_H
S5_P = 64
S5_L = 16
NA_HEADS = 8
NA_DH = 64
NA_WR = 8
NA_WC = 16
NA_QR = 4
NA_KR = 12
GLA_HEADS = 4
GLA_DK = 64
GLA_DV = 128
GLA_RANK = 16
GLA_TAU = 16.0
GLA_CHUNK = 64
DIFF_HEADS = 4
DIFF_DV = 128
DIFF_DH = 64
N_GROUPS = 4
EPG = 8
N_EXPERTS = N_GROUPS * EPG
D_EXPERT = D_MODEL // 4
ROPE_BASE = 10000.0
EPS = 1e-5
ALPHA = (2.0 * DEPTH) ** 0.25
EVEN_IN = 4 * HALF
ODD_IN_PAD = 3200
NEG = -1e30
ROW_TILE = 256
VMEM_LIMIT = 56 * 1024 * 1024


def _cparams(sem):
    return pltpu.CompilerParams(dimension_semantics=sem, vmem_limit_bytes=VMEM_LIMIT)


def _bdot(a, b):
    return jnp.dot(a.astype(BF16), b.astype(BF16), preferred_element_type=F32)


def _bdot_nt(a, b):
    return lax.dot_general(a.astype(BF16), b.astype(BF16), (((1,), (1,)), ((), ())),
                           preferred_element_type=F32)


def _hdot(a, b):
    return jnp.dot(a, b, precision=HIGHEST, preferred_element_type=F32)


def _layer_norm(x, g, b):
    mu = jnp.mean(x, axis=-1, keepdims=True)
    xc = x - mu
    var = jnp.mean(xc * xc, axis=-1, keepdims=True)
    return xc * lax.rsqrt(var + EPS) * g + b


def _silu(x):
    return x * jax.nn.sigmoid(x)


def _mods_kernel(cc_ref, w_ref, b_ref, o_ref):
    o_ref[0] = _hdot(_silu(cc_ref[...]), w_ref[0]) + b_ref[0]


def _mods(cc, mod_w, mod_b):
    nblk = 6
    return pl.pallas_call(
        _mods_kernel,
        out_shape=jax.ShapeDtypeStruct((DEPTH, 8, 6 * D_MODEL), F32),
        grid=(DEPTH, nblk),
        in_specs=[pl.BlockSpec((8, D_MODEL), lambda l, j: (0, 0)),
                  pl.BlockSpec((1, D_MODEL, D_MODEL), lambda l, j: (l, 0, j)),
                  pl.BlockSpec((1, 1, D_MODEL), lambda l, j: (l, 0, j))],
        out_specs=pl.BlockSpec((1, 8, D_MODEL), lambda l, j: (l, 0, j)),
        compiler_params=_cparams(("arbitrary", "arbitrary")),
        name="mods",
    )(cc, mod_w, mod_b.reshape(DEPTH, 1, 6 * D_MODEL))


def _mod_row(nbatch):
    return lambda b, j: jnp.where(j == 0, nbatch, b)


def _inproj_kernel(h_ref, sh_ref, sc_ref, w_ref, o_ref):
    a = h_ref[0] * (1.0 + sc_ref[0]) + sh_ref[0]
    o_ref[0] = _bdot(a, w_ref[...])


def _inproj(h, mods_l, w):
    nb, s, _ = h.shape
    n = w.shape[1]
    row = _mod_row(nb)
    return pl.pallas_call(
        _inproj_kernel,
        out_shape=jax.ShapeDtypeStruct((nb, s, n), F32),
        grid=(nb, s // ROW_TILE),
        in_specs=[pl.BlockSpec((1, ROW_TILE, D_MODEL), lambda b, j: (b, j, 0)),
                  pl.BlockSpec((1, 1, D_MODEL), lambda b, j: (row(b, j), 0, 0)),
                  pl.BlockSpec((1, 1, D_MODEL), lambda b, j: (row(b, j), 0, 1)),
                  pl.BlockSpec((D_MODEL, n), lambda b, j: (0, 0))],
        out_specs=pl.BlockSpec((1, ROW_TILE, n), lambda b, j: (b, j, 0)),
        compiler_params=_cparams(("parallel", "parallel")),
        name="inproj",
    )(h, mods_l, mods_l, w)


def _s5_tables(lam_re, lam_im, log_dt, b_re, b_im, c_re, c_im, d_skip):
    L, H, P, G = S5_L, S5_H, S5_P, S5_G
    lam_re, lam_im = lam_re.astype(F32), lam_im.astype(F32)
    dt = jnp.exp(log_dt.astype(F32))[..., None]
    mag = jnp.exp(lam_re * dt)
    a_re = mag * jnp.cos(lam_im * dt)
    a_im = mag * jnp.sin(lam_im * dt)
    den = lam_re * lam_re + lam_im * lam_im
    f_re = ((a_re - 1.0) * lam_re + a_im * lam_im) / den
    f_im = (a_im * lam_re - (a_re - 1.0) * lam_im) / den
    b_re, b_im = b_re.astype(F32), b_im.astype(F32)
    bb_re = f_re[..., None] * b_re - f_im[..., None] * b_im
    bb_im = f_re[..., None] * b_im + f_im[..., None] * b_re
    tau = jnp.arange(L + 1, dtype=F32)[:, None, None, None]
    mag_t = jnp.exp(lam_re[None] * dt[None] * tau)
    ang_t = lam_im[None] * dt[None] * tau
    p_re = mag_t * jnp.cos(ang_t)
    p_im = mag_t * jnp.sin(ang_t)
    c_re, c_im = c_re.astype(F32), c_im.astype(F32)
    e_re = c_re[None] * p_re[:, :, :, None, :] - c_im[None] * p_im[:, :, :, None, :]
    e_im = c_re[None] * p_im[:, :, :, None, :] + c_im[None] * p_re[:, :, :, None, :]
    k = (jnp.einsum('tdgjp,dgpi->tdgji', e_re, bb_re, precision=HIGHEST)
         - jnp.einsum('tdgjp,dgpi->tdgji', e_im, bb_im, precision=HIGHEST))
    s_i, t_i = np.meshgrid(np.arange(L), np.arange(L), indexing='ij')
    kf = k[np.clip(t_i - s_i, 0, L), 0] * jnp.asarray(t_i >= s_i, F32)[..., None, None, None]
    kb = k[np.clip(s_i - t_i, 0, L), 1] * jnp.asarray(s_i >= t_i, F32)[..., None, None, None]
    skip = (jnp.asarray(s_i == t_i, F32)[..., None, None, None]
            * (jnp.eye(H, dtype=F32)[None] * d_skip.astype(F32).reshape(G, 1, H))[None, None])
    intra = (kf + kb + skip).transpose(2, 0, 4, 1, 3).reshape(G, L * H, L * H)

    def state_in(d, pw):
        pr, pi = p_re[pw, d], p_im[pw, d]
        re = pr[..., None] * bb_re[d][None] - pi[..., None] * bb_im[d][None]
        im = pr[..., None] * bb_im[d][None] + pi[..., None] * bb_re[d][None]
        return jnp.concatenate([re, im], axis=2).transpose(1, 0, 3, 2).reshape(G, L * H, 2 * P)

    sf = state_in(0, np.arange(L)[::-1].copy())
    sb = state_in(1, np.arange(L))
    w1 = jnp.concatenate([intra, sf, sb], axis=-1)

    def state_out(d, pw):
        er, ei = e_re[pw, d], e_im[pw, d]
        return jnp.concatenate([er, -ei], axis=-1).transpose(1, 3, 0, 2).reshape(G, 2 * P, L * H)

    w2 = jnp.concatenate([state_out(0, np.arange(1, L + 1)),
                          state_out(1, np.arange(L, 0, -1))], axis=1)
    ar, ai = p_re[L], p_im[L]
    rows = []
    for d in range(2):
        rows += [jnp.concatenate([ar[d], ar[d]], -1), jnp.concatenate([-ai[d], ai[d]], -1),
                 jnp.concatenate([ai[d], -ai[d]], -1)]
    rows += [jnp.zeros_like(rows[0])] * 2
    al = jnp.stack(rows, axis=1)
    return w1.astype(BF16), w2.astype(BF16), al


def _s5_kernel(x_ref, w1_ref, w2_ref, al_ref, y_ref, xw_ref, sw_ref, hh_ref, *, nch, nctx):
    lh = S5_L * S5_H
    p2 = 2 * S5_P
    xw_ref[...] = jnp.dot(x_ref[0], w1_ref[0], preferred_element_type=F32)
    sw_ref[:, 0:p2] = pltpu.roll(xw_ref[:, lh:lh + p2], S5_P, 1)
    sw_ref[:, p2:2 * p2] = pltpu.roll(xw_ref[:, lh + p2:lh + 2 * p2], S5_P, 1)
    al = al_ref[0]
    a1f, a2f, a3f, a1b, a2b, a3b = [al[i:i + 1] for i in range(6)]

    def body(k, carry):
        vf, wf, vb, wb = carry
        kb = jnp.where(k < nctx, nctx - 1 - k, nch + nctx - 1 - k)
        rf = pl.ds(pl.multiple_of(k * 8, 8), 8)
        rb = pl.ds(pl.multiple_of(kb * 8, 8), 8)
        hh_ref[rf, 0:p2] = vf
        hh_ref[rb, p2:2 * p2] = vb
        s_f = xw_ref[rf, lh:lh + p2]
        s_fs = sw_ref[rf, 0:p2]
        s_b = xw_ref[rb, lh + p2:lh + 2 * p2]
        s_bs = sw_ref[rb, p2:2 * p2]
        return (vf * a1f + wf * a2f + s_f, wf * a1f + vf * a3f + s_fs,
                vb * a1b + wb * a2b + s_b, wb * a1b + vb * a3b + s_bs)

    zero = jnp.zeros((8, p2), F32)
    lax.fori_loop(0, nch, body, (zero,) * 4)
    y_ref[0] = xw_ref[:, 0:lh] + _bdot(hh_ref[...], w2_ref[0])


def _s5(u, tables):
    nb, s, _ = u.shape
    nch = s // S5_L
    lh = S5_L * S5_H
    r = nch * 8
    x = u.astype(BF16).reshape(nb, nch, S5_L, S5_G, S5_H).transpose(3, 1, 0, 2, 4)
    x = jnp.pad(x, ((0, 0), (0, 0), (0, 8 - nb), (0, 0), (0, 0))).reshape(S5_G, r, lh)
    w1, w2, al = tables
    y = pl.pallas_call(
        functools.partial(_s5_kernel, nch=nch, nctx=CTX // S5_L),
        out_shape=jax.ShapeDtypeStruct((S5_G, r, lh), F32),
        grid=(S5_G,),
        in_specs=[pl.BlockSpec((1, r, lh), lambda g: (g, 0, 0)),
                  pl.BlockSpec((1, lh, 2 * lh), lambda g: (g, 0, 0)),
                  pl.BlockSpec((1, lh, lh), lambda g: (g, 0, 0)),
                  pl.BlockSpec((1, 8, 2 * S5_P), lambda g: (g, 0, 0))],
        out_specs=pl.BlockSpec((1, r, lh), lambda g: (g, 0, 0)),
        scratch_shapes=[pltpu.VMEM((r, 2 * lh), F32), pltpu.VMEM((r, lh), F32), pltpu.VMEM((r, lh), F32)],
        compiler_params=_cparams(("parallel",)),
        name="s5",
    )(x, w1, w2, al)
    y = y.reshape(S5_G, nch, 8, S5_L, S5_H)[:, :, :nb]
    return y.transpose(2, 1, 3, 0, 4).reshape(nb, s, HALF)


def _na_bias_tables(rpb, rows):
    w = GRID_W
    pats = []
    for r0 in (0, NA_QR, rows - NA_QR):
        k0 = int(np.clip(r0 - NA_WR // 2, 0, rows - NA_KR))
        rq = r0 + np.arange(NA_QR)[:, None, None, None]
        wq = np.arange(w)[None, :, None, None]
        kr = k0 + np.arange(NA_KR)[None, None, :, None]
        kc = np.arange(w)[None, None, None, :]
        rs = np.clip(rq - NA_WR // 2, 0, rows - NA_WR)
        cs = np.clip(wq - NA_WC // 2, 0, w - NA_WC)
        valid = (kr >= rs) & (kr < rs + NA_WR) & (kc >= cs) & (kc < cs + NA_WC)
        dr = np.clip(kr - rq + NA_WR - 1, 0, 2 * NA_WR - 2) + 0 * kc
        dc = np.clip(kc - wq + NA_WC - 1, 0, 2 * NA_WC - 2) + 0 * kr
        shape = (NA_QR * w, NA_KR * w)
        pats.append((np.broadcast_to(valid, (NA_QR, w, NA_KR, w)).reshape(shape),
                     np.broadcast_to(dr, (NA_QR, w, NA_KR, w)).reshape(shape),
                     np.broadcast_to(dc, (NA_QR, w, NA_KR, w)).reshape(shape)))
    valid = np.stack([p[0] for p in pats])
    dr = np.stack([p[1] for p in pats])
    dc = np.stack([p[2] for p in pats])
    bias = rpb.astype(F32)[:, dr, dc]
    return jnp.where(jnp.asarray(valid)[None], bias, NEG)


def _natten_kernel(q_ref, k_ref, v_ref, bias_ref, o_ref, qb_ref, kb_ref, vb_ref, *, rows):
    nq = NA_QR * GRID_W
    nk = NA_KR * GRID_W
    nblk = rows // NA_QR
    scale = NA_DH ** -0.5
    qb_ref[...] = (q_ref[0] * scale).astype(BF16)
    kb_ref[...] = k_ref[0].astype(BF16)
    vb_ref[...] = v_ref[0].astype(BF16)
    lane = lax.broadcasted_iota(jnp.int32, (1, 2 * NA_DH), 1)
    head_mask = [lane < NA_DH, lane >= NA_DH]
    kc = kb_ref[0:CTX, :]
    vc = vb_ref[0:CTX, :]

    def attend(q, extra):
        outs = []
        for h in range(2):
            qm = jnp.where(head_mask[h], q, jnp.zeros_like(q))
            s_c = _bdot_nt(qm, kc)
            m = jnp.max(s_c, axis=-1, keepdims=True)
            if extra is not None:
                k_loc, v_loc, bias = extra
                s_l = _bdot_nt(qm, k_loc) + bias[h]
                m = jnp.maximum(m, jnp.max(s_l, axis=-1, keepdims=True))
                p_l = jnp.exp(s_l - m)
            p_c = jnp.exp(s_c - m)
            den = jnp.sum(p_c, axis=-1, keepdims=True)
            acc = _bdot(p_c, vc)
            if extra is not None:
                den = den + jnp.sum(p_l, axis=-1, keepdims=True)
                acc = acc + _bdot(p_l, v_loc)
            outs.append(acc / den)
        return jnp.where(head_mask[0], outs[0], outs[1])

    o_ref[0, 0:CTX, :] = attend(qb_ref[0:CTX, :], None)

    def body(i, carry):
        r0 = i * NA_QR
        k0 = jnp.clip(r0 - NA_WR // 2, 0, rows - NA_KR)
        pat = jnp.where(i == 0, 0, jnp.where(i == nblk - 1, 2, 1))
        qs = pl.multiple_of(CTX + r0 * GRID_W, GRID_W)
        ks = pl.multiple_of(CTX + k0 * GRID_W, GRID_W)
        q = qb_ref[pl.ds(qs, nq), :]
        k_loc = kb_ref[pl.ds(ks, nk), :]
        v_loc = vb_ref[pl.ds(ks, nk), :]
        bias = [bias_ref[h, pat] for h in range(2)]
        o_ref[0, pl.ds(qs, nq), :] = attend(q, (k_loc, v_loc, bias))
        return carry

    lax.fori_loop(0, nblk, body, 0)


def _natten(a, bias):
    nb, s, _ = a.shape
    rows = (s - CTX) // GRID_W
    pw = 2 * NA_DH
    nq, nk = NA_QR * GRID_W, NA_KR * GRID_W
    hb = HALF // pw
    return pl.pallas_call(
        functools.partial(_natten_kernel, rows=rows),
        out_shape=jax.ShapeDtypeStruct((nb, s, HALF), F32),
        grid=(nb, hb),
        in_specs=[pl.BlockSpec((1, s, pw), lambda b, h: (b, 0, hb + h)),
                  pl.BlockSpec((1, s, pw), lambda b, h: (b, 0, 2 * hb + h)),
                  pl.BlockSpec((1, s, pw), lambda b, h: (b, 0, 3 * hb + h)),
                  pl.BlockSpec((2, 3, nq, nk), lambda b, h: (h, 0, 0, 0))],
        out_specs=pl.BlockSpec((1, s, pw), lambda b, h: (b, 0, h)),
        scratch_shapes=[pltpu.VMEM((s, pw), BF16)] * 3,
        compiler_params=_cparams(("parallel", "parallel")),
        name="natten",
    )(a, a, a, bias)


def _gelu_tanh(x):
    return 0.5 * x * (1.0 + jnp.tanh(math.sqrt(2.0 / math.pi) * (x + 0.044715 * (x * x * x))))


def _out_kernel(ya_ref, yb_ref, h_ref, g_ref, e0_ref, e1_ref, wo_ref, lng_ref, lnb_ref, o_ref, *, glu):
    if glu:
        z = _bdot(_gelu_tanh(ya_ref[0]), e0_ref[...]) + e1_ref[...]
        ya = z[:, :HALF] * jax.nn.sigmoid(z[:, HALF:])
    else:
        parts = []
        for hd in range(GLA_HEADS):
            cols = slice(hd * GLA_DV, (hd + 1) * GLA_DV)
            o = ya_ref[0, :, cols]
            o = o * lax.rsqrt(jnp.mean(o * o, axis=-1, keepdims=True) + EPS) * e1_ref[:, cols]
            parts.append(o * _silu(e0_ref[0, :, cols]))
        ya = jnp.concatenate(parts, axis=1)
    y = _bdot(ya, wo_ref[0:HALF, :]) + _bdot(yb_ref[0], wo_ref[HALF:, :])
    o_ref[0] = _layer_norm(ALPHA * h_ref[0] + g_ref[0] * y, lng_ref[...], lnb_ref[...])


def _outproj(ya, yb, h, mods_l, w_out, ln_g, ln_b, *, glu_w=None, glu_b=None, proj=None, norm_g=None):
    nb, s, _ = h.shape
    row = _mod_row(nb)
    glu = glu_w is not None
    tile = lambda b, j: (b, j, 0)
    const = lambda b, j: (0, 0)
    if glu:
        extra_specs = [pl.BlockSpec((HALF, D_MODEL), const), pl.BlockSpec((1, D_MODEL), const)]
        extra = [glu_w.astype(BF16), glu_b.astype(F32).reshape(1, D_MODEL)]
    else:
        rblk = (2 * GLA_HEADS * GLA_DK + HALF) // HALF
        extra_specs = [pl.BlockSpec((1, ROW_TILE, HALF), lambda b, j: (b, j, rblk)),
                       pl.BlockSpec((1, HALF), const)]
        extra = [proj, norm_g.astype(F32).reshape(1, HALF)]
    in_specs = [pl.BlockSpec((1, ROW_TILE, HALF), tile),
                pl.BlockSpec((1, ROW_TILE, HALF), tile),
                pl.BlockSpec((1, ROW_TILE, D_MODEL), tile),
                pl.BlockSpec((1, 1, D_MODEL), lambda b, j: (row(b, j), 0, 2))] + extra_specs + [
                pl.BlockSpec((D_MODEL, D_MODEL), const), pl.BlockSpec((1, D_MODEL), const),
                pl.BlockSpec((1, D_MODEL), const)]
    return pl.pallas_call(
        functools.partial(_out_kernel, glu=glu),
        out_shape=jax.ShapeDtypeStruct((nb, s, D_MODEL), F32),
        grid=(nb, s // ROW_TILE),
        in_specs=in_specs,
        out_specs=pl.BlockSpec((1, ROW_TILE, D_MODEL), tile),
        compiler_params=_cparams(("parallel", "parallel")),
        name="outproj_glu" if glu else "outproj_gate",
    )(ya, yb, h, mods_l, *extra, w_out.astype(BF16), ln_g.astype(F32).reshape(1, D_MODEL),
      ln_b.astype(F32).reshape(1, D_MODEL))


def _log_sigmoid(x):
    return jnp.minimum(x, 0.0) - jnp.log1p(jnp.exp(-jnp.abs(x)))


def _gla_kernel(q_ref, k_ref, v_ref, z_ref, wg_ref, bg_ref, o_ref, gf_ref, gb_ref, st_ref, *, nch):
    c = GLA_CHUNK
    dk2 = 2 * GLA_DK
    nctx = CTX // c
    z = z_ref[0]
    gf_ref[...] = _log_sigmoid(_hdot(z, wg_ref[0, 0]) + bg_ref[0, 0]) * (1.0 / GLA_TAU)
    gb_ref[...] = _log_sigmoid(_hdot(z, wg_ref[1, 0]) + bg_ref[1, 0]) * (1.0 / GLA_TAU)
    ri = lax.broadcasted_iota(jnp.int32, (c, c), 0)
    ci = lax.broadcasted_iota(jnp.int32, (c, c), 1)
    lane = lax.broadcasted_iota(jnp.int32, (1, dk2), 1)
    head_mask = [lane < GLA_DK, lane >= GLA_DK]
    ones = jnp.ones((c, dk2), F32)
    scale = GLA_DK ** -0.5

    def run(direction):
        g_ref = gf_ref if direction == 0 else gb_ref
        keep = (ci <= ri) if direction == 0 else (ci >= ri)
        tri = keep.astype(F32)
        st_ref[...] = jnp.zeros_like(st_ref)

        def body(i, carry):
            if direction == 0:
                n = i
            else:
                n = jnp.where(i < nctx, nctx - 1 - i, nch + nctx - 1 - i)
            rs = pl.ds(pl.multiple_of(n * c, c), c)
            g = g_ref[rs, :]
            gc = _hdot(tri, g)
            gend = gc[c - 1:c, :] if direction == 0 else gc[0:1, :]
            q = q_ref[0, rs, :] * scale
            k = k_ref[0, rs, :]
            qd = q * jnp.exp(gc)
            kd = k * jnp.exp(-gc)
            kdec = k * jnp.exp(gend - gc)
            dcol = jnp.exp(lax.dot_general(g, ones, (((0,), (0,)), ((), ())), precision=HIGHEST,
                                           preferred_element_type=F32))
            kdec_t = kdec.T.astype(BF16)
            for h in range(2):
                qm = jnp.where(head_mask[h], qd, 0.0).astype(BF16)
                att = jnp.where(keep, _bdot_nt(qm, kd), 0.0)
                vh = v_ref[0, rs, h * GLA_DV:(h + 1) * GLA_DV].astype(BF16)
                st = st_ref[h]
                o = _bdot(att, vh) + _bdot(qm, st)
                st_ref[h] = dcol * st + jnp.dot(kdec_t, vh, preferred_element_type=F32)
                cols = slice(h * GLA_DV, (h + 1) * GLA_DV)
                if direction == 0:
                    o_ref[0, rs, cols] = o
                else:
                    o_ref[0, rs, cols] += o
            return carry

        lax.fori_loop(0, nch, body, 0)

    run(0)
    run(1)


def _gla(a, wg, bg):
    nb, s, _ = a.shape
    nch = s // GLA_CHUNK
    hp = GLA_HEADS // 2
    dk2, dv2 = 2 * GLA_DK, 2 * GLA_DV
    zblk = (2 * GLA_HEADS * GLA_DK + 5 * HALF) // 128
    return pl.pallas_call(
        functools.partial(_gla_kernel, nch=nch),
        out_shape=jax.ShapeDtypeStruct((nb, s, HALF), F32),
        grid=(nb, hp),
        in_specs=[pl.BlockSpec((1, s, dk2), lambda b, h: (b, 0, h)),
                  pl.BlockSpec((1, s, dk2), lambda b, h: (b, 0, hp + h)),
                  pl.BlockSpec((1, s, dv2), lambda b, h: (b, 0, hp + h)),
                  pl.BlockSpec((1, s, 128), lambda b, h: (b, 0, zblk)),
                  pl.BlockSpec((2, 1, 128, dk2), lambda b, h: (0, h, 0, 0)),
                  pl.BlockSpec((2, 1, 1, dk2), lambda b, h: (0, h, 0, 0))],
        out_specs=pl.BlockSpec((1, s, dv2), lambda b, h: (b, 0, h)),
        scratch_shapes=[pltpu.VMEM((s, dk2), F32), pltpu.VMEM((s, dk2), F32),
                        pltpu.VMEM((2, dk2, GLA_DV), F32)],
        compiler_params=_cparams(("parallel", "parallel")),
        name="gla",
    )(a, a, a, a, wg, bg)


def _rope_tables(s):
    t = np.arange(s - CTX)
    n = DIFF_DH // 4
    inv = ROPE_BASE ** (-np.arange(n, dtype=np.float64) / n)
    ang_r = (t // GRID_W)[:, None] * inv[None]
    ang_c = (t % GRID_W)[:, None] * inv[None]
    ang = np.concatenate([ang_r, ang_r, ang_c, ang_c], axis=1)
    ang = np.concatenate([ang, ang], axis=1)
    cos = np.concatenate([np.ones((CTX, 128)), np.cos(ang)], axis=0)
    sin = np.concatenate([np.zeros((CTX, 128)), np.sin(ang)], axis=0)
    first = ((np.arange(128) // n) % 2 == 0)[None]
    sin_up = np.where(first, -sin, 0.0)
    sin_dn = np.where(first, 0.0, sin)
    f = lambda z: jnp.asarray(z, F32)
    return f(cos), f(sin_up), f(sin_dn)


def _rope_kernel(q_ref, k_ref, cos_ref, su_ref, sd_ref, qo_ref, ko_ref):
    n = DIFF_DH // 4
    cos = jnp.concatenate([cos_ref[...]] * DIFF_HEADS, axis=1)
    su = jnp.concatenate([su_ref[...]] * DIFF_HEADS, axis=1)
    sd = jnp.concatenate([sd_ref[...]] * DIFF_HEADS, axis=1)

    def rot(x):
        w = x.shape[1]
        return x * cos + pltpu.roll(x, w - n, 1) * su + pltpu.roll(x, n, 1) * sd

    qo_ref[0] = rot(q_ref[0] * (DIFF_DH ** -0.5)).astype(BF16)
    ko_ref[0] = rot(k_ref[0]).astype(BF16)


def _rope(a, tables):
    nb, s, _ = a.shape
    qblk = (2 * GLA_HEADS * GLA_DK + 2 * HALF) // HALF
    tile = lambda b, j: (b, j, 0)
    tab = pl.BlockSpec((ROW_TILE, 128), lambda b, j: (j, 0))
    return pl.pallas_call(
        _rope_kernel,
        out_shape=[jax.ShapeDtypeStruct((nb, s, HALF), BF16)] * 2,
        grid=(nb, s // ROW_TILE),
        in_specs=[pl.BlockSpec((1, ROW_TILE, HALF), lambda b, j: (b, j, qblk)),
                  pl.BlockSpec((1, ROW_TILE, HALF), lambda b, j: (b, j, qblk + 1)),
                  tab, tab, tab],
        out_specs=[pl.BlockSpec((1, ROW_TILE, HALF), tile)] * 2,
        compiler_params=_cparams(("parallel", "parallel")),
        name="rope",
    )(a, a, *tables)


def _diff_kernel(q_ref, k_ref, v_ref, lam_ref, ng_ref, o_ref, vb_ref, *, lam_init):
    j = pl.program_id(2)
    lane = lax.broadcasted_iota(jnp.int32, (1, 2 * DIFF_DH), 1)
    map_mask = [lane < DIFF_DH, lane >= DIFF_DH]
    lv = lam_ref[...]
    lam = (jnp.exp(jnp.sum(lv[0:1] * lv[1:2], axis=-1, keepdims=True))
           - jnp.exp(jnp.sum(lv[2:3] * lv[3:4], axis=-1, keepdims=True)) + lam_init)

    @pl.when(j == 0)
    def _():
        vb_ref[...] = v_ref[0].astype(BF16)

    def attend(nkeys):
        q = q_ref[0]
        k = k_ref[0, 0:nkeys, :]
        w = None
        for m in range(2):
            s = _bdot_nt(jnp.where(map_mask[m], q, jnp.zeros_like(q)), k)
            p = jnp.exp(s - jnp.max(s, axis=-1, keepdims=True))
            p = p * (1.0 / jnp.sum(p, axis=-1, keepdims=True))
            w = p if m == 0 else w - lam * p
        o = _bdot(w, vb_ref[0:nkeys, :])
        o = o * lax.rsqrt(jnp.mean(o * o, axis=-1, keepdims=True) + EPS) * ng_ref[...]
        o_ref[0] = o * (1.0 - lam_init)

    @pl.when(j == 0)
    def _():
        attend(CTX)

    @pl.when(j > 0)
    def _():
        attend(k_ref.shape[1])


def _diff(qr, kr, a, lam_vecs, ng, lam_init):
    nb, s, _ = qr.shape
    dw = 2 * DIFF_DH
    vblk = (2 * GLA_HEADS * GLA_DK + 4 * HALF) // dw
    return pl.pallas_call(
        functools.partial(_diff_kernel, lam_init=lam_init),
        out_shape=jax.ShapeDtypeStruct((nb, s, HALF), F32),
        grid=(nb, DIFF_HEADS, s // ROW_TILE),
        in_specs=[pl.BlockSpec((1, ROW_TILE, dw), lambda b, h, j: (b, j, h)),
                  pl.BlockSpec((1, s, dw), lambda b, h, j: (b, 0, h)),
                  pl.BlockSpec((1, s, DIFF_DV), lambda b, h, j: (b, 0, vblk + h)),
                  pl.BlockSpec((8, DIFF_DH), lambda b, h, j: (0, 0)),
                  pl.BlockSpec((1, DIFF_DV), lambda b, h, j: (0, 0))],
        out_specs=pl.BlockSpec((1, ROW_TILE, DIFF_DV), lambda b, h, j: (b, j, h)),
        scratch_shapes=[pltpu.VMEM((s, DIFF_DV), BF16)],
        compiler_params=_cparams(("parallel", "parallel", "arbitrary")),
        name="diff_attn",
    )(qr, kr, a, lam_vecs, ng)


def _moe_kernel(h_ref, ml_ref, mc_ref, wr_ref, br_ref, wg_ref, wu_ref, wd_ref, lng_ref, lnb_ref, o_ref,
                a_ref, comb_ref, acc_ref, *, tiles_per_seq):
    t = pl.program_id(0)
    e = pl.program_id(1)
    tm = h_ref.shape[0]
    lane = lax.broadcasted_iota(jnp.int32, (1, 128), 1).astype(F32)
    d = D_MODEL

    def mod(col):
        row = (t % tiles_per_seq) * tm + lax.broadcasted_iota(jnp.int32, (tm, 1), 0)
        return jnp.where(row < CTX, mc_ref[0, :, col * d:(col + 1) * d], ml_ref[0, :, col * d:(col + 1) * d])

    @pl.when(e == 0)
    def _():
        a = h_ref[...] * (1.0 + mod(1)) + mod(0)
        a_ref[...] = a.astype(BF16)
        acc_ref[...] = jnp.zeros_like(acc_ref)
        logits = _hdot(a, wr_ref[...]) + br_ref[...]
        is_g = lane < N_GROUPS
        lg = jnp.where(is_g, logits, NEG)
        mg = jnp.max(lg, axis=-1, keepdims=True)
        g_sel = jnp.min(jnp.where(lg == mg, lane, 128.0), axis=-1, keepdims=True)
        p_grp = 1.0 / jnp.sum(jnp.where(is_g, jnp.exp(lg - mg), 0.0), axis=-1, keepdims=True)
        lo = N_GROUPS + g_sel * EPG
        in_grp = (lane >= lo) & (lane < lo + EPG)
        le = jnp.where(in_grp, logits, NEG)
        v1 = jnp.max(le, axis=-1, keepdims=True)
        i1 = jnp.min(jnp.where(le == v1, lane, 128.0), axis=-1, keepdims=True)
        le2 = jnp.where(lane == i1, NEG, le)
        v2 = jnp.max(le2, axis=-1, keepdims=True)
        i2 = jnp.min(jnp.where(le2 == v2, lane, 128.0), axis=-1, keepdims=True)
        e2 = jnp.exp(v2 - v1)
        w1 = p_grp / (1.0 + e2)
        w2 = p_grp * e2 / (1.0 + e2)
        comb_ref[...] = jnp.where(lane == i1, w1, jnp.where(lane == i2, w2, 0.0))

    a = a_ref[...]
    cw = jnp.sum(jnp.where(lane == (e + N_GROUPS).astype(F32), comb_ref[...], 0.0), axis=-1, keepdims=True)
    hg = jnp.dot(a, wg_ref[0].astype(BF16), preferred_element_type=F32)
    hu = jnp.dot(a, wu_ref[0].astype(BF16), preferred_element_type=F32)
    act = _silu(hg) * hu * cw
    acc_ref[...] += _bdot(act, wd_ref[0])

    @pl.when(e == N_EXPERTS - 1)
    def _():
        o_ref[...] = _layer_norm(ALPHA * h_ref[...] + mod(2) * acc_ref[...], lng_ref[...], lnb_ref[...])


def _moe_tile(s):
    k = 1
    while s // k > 1088 or s % k or (s // k) % 8:
        k += 1
    return s // k


def _moe(h, mods_l, w_router, b_router, w_gate, w_up, w_down, ln_g, ln_b):
    nb, s, d = h.shape
    tm = _moe_tile(s)
    tps = s // tm
    const = lambda t, e: (0, 0)
    out = pl.pallas_call(
        functools.partial(_moe_kernel, tiles_per_seq=tps),
        out_shape=jax.ShapeDtypeStruct((nb * s, d), F32),
        grid=(nb * tps, N_EXPERTS),
        in_specs=[pl.BlockSpec((tm, d), lambda t, e: (t, 0)),
                  pl.BlockSpec((1, 1, 3 * d), lambda t, e: (t // tps, 0, 1)),
                  pl.BlockSpec((1, 1, 3 * d), lambda t, e: (nb, 0, 1)),
                  pl.BlockSpec((d, 128), const),
                  pl.BlockSpec((1, 128), const),
                  pl.BlockSpec((1, d, D_EXPERT), lambda t, e: (e, 0, 0)),
                  pl.BlockSpec((1, d, D_EXPERT), lambda t, e: (e, 0, 0)),
                  pl.BlockSpec((1, D_EXPERT, d), lambda t, e: (e, 0, 0)),
                  pl.BlockSpec((1, d), const),
                  pl.BlockSpec((1, d), const)],
        out_specs=pl.BlockSpec((tm, d), lambda t, e: (t, 0)),
        scratch_shapes=[pltpu.VMEM((tm, d), BF16), pltpu.VMEM((tm, 128), F32), pltpu.VMEM((tm, d), F32)],
        compiler_params=_cparams(("parallel", "arbitrary")),
        name="moe",
    )(h.reshape(nb * s, d), mods_l, mods_l, w_router, b_router, w_gate, w_up, w_down,
      ln_g.astype(F32).reshape(1, d), ln_b.astype(F32).reshape(1, d))
    return out.reshape(nb, s, d)


def _pad_lanes(x, n):
    return jnp.pad(x, [(0, 0)] * (x.ndim - 1) + [(0, n - x.shape[-1])])


def kernel(x, c, ctx, c_ctx, mod_w, mod_b, ln_g, ln_b, even_w_in, even_w_out, s5_lam_re, s5_lam_im, s5_log_dt, s5_b_re, s5_b_im, s5_c_re, s5_c_im, s5_d, s5_glu_w, s5_glu_b, na_rpb, odd_w_in, odd_w_out, gla_gate_w, gla_gate_b, gla_norm_g, diff_lq1, diff_lk1, diff_lq2, diff_lk2, diff_norm_g, moe_w_grp, moe_b_grp, moe_w_exp, moe_b_exp, moe_w_gate, moe_w_up, moe_w_down):
    nb, t, d = x.shape
    assert d == D_MODEL and ctx.shape[1] == CTX and nb < 8 and t % (NA_QR * GRID_W) == 0
    s = CTX + t
    rows = t // GRID_W
    h = jnp.concatenate([ctx, x], axis=1).astype(F32)
    cc = jnp.zeros((8, d), F32).at[:nb].set(c.astype(F32)).at[nb].set(c_ctx.astype(F32))
    mods = _mods(cc, mod_w.astype(F32), mod_b.astype(F32)).reshape(DEPTH, 8, 1, 6 * d)
    rope_tabs = _rope_tables(s)
    for i in range(DEPTH):
        j = i // 2
        mods_l = mods[i]
        if i % 2 == 0:
            a = _inproj(h, mods_l, even_w_in[j].astype(BF16))
            tables = _s5_tables(s5_lam_re[j], s5_lam_im[j], s5_log_dt[j], s5_b_re[j], s5_b_im[j],
                                s5_c_re[j], s5_c_im[j], s5_d[j])
            y_a = _s5(a[..., :HALF], tables)
            y_b = _natten(a, _na_bias_tables(na_rpb[j], rows))
            h = _outproj(y_a, y_b, h, mods_l, even_w_out[j], ln_g[i, 0], ln_b[i, 0],
                         glu_w=s5_glu_w[j], glu_b=s5_glu_b[j])
        else:
            lam_init = 0.8 - 0.6 * math.exp(-0.3 * i)
            w = odd_w_in[j]
            nqk = 2 * GLA_HEADS * GLA_DK + 2 * HALF
            w = jnp.concatenate([w[:, :nqk], w[:, nqk + 2 * GLA_RANK:],
                                 _pad_lanes(w[:, nqk:nqk + 2 * GLA_RANK], 128)], axis=1).astype(BF16)
            a = _inproj(h, mods_l, w)
            gw = gla_gate_w[j].astype(F32)
            wg = jnp.zeros((2, 128, GLA_HEADS * GLA_DK), F32)
            for dd in range(2):
                wg = wg.at[dd, dd * GLA_RANK:(dd + 1) * GLA_RANK].set(gw[dd])
            wg = wg.reshape(2, 128, 2, 2 * GLA_DK).transpose(0, 2, 1, 3)
            bg = gla_gate_b[j].astype(F32).reshape(2, 2, 1, 2 * GLA_DK)
            y_a = _gla(a, wg, bg)
            qr, kr = _rope(a, rope_tabs)
            lam_vecs = jnp.zeros((8, DIFF_DH), F32).at[0].set(diff_lq1[j]).at[1].set(diff_lk1[j]) \
                .at[2].set(diff_lq2[j]).at[3].set(diff_lk2[j])
            y_b = _diff(qr, kr, a, lam_vecs, diff_norm_g[j].astype(F32).reshape(1, DIFF_DV), lam_init)
            h = _outproj(y_a, y_b, h, mods_l, odd_w_out[j], ln_g[i, 0], ln_b[i, 0],
                         proj=a, norm_g=gla_norm_g[j])
        w_router = _pad_lanes(jnp.concatenate([moe_w_grp[i], moe_w_exp[i]], axis=1).astype(F32), 128)
        b_router = _pad_lanes(jnp.concatenate([moe_b_grp[i], moe_b_exp[i]]).astype(F32)[None], 128)
        h = _moe(h, mods_l, w_router, b_router, moe_w_gate[i], moe_w_up[i], moe_w_down[i], ln_g[i, 1], ln_b[i, 1])
    return h[:, CTX:].astype(x.dtype)
```

```python
import functools
import math

import numpy as np
import jax
import jax.numpy as jnp
from jax import lax
from jax.experimental import pallas as pl
from jax.experimental.pallas import tpu as pltpu

F32 = jnp.float32
BF16 = jnp.bfloat16
HIGHEST = lax.Precision.HIGHEST

D_MODEL = 1024
DEPTH = 4
GRID_W = 64
CTX = 256
HALF = D_MODEL // 2
S5_H = 16
S5_G = HALF // S5_H
S5_P = 64
S5_L = 16
NA_HEADS = 8
NA_DH = 64
NA_WR = 8
NA_WC = 16
NA_QR = 4
NA_KR = 12
GLA_HEADS = 4
GLA_DK = 64
GLA_DV = 128
GLA_RANK = 16
GLA_TAU = 16.0
GLA_CHUNK = 64
DIFF_HEADS = 4
DIFF_DV = 128
DIFF_DH = 64
N_GROUPS = 4
EPG = 8
N_EXPERTS = N_GROUPS * EPG
D_EXPERT = D_MODEL // 4
ROPE_BASE = 10000.0
EPS = 1e-5
ALPHA = (2.0 * DEPTH) ** 0.25
EVEN_IN = 4 * HALF
ODD_IN_PAD = 3200
NEG = -1e30
LOG2E = math.log2(math.e)
ROW_TILE = 256
VMEM_LIMIT = 56 * 1024 * 1024


def _cparams(sem):
    return pltpu.CompilerParams(dimension_semantics=sem, vmem_limit_bytes=VMEM_LIMIT)


def _bdot(a, b):
    return jnp.dot(a.astype(BF16), b.astype(BF16), preferred_element_type=F32)


def _bdot_nt(a, b):
    return lax.dot_general(a.astype(BF16), b.astype(BF16), (((1,), (1,)), ((), ())),
                           preferred_element_type=F32)


def _hdot(a, b):
    return jnp.dot(a, b, precision=HIGHEST, preferred_element_type=F32)


def _layer_norm(x, g, b):
    mu = jnp.mean(x, axis=-1, keepdims=True)
    xc = x - mu
    var = jnp.mean(xc * xc, axis=-1, keepdims=True)
    return xc * lax.rsqrt(var + EPS) * g + b


def _silu(x):
    return x * jax.nn.sigmoid(x)


def _mods_kernel(cc_ref, w_ref, b_ref, o_ref):
    o_ref[0] = _hdot(_silu(cc_ref[...]), w_ref[0]) + b_ref[0]


def _mods(cc, mod_w, mod_b):
    nblk = 6
    return pl.pallas_call(
        _mods_kernel,
        out_shape=jax.ShapeDtypeStruct((DEPTH, 8, 6 * D_MODEL), F32),
        grid=(DEPTH, nblk),
        in_specs=[pl.BlockSpec((8, D_MODEL), lambda l, j: (0, 0)),
                  pl.BlockSpec((1, D_MODEL, D_MODEL), lambda l, j: (l, 0, j)),
                  pl.BlockSpec((1, 1, D_MODEL), lambda l, j: (l, 0, j))],
        out_specs=pl.BlockSpec((1, 8, D_MODEL), lambda l, j: (l, 0, j)),
        compiler_params=_cparams(("arbitrary", "arbitrary")),
        name="mods",
    )(cc, mod_w, mod_b.reshape(DEPTH, 1, 6 * D_MODEL))


def _mod_row(nbatch):
    return lambda b, j: jnp.where(j == 0, nbatch, b)


def _inproj_kernel(h_ref, sh_ref, sc_ref, w_ref, o_ref):
    a = h_ref[0] * (1.0 + sc_ref[0]) + sh_ref[0]
    o_ref[0] = _bdot(a, w_ref[...])


def _inproj(h, mods_l, w):
    nb, s, _ = h.shape
    n = w.shape[1]
    row = _mod_row(nb)
    return pl.pallas_call(
        _inproj_kernel,
        out_shape=jax.ShapeDtypeStruct((nb, s, n), F32),
        grid=(nb, s // ROW_TILE),
        in_specs=[pl.BlockSpec((1, ROW_TILE, D_MODEL), lambda b, j: (b, j, 0)),
                  pl.BlockSpec((1, 1, D_MODEL), lambda b, j: (row(b, j), 0, 0)),
                  pl.BlockSpec((1, 1, D_MODEL), lambda b, j: (row(b, j), 0, 1)),
                  pl.BlockSpec((D_MODEL, n), lambda b, j: (0, 0))],
        out_specs=pl.BlockSpec((1, ROW_TILE, n), lambda b, j: (b, j, 0)),
        compiler_params=_cparams(("parallel", "parallel")),
        name="inproj",
    )(h, mods_l, mods_l, w)


def _s5_tables(lam_re, lam_im, log_dt, b_re, b_im, c_re, c_im, d_skip):
    L, H, P, G = S5_L, S5_H, S5_P, S5_G
    lam_re, lam_im = lam_re.astype(F32), lam_im.astype(F32)
    dt = jnp.exp(log_dt.astype(F32))[..., None]
    mag = jnp.exp(lam_re * dt)
    a_re = mag * jnp.cos(lam_im * dt)
    a_im = mag * jnp.sin(lam_im * dt)
    den = lam_re * lam_re + lam_im * lam_im
    f_re = ((a_re - 1.0) * lam_re + a_im * lam_im) / den
    f_im = (a_im * lam_re - (a_re - 1.0) * lam_im) / den
    b_re, b_im = b_re.astype(F32), b_im.astype(F32)
    bb_re = f_re[..., None] * b_re - f_im[..., None] * b_im
    bb_im = f_re[..., None] * b_im + f_im[..., None] * b_re
    tau = jnp.arange(L + 1, dtype=F32)[:, None, None, None]
    mag_t = jnp.exp(lam_re[None] * dt[None] * tau)
    ang_t = lam_im[None] * dt[None] * tau
    p_re = mag_t * jnp.cos(ang_t)
    p_im = mag_t * jnp.sin(ang_t)
    c_re, c_im = c_re.astype(F32), c_im.astype(F32)
    e_re = c_re[None] * p_re[:, :, :, None, :] - c_im[None] * p_im[:, :, :, None, :]
    e_im = c_re[None] * p_im[:, :, :, None, :] + c_im[None] * p_re[:, :, :, None, :]
    k = (jnp.einsum('tdgjp,dgpi->tdgji', e_re, bb_re, precision=HIGHEST)
         - jnp.einsum('tdgjp,dgpi->tdgji', e_im, bb_im, precision=HIGHEST))
    s_i, t_i = np.meshgrid(np.arange(L), np.arange(L), indexing='ij')
    kf = k[np.clip(t_i - s_i, 0, L), 0] * jnp.asarray(t_i >= s_i, F32)[..., None, None, None]
    kb = k[np.clip(s_i - t_i, 0, L), 1] * jnp.asarray(s_i >= t_i, F32)[..., None, None, None]
    skip = (jnp.asarray(s_i == t_i, F32)[..., None, None, None]
            * (jnp.eye(H, dtype=F32)[None] * d_skip.astype(F32).reshape(G, 1, H))[None, None])
    intra = (kf + kb + skip).transpose(2, 0, 4, 1, 3).reshape(G, L * H, L * H)

    def state_in(d, pw):
        pr, pi = p_re[pw, d], p_im[pw, d]
        re = pr[..., None] * bb_re[d][None] - pi[..., None] * bb_im[d][None]
        im = pr[..., None] * bb_im[d][None] + pi[..., None] * bb_re[d][None]
        return jnp.concatenate([re, im], axis=2).transpose(1, 0, 3, 2).reshape(G, L * H, 2 * P)

    sf = state_in(0, np.arange(L)[::-1].copy())
    sb = state_in(1, np.arange(L))
    w1 = jnp.concatenate([intra, sf, sb], axis=-1)

    def state_out(d, pw):
        er, ei = e_re[pw, d], e_im[pw, d]
        return jnp.concatenate([er, -ei], axis=-1).transpose(1, 3, 0, 2).reshape(G, 2 * P, L * H)

    w2 = jnp.concatenate([state_out(0, np.arange(1, L + 1)),
                          state_out(1, np.arange(L, 0, -1))], axis=1)
    ar, ai = p_re[L], p_im[L]
    rows = []
    for d in range(2):
        rows += [jnp.concatenate([ar[d], ar[d]], -1), jnp.concatenate([-ai[d], ai[d]], -1),
                 jnp.concatenate([ai[d], -ai[d]], -1)]
    rows += [jnp.zeros_like(rows[0])] * 2
    al = jnp.stack(rows, axis=1)
    return w1.astype(BF16), w2.astype(BF16), al


def _s5_kernel(x_ref, w1_ref, w2_ref, al_ref, y_ref, xw_ref, sw_ref, hh_ref, *, nch, nctx):
    lh = S5_L * S5_H
    p2 = 2 * S5_P
    xw_ref[...] = jnp.dot(x_ref[0], w1_ref[0], preferred_element_type=F32)
    sw_ref[:, 0:p2] = pltpu.roll(xw_ref[:, lh:lh + p2], S5_P, 1)
    sw_ref[:, p2:2 * p2] = pltpu.roll(xw_ref[:, lh + p2:lh + 2 * p2], S5_P, 1)
    al = al_ref[0]
    a1f, a2f, a3f, a1b, a2b, a3b = [al[i:i + 1] for i in range(6)]

    def body(k, carry):
        vf, wf, vb, wb = carry
        kb = jnp.where(k < nctx, nctx - 1 - k, nch + nctx - 1 - k)
        rf = pl.ds(pl.multiple_of(k * 8, 8), 8)
        rb = pl.ds(pl.multiple_of(kb * 8, 8), 8)
        hh_ref[rf, 0:p2] = vf
        hh_ref[rb, p2:2 * p2] = vb
        s_f = xw_ref[rf, lh:lh + p2]
        s_fs = sw_ref[rf, 0:p2]
        s_b = xw_ref[rb, lh + p2:lh + 2 * p2]
        s_bs = sw_ref[rb, p2:2 * p2]
        return (vf * a1f + wf * a2f + s_f, wf * a1f + vf * a3f + s_fs,
                vb * a1b + wb * a2b + s_b, wb * a1b + vb * a3b + s_bs)

    zero = jnp.zeros((8, p2), F32)
    lax.fori_loop(0, nch, body, (zero,) * 4)
    y_ref[0] = xw_ref[:, 0:lh] + _bdot(hh_ref[...], w2_ref[0])


def _s5(u, tables):
    nb, s, _ = u.shape
    nch = s // S5_L
    lh = S5_L * S5_H
    r = nch * 8
    x = u.astype(BF16).reshape(nb, nch, S5_L, S5_G, S5_H).transpose(3, 1, 0, 2, 4)
    x = jnp.pad(x, ((0, 0), (0, 0), (0, 8 - nb), (0, 0), (0, 0))).reshape(S5_G, r, lh)
    w1, w2, al = tables
    y = pl.pallas_call(
        functools.partial(_s5_kernel, nch=nch, nctx=CTX // S5_L),
        out_shape=jax.ShapeDtypeStruct((S5_G, r, lh), F32),
        grid=(S5_G,),
        in_specs=[pl.BlockSpec((1, r, lh), lambda g: (g, 0, 0)),
                  pl.BlockSpec((1, lh, 2 * lh), lambda g: (g, 0, 0)),
                  pl.BlockSpec((1, lh, lh), lambda g: (g, 0, 0)),
                  pl.BlockSpec((1, 8, 2 * S5_P), lambda g: (g, 0, 0))],
        out_specs=pl.BlockSpec((1, r, lh), lambda g: (g, 0, 0)),
        scratch_shapes=[pltpu.VMEM((r, 2 * lh), F32), pltpu.VMEM((r, lh), F32), pltpu.VMEM((r, lh), F32)],
        compiler_params=_cparams(("parallel",)),
        name="s5",
    )(x, w1, w2, al)
    y = y.reshape(S5_G, nch, 8, S5_L, S5_H)[:, :, :nb]
    return y.transpose(2, 1, 3, 0, 4).reshape(nb, s, HALF)


def _na_bias_tables(rpb, rows):
    w = GRID_W
    nh, ndr, ndc = rpb.shape
    v = jnp.pad(rpb.astype(F32), ((0, 0), (0, 0), (w - NA_WC, 2 * w - ndc - (w - NA_WC))))
    skew = jnp.tile(v, (1, 1, w))[:, :, :w * (2 * w - 1)].reshape(nh, ndr, w, 2 * w - 1)
    tc = skew[..., w - 1:]
    valid, dr = [], []
    for r0 in (0, NA_QR, rows - NA_QR):
        k0 = int(np.clip(r0 - NA_WR // 2, 0, rows - NA_KR))
        rq = r0 + np.arange(NA_QR)[:, None, None, None]
        wq = np.arange(w)[None, :, None, None]
        kr = k0 + np.arange(NA_KR)[None, None, :, None]
        kc = np.arange(w)[None, None, None, :]
        rs = np.clip(rq - NA_WR // 2, 0, rows - NA_WR)
        cs = np.clip(wq - NA_WC // 2, 0, w - NA_WC)
        valid.append((kr >= rs) & (kr < rs + NA_WR) & (kc >= cs) & (kc < cs + NA_WC))
        dr.append(np.clip(kr - rq + NA_WR - 1, 0, ndr - 1)[:, 0, :, 0])
    dr = np.stack(dr).reshape(-1)
    bias = jnp.take(tc, jnp.asarray(dr), axis=1)
    bias = bias.reshape(nh, 3, NA_QR, NA_KR, w, w).transpose(0, 1, 2, 4, 3, 5)
    bias = jnp.where(jnp.asarray(np.stack(valid))[None], bias, NEG)
    return bias.reshape(nh, 3, NA_QR * w, NA_KR * w)


def _natten_kernel(q_ref, k_ref, v_ref, bias_ref, o_ref, qb_ref, kb_ref, vb_ref, *, rows):
    nq = NA_QR * GRID_W
    nk = NA_KR * GRID_W
    nblk = rows // NA_QR
    scale = NA_DH ** -0.5
    qb_ref[...] = (q_ref[0] * scale).astype(BF16)
    kb_ref[...] = k_ref[0].astype(BF16)
    vb_ref[...] = v_ref[0].astype(BF16)
    lane = lax.broadcasted_iota(jnp.int32, (1, 2 * NA_DH), 1)
    head_mask = [lane < NA_DH, lane >= NA_DH]
    kc = kb_ref[0:CTX, :]
    vc = vb_ref[0:CTX, :]

    def attend(q, extra):
        outs = []
        for h in range(2):
            qm = jnp.where(head_mask[h], q, jnp.zeros_like(q))
            s_c = _bdot_nt(qm, kc)
            m = jnp.max(s_c, axis=-1, keepdims=True)
            if extra is not None:
                k_loc, v_loc, bias = extra
                s_l = _bdot_nt(qm, k_loc) + bias[h]
                m = jnp.maximum(m, jnp.max(s_l, axis=-1, keepdims=True))
                p_l = jnp.exp(s_l - m)
            p_c = jnp.exp(s_c - m)
            den = jnp.sum(p_c, axis=-1, keepdims=True)
            acc = _bdot(p_c, vc)
            if extra is not None:
                den = den + jnp.sum(p_l, axis=-1, keepdims=True)
                acc = acc + _bdot(p_l, v_loc)
            outs.append(acc / den)
        return jnp.where(head_mask[0], outs[0], outs[1])

    o_ref[0, 0:CTX, :] = attend(qb_ref[0:CTX, :], None)

    def body(i, carry):
        r0 = i * NA_QR
        k0 = jnp.clip(r0 - NA_WR // 2, 0, rows - NA_KR)
        pat = jnp.where(i == 0, 0, jnp.where(i == nblk - 1, 2, 1))
        qs = pl.multiple_of(CTX + r0 * GRID_W, GRID_W)
        ks = pl.multiple_of(CTX + k0 * GRID_W, GRID_W)
        q = qb_ref[pl.ds(qs, nq), :]
        k_loc = kb_ref[pl.ds(ks, nk), :]
        v_loc = vb_ref[pl.ds(ks, nk), :]
        bias = [bias_ref[h, pat] for h in range(2)]
        o_ref[0, pl.ds(qs, nq), :] = attend(q, (k_loc, v_loc, bias))
        return carry

    lax.fori_loop(0, nblk, body, 0)


def _natten(a, bias):
    nb, s, _ = a.shape
    rows = (s - CTX) // GRID_W
    pw = 2 * NA_DH
    nq, nk = NA_QR * GRID_W, NA_KR * GRID_W
    hb = HALF // pw
    return pl.pallas_call(
        functools.partial(_natten_kernel, rows=rows),
        out_shape=jax.ShapeDtypeStruct((nb, s, HALF), F32),
        grid=(nb, hb),
        in_specs=[pl.BlockSpec((1, s, pw), lambda b, h: (b, 0, hb + h)),
                  pl.BlockSpec((1, s, pw), lambda b, h: (b, 0, 2 * hb + h)),
                  pl.BlockSpec((1, s, pw), lambda b, h: (b, 0, 3 * hb + h)),
                  pl.BlockSpec((2, 3, nq, nk), lambda b, h: (h, 0, 0, 0))],
        out_specs=pl.BlockSpec((1, s, pw), lambda b, h: (b, 0, h)),
        scratch_shapes=[pltpu.VMEM((s, pw), BF16)] * 3,
        compiler_params=_cparams(("parallel", "parallel")),
        name="natten",
    )(a, a, a, bias)


def _gelu_tanh(x):
    return 0.5 * x * (1.0 + jnp.tanh(math.sqrt(2.0 / math.pi) * (x + 0.044715 * (x * x * x))))


def _out_kernel(ya_ref, yb_ref, h_ref, g_ref, e0_ref, e1_ref, wo_ref, lng_ref, lnb_ref, o_ref, *, glu):
    if glu:
        z = _bdot(_gelu_tanh(ya_ref[0]), e0_ref[...]) + e1_ref[...]
        ya = z[:, :HALF] * jax.nn.sigmoid(z[:, HALF:])
    else:
        parts = []
        for hd in range(GLA_HEADS):
            cols = slice(hd * GLA_DV, (hd + 1) * GLA_DV)
            o = ya_ref[0, :, cols]
            o = o * lax.rsqrt(jnp.mean(o * o, axis=-1, keepdims=True) + EPS) * e1_ref[:, cols]
            parts.append(o * _silu(e0_ref[0, :, cols]))
        ya = jnp.concatenate(parts, axis=1)
    y = _bdot(ya, wo_ref[0:HALF, :]) + _bdot(yb_ref[0], wo_ref[HALF:, :])
    o_ref[0] = _layer_norm(ALPHA * h_ref[0] + g_ref[0] * y, lng_ref[...], lnb_ref[...])


def _outproj(ya, yb, h, mods_l, w_out, ln_g, ln_b, *, glu_w=None, glu_b=None, proj=None, norm_g=None):
    nb, s, _ = h.shape
    row = _mod_row(nb)
    glu = glu_w is not None
    tile = lambda b, j: (b, j, 0)
    const = lambda b, j: (0, 0)
    if glu:
        extra_specs = [pl.BlockSpec((HALF, D_MODEL), const), pl.BlockSpec((1, D_MODEL), const)]
        extra = [glu_w.astype(BF16), glu_b.astype(F32).reshape(1, D_MODEL)]
    else:
        rblk = (2 * GLA_HEADS * GLA_DK + HALF) // HALF
        extra_specs = [pl.BlockSpec((1, ROW_TILE, HALF), lambda b, j: (b, j, rblk)),
                       pl.BlockSpec((1, HALF), const)]
        extra = [proj, norm_g.astype(F32).reshape(1, HALF)]
    in_specs = [pl.BlockSpec((1, ROW_TILE, HALF), tile),
                pl.BlockSpec((1, ROW_TILE, HALF), tile),
                pl.BlockSpec((1, ROW_TILE, D_MODEL), tile),
                pl.BlockSpec((1, 1, D_MODEL), lambda b, j: (row(b, j), 0, 2))] + extra_specs + [
                pl.BlockSpec((D_MODEL, D_MODEL), const), pl.BlockSpec((1, D_MODEL), const),
                pl.BlockSpec((1, D_MODEL), const)]
    return pl.pallas_call(
        functools.partial(_out_kernel, glu=glu),
        out_shape=jax.ShapeDtypeStruct((nb, s, D_MODEL), F32),
        grid=(nb, s // ROW_TILE),
        in_specs=in_specs,
        out_specs=pl.BlockSpec((1, ROW_TILE, D_MODEL), tile),
        compiler_params=_cparams(("parallel", "parallel")),
        name="outproj_glu" if glu else "outproj_gate",
    )(ya, yb, h, mods_l, *extra, w_out.astype(BF16), ln_g.astype(F32).reshape(1, D_MODEL),
      ln_b.astype(F32).reshape(1, D_MODEL))


def _log_sigmoid(x):
    return jnp.minimum(x, 0.0) - jnp.log1p(jnp.exp(-jnp.abs(x)))


def _gla_kernel(q_ref, k_ref, v_ref, z_ref, wg_ref, bg_ref, o_ref, gf_ref, gb_ref, ob_ref, st_ref, *, nch):
    c = GLA_CHUNK
    dk2 = 2 * GLA_DK
    nctx = CTX // c
    z = z_ref[0]
    gf_ref[...] = _log_sigmoid(_hdot(z, wg_ref[0, 0]) + bg_ref[0, 0]) * (1.0 / GLA_TAU)
    gb_ref[...] = _log_sigmoid(_hdot(z, wg_ref[1, 0]) + bg_ref[1, 0]) * (1.0 / GLA_TAU)
    ri = lax.broadcasted_iota(jnp.int32, (c, c), 0)
    ci = lax.broadcasted_iota(jnp.int32, (c, c), 1)
    lane = lax.broadcasted_iota(jnp.int32, (1, dk2), 1)
    head_mask = [lane < GLA_DK, lane >= GLA_DK]
    ones = jnp.ones((c, dk2), F32)
    scale = GLA_DK ** -0.5

    st_ref[...] = jnp.zeros_like(st_ref)

    def chunk(direction, n):
        g_ref = gf_ref if direction == 0 else gb_ref
        keep = (ci <= ri) if direction == 0 else (ci >= ri)
        rs = pl.ds(pl.multiple_of(n * c, c), c)
        g = g_ref[rs, :]
        gc = _hdot(keep.astype(F32), g)
        gend = gc[c - 1:c, :] if direction == 0 else gc[0:1, :]
        q = q_ref[0, rs, :] * scale
        k = k_ref[0, rs, :]
        qd = q * jnp.exp(gc)
        kd = k * jnp.exp(-gc)
        kdec = k * jnp.exp(gend - gc)
        dcol = jnp.exp(lax.dot_general(g, ones, (((0,), (0,)), ((), ())), precision=HIGHEST,
                                       preferred_element_type=F32))
        kdec_t = kdec.T.astype(BF16)
        for h in range(2):
            qm = jnp.where(head_mask[h], qd, 0.0).astype(BF16)
            att = jnp.where(keep, _bdot_nt(qm, kd), 0.0)
            vh = v_ref[0, rs, h * GLA_DV:(h + 1) * GLA_DV].astype(BF16)
            st = st_ref[direction, h]
            o = _bdot(att, vh) + _bdot(qm, st)
            st_ref[direction, h] = dcol * st + jnp.dot(kdec_t, vh, preferred_element_type=F32)
            cols = slice(h * GLA_DV, (h + 1) * GLA_DV)
            if direction == 0:
                o_ref[0, rs, cols] = o
            else:
                ob_ref[rs, cols] = o

    def body(i, carry):
        chunk(0, i)
        chunk(1, jnp.where(i < nctx, nctx - 1 - i, nch + nctx - 1 - i))
        return carry

    lax.fori_loop(0, nch, body, 0)
    o_ref[0] += ob_ref[...]


def _gla(a, wg, bg):
    nb, s, _ = a.shape
    nch = s // GLA_CHUNK
    hp = GLA_HEADS // 2
    dk2, dv2 = 2 * GLA_DK, 2 * GLA_DV
    zblk = (2 * GLA_HEADS * GLA_DK + 5 * HALF) // 128
    return pl.pallas_call(
        functools.partial(_gla_kernel, nch=nch),
        out_shape=jax.ShapeDtypeStruct((nb, s, HALF), F32),
        grid=(nb, hp),
        in_specs=[pl.BlockSpec((1, s, dk2), lambda b, h: (b, 0, h)),
                  pl.BlockSpec((1, s, dk2), lambda b, h: (b, 0, hp + h)),
                  pl.BlockSpec((1, s, dv2), lambda b, h: (b, 0, hp + h)),
                  pl.BlockSpec((1, s, 128), lambda b, h: (b, 0, zblk)),
                  pl.BlockSpec((2, 1, 128, dk2), lambda b, h: (0, h, 0, 0)),
                  pl.BlockSpec((2, 1, 1, dk2), lambda b, h: (0, h, 0, 0))],
        out_specs=pl.BlockSpec((1, s, dv2), lambda b, h: (b, 0, h)),
        scratch_shapes=[pltpu.VMEM((s, dk2), F32), pltpu.VMEM((s, dk2), F32), pltpu.VMEM((s, dv2), F32),
                        pltpu.VMEM((2, 2, dk2, GLA_DV), F32)],
        compiler_params=_cparams(("parallel", "parallel")),
        name="gla",
    )(a, a, a, a, wg, bg)


def _rope_tables(s):
    t = np.arange(s - CTX)
    n = DIFF_DH // 4
    inv = ROPE_BASE ** (-np.arange(n, dtype=np.float64) / n)
    ang_r = (t // GRID_W)[:, None] * inv[None]
    ang_c = (t % GRID_W)[:, None] * inv[None]
    ang = np.concatenate([ang_r, ang_r, ang_c, ang_c], axis=1)
    ang = np.concatenate([ang, ang], axis=1)
    cos = np.concatenate([np.ones((CTX, 128)), np.cos(ang)], axis=0)
    sin = np.concatenate([np.zeros((CTX, 128)), np.sin(ang)], axis=0)
    first = ((np.arange(128) // n) % 2 == 0)[None]
    sin_up = np.where(first, -sin, 0.0)
    sin_dn = np.where(first, 0.0, sin)
    f = lambda z: jnp.asarray(z, F32)
    return f(cos), f(sin_up), f(sin_dn)


def _rope_kernel(q_ref, k_ref, cos_ref, su_ref, sd_ref, qo_ref, ko_ref):
    n = DIFF_DH // 4
    cos = jnp.concatenate([cos_ref[...]] * DIFF_HEADS, axis=1)
    su = jnp.concatenate([su_ref[...]] * DIFF_HEADS, axis=1)
    sd = jnp.concatenate([sd_ref[...]] * DIFF_HEADS, axis=1)

    def rot(x):
        w = x.shape[1]
        return x * cos + pltpu.roll(x, w - n, 1) * su + pltpu.roll(x, n, 1) * sd

    qo_ref[0] = rot(q_ref[0] * (DIFF_DH ** -0.5 * LOG2E)).astype(BF16)
    ko_ref[0] = rot(k_ref[0]).astype(BF16)


def _rope(a, tables):
    nb, s, _ = a.shape
    qblk = (2 * GLA_HEADS * GLA_DK + 2 * HALF) // HALF
    tile = lambda b, j: (b, j, 0)
    tab = pl.BlockSpec((ROW_TILE, 128), lambda b, j: (j, 0))
    return pl.pallas_call(
        _rope_kernel,
        out_shape=[jax.ShapeDtypeStruct((nb, s, HALF), BF16)] * 2,
        grid=(nb, s // ROW_TILE),
        in_specs=[pl.BlockSpec((1, ROW_TILE, HALF), lambda b, j: (b, j, qblk)),
                  pl.BlockSpec((1, ROW_TILE, HALF), lambda b, j: (b, j, qblk + 1)),
                  tab, tab, tab],
        out_specs=[pl.BlockSpec((1, ROW_TILE, HALF), tile)] * 2,
        compiler_params=_cparams(("parallel", "parallel")),
        name="rope",
    )(a, a, *tables)


def _diff_kernel(q_ref, k_ref, v_ref, lam_ref, ng_ref, o_ref, vb_ref, *, lam_init):
    j = pl.program_id(2)
    lane = lax.broadcasted_iota(jnp.int32, (1, 2 * DIFF_DH), 1)
    map_mask = [lane < DIFF_DH, lane >= DIFF_DH]
    lv = lam_ref[...]
    lam = (jnp.exp(jnp.sum(lv[0:1] * lv[1:2], axis=-1, keepdims=True))
           - jnp.exp(jnp.sum(lv[2:3] * lv[3:4], axis=-1, keepdims=True)) + lam_init)

    @pl.when(j == 0)
    def _():
        vb_ref[...] = v_ref[0].astype(BF16)

    def attend(nkeys):
        q = q_ref[0]
        k = k_ref[0, 0:nkeys, :]
        w = None
        for m in range(2):
            s = _bdot_nt(jnp.where(map_mask[m], q, jnp.zeros_like(q)), k)
            p = jnp.exp2(s - jnp.max(s, axis=-1, keepdims=True))
            inv = 1.0 / jnp.sum(p, axis=-1, keepdims=True)
            w = p * inv if m == 0 else w - p * (lam * inv)
        o = _bdot(w, vb_ref[0:nkeys, :])
        o = o * lax.rsqrt(jnp.mean(o * o, axis=-1, keepdims=True) + EPS) * ng_ref[...]
        o_ref[0] = o * (1.0 - lam_init)

    @pl.when(j == 0)
    def _():
        attend(CTX)

    @pl.when(j > 0)
    def _():
        attend(k_ref.shape[1])


def _diff(qr, kr, a, lam_vecs, ng, lam_init):
    nb, s, _ = qr.shape
    dw = 2 * DIFF_DH
    vblk = (2 * GLA_HEADS * GLA_DK + 4 * HALF) // dw
    return pl.pallas_call(
        functools.partial(_diff_kernel, lam_init=lam_init),
        out_shape=jax.ShapeDtypeStruct((nb, s, HALF), F32),
        grid=(nb, DIFF_HEADS, s // ROW_TILE),
        in_specs=[pl.BlockSpec((1, ROW_TILE, dw), lambda b, h, j: (b, j, h)),
                  pl.BlockSpec((1, s, dw), lambda b, h, j: (b, 0, h)),
                  pl.BlockSpec((1, s, DIFF_DV), lambda b, h, j: (b, 0, vblk + h)),
                  pl.BlockSpec((8, DIFF_DH), lambda b, h, j: (0, 0)),
                  pl.BlockSpec((1, DIFF_DV), lambda b, h, j: (0, 0))],
        out_specs=pl.BlockSpec((1, ROW_TILE, DIFF_DV), lambda b, h, j: (b, j, h)),
        scratch_shapes=[pltpu.VMEM((s, DIFF_DV), BF16)],
        compiler_params=_cparams(("parallel", "parallel", "arbitrary")),
        name="diff_attn",
    )(qr, kr, a, lam_vecs, ng)


def _moe_kernel(h_ref, ml_ref, mc_ref, wr_ref, br_ref, wg_ref, wu_ref, wd_ref, lng_ref, lnb_ref, o_ref,
                a_ref, comb_ref, acc_ref, *, tiles_per_seq):
    t = pl.program_id(0)
    e = pl.program_id(1)
    tm = h_ref.shape[0]
    lane = lax.broadcasted_iota(jnp.int32, (1, 128), 1).astype(F32)
    d = D_MODEL

    def mod(col):
        row = (t % tiles_per_seq) * tm + lax.broadcasted_iota(jnp.int32, (tm, 1), 0)
        return jnp.where(row < CTX, mc_ref[0, :, col * d:(col + 1) * d], ml_ref[0, :, col * d:(col + 1) * d])

    @pl.when(e == 0)
    def _():
        a = h_ref[...] * (1.0 + mod(1)) + mod(0)
        a_ref[...] = a.astype(BF16)
        acc_ref[...] = jnp.zeros_like(acc_ref)
        logits = _hdot(a, wr_ref[...]) + br_ref[...]
        is_g = lane < N_GROUPS
        lg = jnp.where(is_g, logits, NEG)
        mg = jnp.max(lg, axis=-1, keepdims=True)
        g_sel = jnp.min(jnp.where(lg == mg, lane, 128.0), axis=-1, keepdims=True)
        p_grp = 1.0 / jnp.sum(jnp.where(is_g, jnp.exp(lg - mg), 0.0), axis=-1, keepdims=True)
        lo = N_GROUPS + g_sel * EPG
        in_grp = (lane >= lo) & (lane < lo + EPG)
        le = jnp.where(in_grp, logits, NEG)
        v1 = jnp.max(le, axis=-1, keepdims=True)
        i1 = jnp.min(jnp.where(le == v1, lane, 128.0), axis=-1, keepdims=True)
        le2 = jnp.where(lane == i1, NEG, le)
        v2 = jnp.max(le2, axis=-1, keepdims=True)
        i2 = jnp.min(jnp.where(le2 == v2, lane, 128.0), axis=-1, keepdims=True)
        e2 = jnp.exp(v2 - v1)
        w1 = p_grp / (1.0 + e2)
        w2 = p_grp * e2 / (1.0 + e2)
        comb_ref[...] = jnp.where(lane == i1, w1, jnp.where(lane == i2, w2, 0.0))

    a = a_ref[...]
    cw = jnp.sum(jnp.where(lane == (e + N_GROUPS).astype(F32), comb_ref[...], 0.0), axis=-1, keepdims=True)
    hg = jnp.dot(a, wg_ref[0].astype(BF16), preferred_element_type=F32)
    hu = jnp.dot(a, wu_ref[0].astype(BF16), preferred_element_type=F32)
    act = _silu(hg) * hu * cw
    acc_ref[...] += _bdot(act, wd_ref[0])

    @pl.when(e == N_EXPERTS - 1)
    def _():
        o_ref[...] = _layer_norm(ALPHA * h_ref[...] + mod(2) * acc_ref[...], lng_ref[...], lnb_ref[...])


def _moe_tile(s):
    k = 1
    while s // k > 1088 or s % k or (s // k) % 8:
        k += 1
    return s // k


def _moe(h, mods_l, w_router, b_router, w_gate, w_up, w_down, ln_g, ln_b):
    nb, s, d = h.shape
    tm = _moe_tile(s)
    tps = s // tm
    const = lambda t, e: (0, 0)
    out = pl.pallas_call(
        functools.partial(_moe_kernel, tiles_per_seq=tps),
        out_shape=jax.ShapeDtypeStruct((nb * s, d), F32),
        grid=(nb * tps, N_EXPERTS),
        in_specs=[pl.BlockSpec((tm, d), lambda t, e: (t, 0)),
                  pl.BlockSpec((1, 1, 3 * d), lambda t, e: (t // tps, 0, 1)),
                  pl.BlockSpec((1, 1, 3 * d), lambda t, e: (nb, 0, 1)),
                  pl.BlockSpec((d, 128), const),
                  pl.BlockSpec((1, 128), const),
                  pl.BlockSpec((1, d, D_EXPERT), lambda t, e: (e, 0, 0)),
                  pl.BlockSpec((1, d, D_EXPERT), lambda t, e: (e, 0, 0)),
                  pl.BlockSpec((1, D_EXPERT, d), lambda t, e: (e, 0, 0)),
                  pl.BlockSpec((1, d), const),
                  pl.BlockSpec((1, d), const)],
        out_specs=pl.BlockSpec((tm, d), lambda t, e: (t, 0)),
        scratch_shapes=[pltpu.VMEM((tm, d), BF16), pltpu.VMEM((tm, 128), F32), pltpu.VMEM((tm, d), F32)],
        compiler_params=_cparams(("parallel", "arbitrary")),
        name="moe",
    )(h.reshape(nb * s, d), mods_l, mods_l, w_router, b_router, w_gate, w_up, w_down,
      ln_g.astype(F32).reshape(1, d), ln_b.astype(F32).reshape(1, d))
    return out.reshape(nb, s, d)


def _pad_lanes(x, n):
    return jnp.pad(x, [(0, 0)] * (x.ndim - 1) + [(0, n - x.shape[-1])])


def kernel(x, c, ctx, c_ctx, mod_w, mod_b, ln_g, ln_b, even_w_in, even_w_out, s5_lam_re, s5_lam_im, s5_log_dt, s5_b_re, s5_b_im, s5_c_re, s5_c_im, s5_d, s5_glu_w, s5_glu_b, na_rpb, odd_w_in, odd_w_out, gla_gate_w, gla_gate_b, gla_norm_g, diff_lq1, diff_lk1, diff_lq2, diff_lk2, diff_norm_g, moe_w_grp, moe_b_grp, moe_w_exp, moe_b_exp, moe_w_gate, moe_w_up, moe_w_down):
    nb, t, d = x.shape
    assert d == D_MODEL and ctx.shape[1] == CTX and nb < 8 and t % (NA_QR * GRID_W) == 0
    s = CTX + t
    rows = t // GRID_W
    h = jnp.concatenate([ctx, x], axis=1).astype(F32)
    cc = jnp.zeros((8, d), F32).at[:nb].set(c.astype(F32)).at[nb].set(c_ctx.astype(F32))
    mods = _mods(cc, mod_w.astype(F32), mod_b.astype(F32)).reshape(DEPTH, 8, 1, 6 * d)
    rope_tabs = _rope_tables(s)
    for i in range(DEPTH):
        j = i // 2
        mods_l = mods[i]
        if i % 2 == 0:
            a = _inproj(h, mods_l, even_w_in[j].astype(BF16))
            tables = _s5_tables(s5_lam_re[j], s5_lam_im[j], s5_log_dt[j], s5_b_re[j], s5_b_im[j],
                                s5_c_re[j], s5_c_im[j], s5_d[j])
            y_a = _s5(a[..., :HALF], tables)
            y_b = _natten(a, _na_bias_tables(na_rpb[j], rows))
            h = _outproj(y_a, y_b, h, mods_l, even_w_out[j], ln_g[i, 0], ln_b[i, 0],
                         glu_w=s5_glu_w[j], glu_b=s5_glu_b[j])
        else:
            lam_init = 0.8 - 0.6 * math.exp(-0.3 * i)
            w = odd_w_in[j]
            nqk = 2 * GLA_HEADS * GLA_DK + 2 * HALF
            w = jnp.concatenate([w[:, :nqk], w[:, nqk + 2 * GLA_RANK:],
                                 _pad_lanes(w[:, nqk:nqk + 2 * GLA_RANK], 128)], axis=1).astype(BF16)
            a = _inproj(h, mods_l, w)
            gw = gla_gate_w[j].astype(F32)
            wg = jnp.zeros((2, 128, GLA_HEADS * GLA_DK), F32)
            for dd in range(2):
                wg = wg.at[dd, dd * GLA_RANK:(dd + 1) * GLA_RANK].set(gw[dd])
            wg = wg.reshape(2, 128, 2, 2 * GLA_DK).transpose(0, 2, 1, 3)
            bg = gla_gate_b[j].astype(F32).reshape(2, 2, 1, 2 * GLA_DK)
            y_a = _gla(a, wg, bg)
            qr, kr = _rope(a, rope_tabs)
            lam_vecs = jnp.zeros((8, DIFF_DH), F32).at[0].set(diff_lq1[j]).at[1].set(diff_lk1[j]) \
                .at[2].set(diff_lq2[j]).at[3].set(diff_lk2[j])
            y_b = _diff(qr, kr, a, lam_vecs, diff_norm_g[j].astype(F32).reshape(1, DIFF_DV), lam_init)
            h = _outproj(y_a, y_b, h, mods_l, odd_w_out[j], ln_g[i, 0], ln_b[i, 0],
                         proj=a, norm_g=gla_norm_g[j])
        w_router = _pad_lanes(jnp.concatenate([moe_w_grp[i], moe_w_exp[i]], axis=1).astype(F32), 128)
        b_router = _pad_lanes(jnp.concatenate([moe_b_grp[i], moe_b_exp[i]]).astype(F32)[None], 128)
        h = _moe(h, mods_l, w_router, b_router, moe_w_gate[i], moe_w_up[i], moe_w_down[i], ln_g[i, 1], ln_b[i, 1])
    return h[:, CTX:].astype(x.dtype)
```

```python
import functools
import math

import numpy as np
import jax
import jax.numpy as jnp
from jax import lax
from jax.experimental import pallas as pl
from jax.experimental.pallas import tpu as pltpu

F32 = jnp.float32
BF16 = jnp.bfloat16
HIGHEST = lax.Precision.HIGHEST

D_MODEL = 1024
DEPTH = 4
GRID_W = 64
CTX = 256
HALF = D_MODEL // 2
S5_H = 16
S5_G = HALF // S5_H
S5_P = 64
S5_L = 16
NA_HEADS = 8
NA_DH = 64
NA_WR = 8
NA_WC = 16
NA_QR = 4
NA_KR = 12
GLA_HEADS = 4
GLA_DK = 64
GLA_DV = 128
GLA_RANK = 16
GLA_TAU = 16.0
GLA_CHUNK = 64
DIFF_HEADS = 4
DIFF_DV = 128
DIFF_DH = 64
N_GROUPS = 4
EPG = 8
N_EXPERTS = N_GROUPS * EPG
D_EXPERT = D_MODEL // 4
ROPE_BASE = 10000.0
EPS = 1e-5
ALPHA = (2.0 * DEPTH) ** 0.25
EVEN_IN = 4 * HALF
ODD_IN_PAD = 3200
NEG = -1e30
LOG2E = math.log2(math.e)
ROW_TILE = 256
VMEM_LIMIT = 56 * 1024 * 1024


def _cparams(sem):
    return pltpu.CompilerParams(dimension_semantics=sem, vmem_limit_bytes=VMEM_LIMIT)


def _bdot(a, b):
    return jnp.dot(a.astype(BF16), b.astype(BF16), preferred_element_type=F32)


def _bdot_nt(a, b):
    return lax.dot_general(a.astype(BF16), b.astype(BF16), (((1,), (1,)), ((), ())),
                           preferred_element_type=F32)


def _hdot(a, b):
    return jnp.dot(a, b, precision=HIGHEST, preferred_element_type=F32)


def _layer_norm(x, g, b):
    mu = jnp.mean(x, axis=-1, keepdims=True)
    xc = x - mu
    var = jnp.mean(xc * xc, axis=-1, keepdims=True)
    return xc * lax.rsqrt(var + EPS) * g + b


def _silu(x):
    return x * jax.nn.sigmoid(x)


def _mods_kernel(cc_ref, w_ref, b_ref, o_ref):
    o_ref[0] = _hdot(_silu(cc_ref[...]), w_ref[0]) + b_ref[0]


def _mods(cc, mod_w, mod_b):
    nblk = 6
    return pl.pallas_call(
        _mods_kernel,
        out_shape=jax.ShapeDtypeStruct((DEPTH, 8, 6 * D_MODEL), F32),
        grid=(DEPTH, nblk),
        in_specs=[pl.BlockSpec((8, D_MODEL), lambda l, j: (0, 0)),
                  pl.BlockSpec((1, D_MODEL, D_MODEL), lambda l, j: (l, 0, j)),
                  pl.BlockSpec((1, 1, D_MODEL), lambda l, j: (l, 0, j))],
        out_specs=pl.BlockSpec((1, 8, D_MODEL), lambda l, j: (l, 0, j)),
        compiler_params=_cparams(("arbitrary", "arbitrary")),
        name="mods",
    )(cc, mod_w, mod_b.reshape(DEPTH, 1, 6 * D_MODEL))


def _mod_row(nbatch):
    return lambda b, j: jnp.where(j == 0, nbatch, b)


def _inproj_kernel(h_ref, sh_ref, sc_ref, w_ref, o_ref):
    a = h_ref[0] * (1.0 + sc_ref[0]) + sh_ref[0]
    o_ref[0] = _bdot(a, w_ref[...])


def _inproj(h, mods_l, w):
    nb, s, _ = h.shape
    n = w.shape[1]
    row = _mod_row(nb)
    return pl.pallas_call(
        _inproj_kernel,
        out_shape=jax.ShapeDtypeStruct((nb, s, n), F32),
        grid=(nb, s // ROW_TILE),
        in_specs=[pl.BlockSpec((1, ROW_TILE, D_MODEL), lambda b, j: (b, j, 0)),
                  pl.BlockSpec((1, 1, D_MODEL), lambda b, j: (row(b, j), 0, 0)),
                  pl.BlockSpec((1, 1, D_MODEL), lambda b, j: (row(b, j), 0, 1)),
                  pl.BlockSpec((D_MODEL, n), lambda b, j: (0, 0))],
        out_specs=pl.BlockSpec((1, ROW_TILE, n), lambda b, j: (b, j, 0)),
        compiler_params=_cparams(("parallel", "parallel")),
        name="inproj",
    )(h, mods_l, mods_l, w)


def _s5_tables(lam_re, lam_im, log_dt, b_re, b_im, c_re, c_im, d_skip):
    L, H, P, G = S5_L, S5_H, S5_P, S5_G
    lam_re, lam_im = lam_re.astype(F32), lam_im.astype(F32)
    dt = jnp.exp(log_dt.astype(F32))[..., None]
    mag = jnp.exp(lam_re * dt)
    a_re = mag * jnp.cos(lam_im * dt)
    a_im = mag * jnp.sin(lam_im * dt)
    den = lam_re * lam_re + lam_im * lam_im
    f_re = ((a_re - 1.0) * lam_re + a_im * lam_im) / den
    f_im = (a_im * lam_re - (a_re - 1.0) * lam_im) / den
    b_re, b_im = b_re.astype(F32), b_im.astype(F32)
    bb_re = f_re[..., None] * b_re - f_im[..., None] * b_im
    bb_im = f_re[..., None] * b_im + f_im[..., None] * b_re
    tau = jnp.arange(L + 1, dtype=F32)[:, None, None, None]
    mag_t = jnp.exp(lam_re[None] * dt[None] * tau)
    ang_t = lam_im[None] * dt[None] * tau
    p_re = mag_t * jnp.cos(ang_t)
    p_im = mag_t * jnp.sin(ang_t)
    c_re, c_im = c_re.astype(F32), c_im.astype(F32)
    e_re = c_re[None] * p_re[:, :, :, None, :] - c_im[None] * p_im[:, :, :, None, :]
    e_im = c_re[None] * p_im[:, :, :, None, :] + c_im[None] * p_re[:, :, :, None, :]
    k = (jnp.einsum('tdgjp,dgpi->tdgji', e_re, bb_re, precision=HIGHEST)
         - jnp.einsum('tdgjp,dgpi->tdgji', e_im, bb_im, precision=HIGHEST))
    s_i, t_i = np.meshgrid(np.arange(L), np.arange(L), indexing='ij')
    kf = k[np.clip(t_i - s_i, 0, L), 0] * jnp.asarray(t_i >= s_i, F32)[..., None, None, None]
    kb = k[np.clip(s_i - t_i, 0, L), 1] * jnp.asarray(s_i >= t_i, F32)[..., None, None, None]
    skip = (jnp.asarray(s_i == t_i, F32)[..., None, None, None]
            * (jnp.eye(H, dtype=F32)[None] * d_skip.astype(F32).reshape(G, 1, H))[None, None])
    intra = (kf + kb + skip).transpose(2, 0, 4, 1, 3).reshape(G, L * H, L * H)

    def state_in(d, pw):
        pr, pi = p_re[pw, d], p_im[pw, d]
        re = pr[..., None] * bb_re[d][None] - pi[..., None] * bb_im[d][None]
        im = pr[..., None] * bb_im[d][None] + pi[..., None] * bb_re[d][None]
        return jnp.concatenate([re, im], axis=2).transpose(1, 0, 3, 2).reshape(G, L * H, 2 * P)

    sf = state_in(0, np.arange(L)[::-1].copy())
    sb = state_in(1, np.arange(L))
    w1 = jnp.concatenate([intra, sf, sb], axis=-1)

    def state_out(d, pw):
        er, ei = e_re[pw, d], e_im[pw, d]
        return jnp.concatenate([er, -ei], axis=-1).transpose(1, 3, 0, 2).reshape(G, 2 * P, L * H)

    w2 = jnp.concatenate([state_out(0, np.arange(1, L + 1)),
                          state_out(1, np.arange(L, 0, -1))], axis=1)
    ar, ai = p_re[L], p_im[L]
    rows = []
    for d in range(2):
        rows += [jnp.concatenate([ar[d], ar[d]], -1), jnp.concatenate([-ai[d], ai[d]], -1),
                 jnp.concatenate([ai[d], -ai[d]], -1)]
    rows += [jnp.zeros_like(rows[0])] * 2
    al = jnp.stack(rows, axis=1)
    return w1.astype(BF16), w2.astype(BF16), al


def _s5_kernel(x_ref, w1_ref, w2_ref, al_ref, y_ref, xw_ref, sw_ref, hh_ref, *, nch, nctx):
    lh = S5_L * S5_H
    p2 = 2 * S5_P
    xw_ref[...] = jnp.dot(x_ref[0], w1_ref[0], preferred_element_type=F32)
    sw_ref[:, 0:p2] = pltpu.roll(xw_ref[:, lh:lh + p2], S5_P, 1)
    sw_ref[:, p2:2 * p2] = pltpu.roll(xw_ref[:, lh + p2:lh + 2 * p2], S5_P, 1)
    al = al_ref[0]
    a1f, a2f, a3f, a1b, a2b, a3b = [al[i:i + 1] for i in range(6)]

    def body(k, carry):
        vf, wf, vb, wb = carry
        kb = jnp.where(k < nctx, nctx - 1 - k, nch + nctx - 1 - k)
        rf = pl.ds(pl.multiple_of(k * 8, 8), 8)
        rb = pl.ds(pl.multiple_of(kb * 8, 8), 8)
        hh_ref[rf, 0:p2] = vf
        hh_ref[rb, p2:2 * p2] = vb
        s_f = xw_ref[rf, lh:lh + p2]
        s_fs = sw_ref[rf, 0:p2]
        s_b = xw_ref[rb, lh + p2:lh + 2 * p2]
        s_bs = sw_ref[rb, p2:2 * p2]
        return (vf * a1f + wf * a2f + s_f, wf * a1f + vf * a3f + s_fs,
                vb * a1b + wb * a2b + s_b, wb * a1b + vb * a3b + s_bs)

    zero = jnp.zeros((8, p2), F32)
    lax.fori_loop(0, nch, body, (zero,) * 4)
    y_ref[0] = xw_ref[:, 0:lh] + _bdot(hh_ref[...], w2_ref[0])


def _s5(u, tables):
    nb, s, _ = u.shape
    nch = s // S5_L
    lh = S5_L * S5_H
    r = nch * 8
    x = u.astype(BF16).reshape(nb, nch, S5_L, S5_G, S5_H).transpose(3, 1, 0, 2, 4)
    x = jnp.pad(x, ((0, 0), (0, 0), (0, 8 - nb), (0, 0), (0, 0))).reshape(S5_G, r, lh)
    w1, w2, al = tables
    y = pl.pallas_call(
        functools.partial(_s5_kernel, nch=nch, nctx=CTX // S5_L),
        out_shape=jax.ShapeDtypeStruct((S5_G, r, lh), F32),
        grid=(S5_G,),
        in_specs=[pl.BlockSpec((1, r, lh), lambda g: (g, 0, 0)),
                  pl.BlockSpec((1, lh, 2 * lh), lambda g: (g, 0, 0)),
                  pl.BlockSpec((1, lh, lh), lambda g: (g, 0, 0)),
                  pl.BlockSpec((1, 8, 2 * S5_P), lambda g: (g, 0, 0))],
        out_specs=pl.BlockSpec((1, r, lh), lambda g: (g, 0, 0)),
        scratch_shapes=[pltpu.VMEM((r, 2 * lh), F32), pltpu.VMEM((r, lh), F32), pltpu.VMEM((r, lh), F32)],
        compiler_params=_cparams(("parallel",)),
        name="s5",
    )(x, w1, w2, al)
    y = y.reshape(S5_G, nch, 8, S5_L, S5_H)[:, :, :nb]
    return y.transpose(2, 1, 3, 0, 4).reshape(nb, s, HALF)


def _na_bias_tables(rpb, rows):
    w = GRID_W
    nh, ndr, ndc = rpb.shape
    v = jnp.pad(rpb.astype(F32), ((0, 0), (0, 0), (w - NA_WC, 2 * w - ndc - (w - NA_WC))))
    skew = jnp.tile(v, (1, 1, w))[:, :, :w * (2 * w - 1)].reshape(nh, ndr, w, 2 * w - 1)
    tc = skew[..., w - 1:]
    valid, dr = [], []
    for r0 in (0, NA_QR, rows - NA_QR):
        k0 = int(np.clip(r0 - NA_WR // 2, 0, rows - NA_KR))
        rq = r0 + np.arange(NA_QR)[:, None, None, None]
        wq = np.arange(w)[None, :, None, None]
        kr = k0 + np.arange(NA_KR)[None, None, :, None]
        kc = np.arange(w)[None, None, None, :]
        rs = np.clip(rq - NA_WR // 2, 0, rows - NA_WR)
        cs = np.clip(wq - NA_WC // 2, 0, w - NA_WC)
        valid.append((kr >= rs) & (kr < rs + NA_WR) & (kc >= cs) & (kc < cs + NA_WC))
        dr.append(np.clip(kr - rq + NA_WR - 1, 0, ndr - 1)[:, 0, :, 0])
    dr = np.stack(dr).reshape(-1)
    bias = jnp.take(tc, jnp.asarray(dr), axis=1)
    bias = bias.reshape(nh, 3, NA_QR, NA_KR, w, w).transpose(0, 1, 2, 4, 3, 5)
    bias = jnp.where(jnp.asarray(np.stack(valid))[None], bias, NEG)
    return bias.reshape(nh, 3, NA_QR * w, NA_KR * w)


def _natten_kernel(q_ref, k_ref, v_ref, bias_ref, o_ref, qb_ref, kb_ref, vb_ref, *, rows):
    nq = NA_QR * GRID_W
    nk = NA_KR * GRID_W
    nblk = rows // NA_QR
    scale = NA_DH ** -0.5
    qb_ref[...] = (q_ref[0] * scale).astype(BF16)
    kb_ref[...] = k_ref[0].astype(BF16)
    vb_ref[...] = v_ref[0].astype(BF16)
    lane = lax.broadcasted_iota(jnp.int32, (1, 2 * NA_DH), 1)
    head_mask = [lane < NA_DH, lane >= NA_DH]
    kc = kb_ref[0:CTX, :]
    vc = vb_ref[0:CTX, :]

    def attend(q, extra):
        outs = []
        for h in range(2):
            qm = jnp.where(head_mask[h], q, jnp.zeros_like(q))
            s_c = _bdot_nt(qm, kc)
            m = jnp.max(s_c, axis=-1, keepdims=True)
            if extra is not None:
                k_loc, v_loc, bias = extra
                s_l = _bdot_nt(qm, k_loc) + bias[h]
                m = jnp.maximum(m, jnp.max(s_l, axis=-1, keepdims=True))
                p_l = jnp.exp(s_l - m)
            p_c = jnp.exp(s_c - m)
            den = jnp.sum(p_c, axis=-1, keepdims=True)
            acc = _bdot(p_c, vc)
            if extra is not None:
                den = den + jnp.sum(p_l, axis=-1, keepdims=True)
                acc = acc + _bdot(p_l, v_loc)
            outs.append(acc / den)
        return jnp.where(head_mask[0], outs[0], outs[1])

    o_ref[0, 0:CTX, :] = attend(qb_ref[0:CTX, :], None)

    def body(i, carry):
        r0 = i * NA_QR
        k0 = jnp.clip(r0 - NA_WR // 2, 0, rows - NA_KR)
        pat = jnp.where(i == 0, 0, jnp.where(i == nblk - 1, 2, 1))
        qs = pl.multiple_of(CTX + r0 * GRID_W, GRID_W)
        ks = pl.multiple_of(CTX + k0 * GRID_W, GRID_W)
        q = qb_ref[pl.ds(qs, nq), :]
        k_loc = kb_ref[pl.ds(ks, nk), :]
        v_loc = vb_ref[pl.ds(ks, nk), :]
        bias = [bias_ref[h, pat] for h in range(2)]
        o_ref[0, pl.ds(qs, nq), :] = attend(q, (k_loc, v_loc, bias))
        return carry

    lax.fori_loop(0, nblk, body, 0)


def _natten(a, bias):
    nb, s, _ = a.shape
    rows = (s - CTX) // GRID_W
    pw = 2 * NA_DH
    nq, nk = NA_QR * GRID_W, NA_KR * GRID_W
    hb = HALF // pw
    return pl.pallas_call(
        functools.partial(_natten_kernel, rows=rows),
        out_shape=jax.ShapeDtypeStruct((nb, s, HALF), F32),
        grid=(nb, hb),
        in_specs=[pl.BlockSpec((1, s, pw), lambda b, h: (b, 0, hb + h)),
                  pl.BlockSpec((1, s, pw), lambda b, h: (b, 0, 2 * hb + h)),
                  pl.BlockSpec((1, s, pw), lambda b, h: (b, 0, 3 * hb + h)),
                  pl.BlockSpec((2, 3, nq, nk), lambda b, h: (h, 0, 0, 0))],
        out_specs=pl.BlockSpec((1, s, pw), lambda b, h: (b, 0, h)),
        scratch_shapes=[pltpu.VMEM((s, pw), BF16)] * 3,
        compiler_params=_cparams(("parallel", "parallel")),
        name="natten",
    )(a, a, a, bias)


def _gelu_tanh(x):
    return 0.5 * x * (1.0 + jnp.tanh(math.sqrt(2.0 / math.pi) * (x + 0.044715 * (x * x * x))))


def _out_kernel(ya_ref, yb_ref, h_ref, g_ref, e0_ref, e1_ref, wo_ref, lng_ref, lnb_ref, o_ref, *, glu):
    if glu:
        z = _bdot(_gelu_tanh(ya_ref[0]), e0_ref[...]) + e1_ref[...]
        ya = z[:, :HALF] * jax.nn.sigmoid(z[:, HALF:])
    else:
        parts = []
        for hd in range(GLA_HEADS):
            cols = slice(hd * GLA_DV, (hd + 1) * GLA_DV)
            o = ya_ref[0, :, cols]
            o = o * lax.rsqrt(jnp.mean(o * o, axis=-1, keepdims=True) + EPS) * e1_ref[:, cols]
            parts.append(o * _silu(e0_ref[0, :, cols]))
        ya = jnp.concatenate(parts, axis=1)
    y = _bdot(ya, wo_ref[0:HALF, :]) + _bdot(yb_ref[0], wo_ref[HALF:, :])
    o_ref[0] = _layer_norm(ALPHA * h_ref[0] + g_ref[0] * y, lng_ref[...], lnb_ref[...])


def _outproj(ya, yb, h, mods_l, w_out, ln_g, ln_b, *, glu_w=None, glu_b=None, proj=None, norm_g=None):
    nb, s, _ = h.shape
    row = _mod_row(nb)
    glu = glu_w is not None
    tile = lambda b, j: (b, j, 0)
    const = lambda b, j: (0, 0)
    if glu:
        extra_specs = [pl.BlockSpec((HALF, D_MODEL), const), pl.BlockSpec((1, D_MODEL), const)]
        extra = [glu_w.astype(BF16), glu_b.astype(F32).reshape(1, D_MODEL)]
    else:
        rblk = (2 * GLA_HEADS * GLA_DK + HALF) // HALF
        extra_specs = [pl.BlockSpec((1, ROW_TILE, HALF), lambda b, j: (b, j, rblk)),
                       pl.BlockSpec((1, HALF), const)]
        extra = [proj, norm_g.astype(F32).reshape(1, HALF)]
    in_specs = [pl.BlockSpec((1, ROW_TILE, HALF), tile),
                pl.BlockSpec((1, ROW_TILE, HALF), tile),
                pl.BlockSpec((1, ROW_TILE, D_MODEL), tile),
                pl.BlockSpec((1, 1, D_MODEL), lambda b, j: (row(b, j), 0, 2))] + extra_specs + [
                pl.BlockSpec((D_MODEL, D_MODEL), const), pl.BlockSpec((1, D_MODEL), const),
                pl.BlockSpec((1, D_MODEL), const)]
    return pl.pallas_call(
        functools.partial(_out_kernel, glu=glu),
        out_shape=jax.ShapeDtypeStruct((nb, s, D_MODEL), F32),
        grid=(nb, s // ROW_TILE),
        in_specs=in_specs,
        out_specs=pl.BlockSpec((1, ROW_TILE, D_MODEL), tile),
        compiler_params=_cparams(("parallel", "parallel")),
        name="outproj_glu" if glu else "outproj_gate",
    )(ya, yb, h, mods_l, *extra, w_out.astype(BF16), ln_g.astype(F32).reshape(1, D_MODEL),
      ln_b.astype(F32).reshape(1, D_MODEL))


def _log_sigmoid(x):
    return jnp.minimum(x, 0.0) - jnp.log1p(jnp.exp(-jnp.abs(x)))


def _gla_kernel(q_ref, k_ref, v_ref, z_ref, wg_ref, bg_ref, o_ref, gf_ref, gb_ref, ob_ref, st_ref, *, nch):
    c = GLA_CHUNK
    dk2 = 2 * GLA_DK
    nctx = CTX // c
    z = z_ref[0]
    gf_ref[...] = _log_sigmoid(_hdot(z, wg_ref[0, 0]) + bg_ref[0, 0]) * (1.0 / GLA_TAU)
    gb_ref[...] = _log_sigmoid(_hdot(z, wg_ref[1, 0]) + bg_ref[1, 0]) * (1.0 / GLA_TAU)
    ri = lax.broadcasted_iota(jnp.int32, (c, c), 0)
    ci = lax.broadcasted_iota(jnp.int32, (c, c), 1)
    lane = lax.broadcasted_iota(jnp.int32, (1, dk2), 1)
    head_mask = [lane < GLA_DK, lane >= GLA_DK]
    ones = jnp.ones((c, dk2), F32)
    scale = GLA_DK ** -0.5

    st_ref[...] = jnp.zeros_like(st_ref)

    def chunk(direction, n):
        g_ref = gf_ref if direction == 0 else gb_ref
        keep = (ci <= ri) if direction == 0 else (ci >= ri)
        rs = pl.ds(pl.multiple_of(n * c, c), c)
        g = g_ref[rs, :]
        gc = _hdot(keep.astype(F32), g)
        gend = gc[c - 1:c, :] if direction == 0 else gc[0:1, :]
        q = q_ref[0, rs, :] * scale
        k = k_ref[0, rs, :]
        qd = q * jnp.exp(gc)
        kd = k * jnp.exp(-gc)
        kdec = k * jnp.exp(gend - gc)
        dcol = jnp.exp(lax.dot_general(g, ones, (((0,), (0,)), ((), ())), precision=HIGHEST,
                                       preferred_element_type=F32))
        kdec_t = kdec.T.astype(BF16)
        for h in range(2):
            qm = jnp.where(head_mask[h], qd, 0.0).astype(BF16)
            att = jnp.where(keep, _bdot_nt(qm, kd), 0.0)
            vh = v_ref[0, rs, h * GLA_DV:(h + 1) * GLA_DV].astype(BF16)
            st = st_ref[direction, h]
            o = _bdot(att, vh) + _bdot(qm, st)
            st_ref[direction, h] = dcol * st + jnp.dot(kdec_t, vh, preferred_element_type=F32)
            cols = slice(h * GLA_DV, (h + 1) * GLA_DV)
            if direction == 0:
                o_ref[0, rs, cols] = o
            else:
                ob_ref[rs, cols] = o

    def body(i, carry):
        chunk(0, i)
        chunk(1, jnp.where(i < nctx, nctx - 1 - i, nch + nctx - 1 - i))
        return carry

    lax.fori_loop(0, nch, body, 0)
    o_ref[0] += ob_ref[...]


def _gla(a, wg, bg):
    nb, s, _ = a.shape
    nch = s // GLA_CHUNK
    hp = GLA_HEADS // 2
    dk2, dv2 = 2 * GLA_DK, 2 * GLA_DV
    zblk = (2 * GLA_HEADS * GLA_DK + 5 * HALF) // 128
    return pl.pallas_call(
        functools.partial(_gla_kernel, nch=nch),
        out_shape=jax.ShapeDtypeStruct((nb, s, HALF), F32),
        grid=(nb, hp),
        in_specs=[pl.BlockSpec((1, s, dk2), lambda b, h: (b, 0, h)),
                  pl.BlockSpec((1, s, dk2), lambda b, h: (b, 0, hp + h)),
                  pl.BlockSpec((1, s, dv2), lambda b, h: (b, 0, hp + h)),
                  pl.BlockSpec((1, s, 128), lambda b, h: (b, 0, zblk)),
                  pl.BlockSpec((2, 1, 128, dk2), lambda b, h: (0, h, 0, 0)),
                  pl.BlockSpec((2, 1, 1, dk2), lambda b, h: (0, h, 0, 0))],
        out_specs=pl.BlockSpec((1, s, dv2), lambda b, h: (b, 0, h)),
        scratch_shapes=[pltpu.VMEM((s, dk2), F32), pltpu.VMEM((s, dk2), F32), pltpu.VMEM((s, dv2), F32),
                        pltpu.VMEM((2, 2, dk2, GLA_DV), F32)],
        compiler_params=_cparams(("parallel", "parallel")),
        name="gla",
    )(a, a, a, a, wg, bg)


def _rope_tables(s):
    t = np.arange(s - CTX)
    n = DIFF_DH // 4
    inv = ROPE_BASE ** (-np.arange(n, dtype=np.float64) / n)
    ang_r = (t // GRID_W)[:, None] * inv[None]
    ang_c = (t % GRID_W)[:, None] * inv[None]
    ang = np.concatenate([ang_r, ang_r, ang_c, ang_c], axis=1)
    ang = np.concatenate([ang, ang], axis=1)
    cos = np.concatenate([np.ones((CTX, 128)), np.cos(ang)], axis=0)
    sin = np.concatenate([np.zeros((CTX, 128)), np.sin(ang)], axis=0)
    first = ((np.arange(128) // n) % 2 == 0)[None]
    sin_up = np.where(first, -sin, 0.0)
    sin_dn = np.where(first, 0.0, sin)
    f = lambda z: jnp.asarray(z, F32)
    return f(cos), f(sin_up), f(sin_dn)


def _rope_kernel(q_ref, k_ref, cos_ref, su_ref, sd_ref, qo_ref, ko_ref):
    n = DIFF_DH // 4
    cos = jnp.concatenate([cos_ref[...]] * DIFF_HEADS, axis=1)
    su = jnp.concatenate([su_ref[...]] * DIFF_HEADS, axis=1)
    sd = jnp.concatenate([sd_ref[...]] * DIFF_HEADS, axis=1)

    def rot(x):
        w = x.shape[1]
        return x * cos + pltpu.roll(x, w - n, 1) * su + pltpu.roll(x, n, 1) * sd

    qo_ref[0] = rot(q_ref[0] * (DIFF_DH ** -0.5 * LOG2E)).astype(BF16)
    ko_ref[0] = rot(k_ref[0]).astype(BF16)


def _rope(a, tables):
    nb, s, _ = a.shape
    qblk = (2 * GLA_HEADS * GLA_DK + 2 * HALF) // HALF
    tile = lambda b, j: (b, j, 0)
    tab = pl.BlockSpec((ROW_TILE, 128), lambda b, j: (j, 0))
    return pl.pallas_call(
        _rope_kernel,
        out_shape=[jax.ShapeDtypeStruct((nb, s, HALF), BF16)] * 2,
        grid=(nb, s // ROW_TILE),
        in_specs=[pl.BlockSpec((1, ROW_TILE, HALF), lambda b, j: (b, j, qblk)),
                  pl.BlockSpec((1, ROW_TILE, HALF), lambda b, j: (b, j, qblk + 1)),
                  tab, tab, tab],
        out_specs=[pl.BlockSpec((1, ROW_TILE, HALF), tile)] * 2,
        compiler_params=_cparams(("parallel", "parallel")),
        name="rope",
    )(a, a, *tables)


def _diff_kernel(q_ref, k_ref, v_ref, lam_ref, ng_ref, o_ref, vb_ref, *, lam_init):
    j = pl.program_id(2)
    lane = lax.broadcasted_iota(jnp.int32, (1, 2 * DIFF_DH), 1)
    map_mask = [lane < DIFF_DH, lane >= DIFF_DH]
    lv = lam_ref[...]
    lam = (jnp.exp(jnp.sum(lv[0:1] * lv[1:2], axis=-1, keepdims=True))
           - jnp.exp(jnp.sum(lv[2:3] * lv[3:4], axis=-1, keepdims=True)) + lam_init)

    @pl.when(j == 0)
    def _():
        vb_ref[...] = v_ref[0].astype(BF16)

    def attend(nkeys):
        q = q_ref[0]
        k = k_ref[0, 0:nkeys, :]
        w = None
        for m in range(2):
            s = _bdot_nt(jnp.where(map_mask[m], q, jnp.zeros_like(q)), k)
            p = jnp.exp2(s - jnp.max(s, axis=-1, keepdims=True))
            inv = 1.0 / jnp.sum(p, axis=-1, keepdims=True)
            w = p * inv if m == 0 else w - p * (lam * inv)
        o = _bdot(w, vb_ref[0:nkeys, :])
        o = o * lax.rsqrt(jnp.mean(o * o, axis=-1, keepdims=True) + EPS) * ng_ref[...]
        o_ref[0] = o * (1.0 - lam_init)

    @pl.when(j == 0)
    def _():
        attend(CTX)

    @pl.when(j > 0)
    def _():
        attend(k_ref.shape[1])


def _diff(qr, kr, a, lam_vecs, ng, lam_init):
    nb, s, _ = qr.shape
    dw = 2 * DIFF_DH
    vblk = (2 * GLA_HEADS * GLA_DK + 4 * HALF) // dw
    return pl.pallas_call(
        functools.partial(_diff_kernel, lam_init=lam_init),
        out_shape=jax.ShapeDtypeStruct((nb, s, HALF), F32),
        grid=(nb, DIFF_HEADS, s // ROW_TILE),
        in_specs=[pl.BlockSpec((1, ROW_TILE, dw), lambda b, h, j: (b, j, h)),
                  pl.BlockSpec((1, s, dw), lambda b, h, j: (b, 0, h)),
                  pl.BlockSpec((1, s, DIFF_DV), lambda b, h, j: (b, 0, vblk + h)),
                  pl.BlockSpec((8, DIFF_DH), lambda b, h, j: (0, 0)),
                  pl.BlockSpec((1, DIFF_DV), lambda b, h, j: (0, 0))],
        out_specs=pl.BlockSpec((1, ROW_TILE, DIFF_DV), lambda b, h, j: (b, j, h)),
        scratch_shapes=[pltpu.VMEM((s, DIFF_DV), BF16)],
        compiler_params=_cparams(("parallel", "parallel", "arbitrary")),
        name="diff_attn",
    )(qr, kr, a, lam_vecs, ng)


def _split3(x):
    hi = x.astype(BF16)
    r1 = x - hi.astype(F32)
    mid = r1.astype(BF16)
    lo = (r1 - mid.astype(F32)).astype(BF16)
    return hi, mid, lo


def _moe_kernel(h_ref, m0_ref, m1_ref, mc_ref, wr_ref, br_ref, wg_ref, wu_ref, wd_ref, lng_ref, lnb_ref, o_ref,
                as_ref, cs_ref, pg_ref, ps_ref, ys_ref, win_ref, *, seq, win):
    t = pl.program_id(0)
    e = pl.program_id(1)
    tm = h_ref.shape[0]
    lane = lax.broadcasted_iota(jnp.int32, (1, 128), 1).astype(F32)
    d = D_MODEL

    def mod(col):
        b0 = (t * tm) // seq
        grow = t * tm + lax.broadcasted_iota(jnp.int32, (tm, 1), 0)
        in_next = grow >= (b0 + 1) * seq
        local = grow - jnp.where(in_next, b0 + 1, b0) * seq
        cols = slice(col * d, (col + 1) * d)
        return jnp.where(local < CTX, mc_ref[0, :, cols],
                         jnp.where(in_next, m1_ref[0, :, cols], m0_ref[0, :, cols]))

    @pl.when(e == 0)
    def _():
        a = h_ref[...] * (1.0 + mod(1)) + mod(0)
        logits = _hdot(a, wr_ref[...]) + br_ref[...]
        is_g = lane < N_GROUPS
        lg = jnp.where(is_g, logits, NEG)
        mg = jnp.max(lg, axis=-1, keepdims=True)
        g_sel = jnp.min(jnp.where(lg == mg, lane, 128.0), axis=-1, keepdims=True)
        p_grp = 1.0 / jnp.sum(jnp.where(is_g, jnp.exp(lg - mg), 0.0), axis=-1, keepdims=True)
        lo = N_GROUPS + g_sel * EPG
        in_grp = (lane >= lo) & (lane < lo + EPG)
        le = jnp.where(in_grp, logits, NEG)
        v1 = jnp.max(le, axis=-1, keepdims=True)
        i1 = jnp.min(jnp.where(le == v1, lane, 128.0), axis=-1, keepdims=True)
        le2 = jnp.where(lane == i1, NEG, le)
        v2 = jnp.max(le2, axis=-1, keepdims=True)
        i2 = jnp.min(jnp.where(le2 == v2, lane, 128.0), axis=-1, keepdims=True)
        e2 = jnp.exp(v2 - v1)
        w1 = p_grp / (1.0 + e2)
        w2 = p_grp * e2 / (1.0 + e2)
        comb = jnp.where(lane == i1, w1, jnp.where(lane == i2, w2, 0.0))

        onehot = jnp.where(lane == g_sel, 1.0, 0.0).astype(BF16)
        ri = lax.broadcasted_iota(jnp.int32, (tm, tm), 0)
        ci = lax.broadcasted_iota(jnp.int32, (tm, tm), 1)
        ranks = jnp.dot(jnp.where(ci <= ri, 1.0, 0.0).astype(BF16), onehot, preferred_element_type=F32)
        cnt = ranks[tm - 1:tm, :]
        n = [jnp.sum(jnp.where(lane == float(g), cnt, 0.0), axis=-1, keepdims=True) for g in range(N_GROUPS)]
        off = [jnp.zeros_like(n[0]), n[0], n[0] + n[1], n[0] + n[1] + n[2]]
        off_vec = sum(jnp.where(lane == float(g), off[g], 0.0) for g in range(1, N_GROUPS))
        dest = jnp.sum(jnp.where(lane == g_sel, ranks + off_vec, 0.0), axis=-1, keepdims=True) - 1.0

        pos = lax.broadcasted_iota(jnp.int32, (1, tm), 1).astype(F32)
        ps_ref[...] = jnp.where(dest == pos, 1.0, 0.0).astype(BF16)
        hi = jnp.floor(dest * (1.0 / 32.0))
        digits = jnp.where(lane == 0.0, hi, jnp.where(lane == 1.0, dest - 32.0 * hi, 0.0)).astype(BF16)
        ri8 = lax.broadcasted_iota(jnp.int32, (8, 128), 0)
        ci8 = lax.broadcasted_iota(jnp.int32, (8, 128), 1)
        dig_t = _bdot_nt(jnp.where(ri8 == ci8, 1.0, 0.0), digits)
        dest_row = 32.0 * dig_t[0:1, :] + dig_t[1:2, :]
        rows = lax.broadcasted_iota(jnp.int32, (tm, 1), 0).astype(F32)
        pg = jnp.where(dest_row == rows, 1.0, 0.0).astype(BF16)
        pg_ref[...] = pg
        as_ref[...] = jnp.dot(pg, a.astype(BF16), preferred_element_type=F32).astype(BF16)
        c3 = _split3(comb)
        cs_ref[...] = (jnp.dot(pg, c3[0], preferred_element_type=F32)
                       + jnp.dot(pg, c3[1], preferred_element_type=F32)
                       + jnp.dot(pg, c3[2], preferred_element_type=F32))
        ys_ref[...] = jnp.zeros_like(ys_ref)
        for g in range(N_GROUPS):
            start = jnp.minimum((jnp.sum(off[g]).astype(jnp.int32) // 16) * 16, tm - win)
            end = jnp.sum(off[g] + n[g]).astype(jnp.int32)
            win_ref[g] = start
            win_ref[N_GROUPS + g] = jnp.where(end > start, (end - start + win - 1) // win, 0)

    g = e // EPG
    start = win_ref[g]
    sel = lane == (e + N_GROUPS).astype(F32)
    wg = wg_ref[0].astype(BF16)
    wu = wu_ref[0].astype(BF16)
    wd = wd_ref[0].astype(BF16)

    def window(w, carry):
        lo = start + w * win
        rs = pl.ds(pl.multiple_of(jnp.minimum(lo, tm - win), 16), win)
        x = as_ref[rs, :]
        row = jnp.minimum(lo, tm - win) + lax.broadcasted_iota(jnp.int32, (win, 1), 0)
        cw = jnp.sum(jnp.where(sel, cs_ref[rs, :], 0.0), axis=-1, keepdims=True)
        cw = jnp.where(row >= lo, cw, 0.0)
        hg = jnp.dot(x, wg, preferred_element_type=F32)
        hu = jnp.dot(x, wu, preferred_element_type=F32)
        act = (_silu(hg) * hu * cw).astype(BF16)
        ys_ref[rs, :] += jnp.dot(act, wd, preferred_element_type=F32)
        return carry

    lax.fori_loop(0, win_ref[N_GROUPS + g], window, 0)

    @pl.when(e == N_EXPERTS - 1)
    def _():
        y3 = _split3(ys_ref[...])
        ps = ps_ref[...]
        y = (jnp.dot(ps, y3[0], preferred_element_type=F32) + jnp.dot(ps, y3[1], preferred_element_type=F32)
             + jnp.dot(ps, y3[2], preferred_element_type=F32))
        o_ref[...] = _layer_norm(ALPHA * h_ref[...] + mod(2) * y, lng_ref[...], lnb_ref[...])


def _moe(h, mods_l, w_router, b_router, w_gate, w_up, w_down, ln_g, ln_b):
    nb, s, d = h.shape
    rows = nb * s
    tm = 1024 if rows % 1024 == 0 else 256
    win = tm // N_GROUPS + tm // 16
    nt = rows // tm
    const = lambda t, e: (0, 0)
    b0 = lambda t: (t * tm) // s
    out = pl.pallas_call(
        functools.partial(_moe_kernel, seq=s, win=win),
        out_shape=jax.ShapeDtypeStruct((rows, d), F32),
        grid=(nt, N_EXPERTS),
        in_specs=[pl.BlockSpec((tm, d), lambda t, e: (t, 0)),
                  pl.BlockSpec((1, 1, 3 * d), lambda t, e: (b0(t), 0, 1)),
                  pl.BlockSpec((1, 1, 3 * d), lambda t, e: (jnp.minimum(b0(t) + 1, nb - 1), 0, 1)),
                  pl.BlockSpec((1, 1, 3 * d), lambda t, e: (nb, 0, 1)),
                  pl.BlockSpec((d, 128), const),
                  pl.BlockSpec((1, 128), const),
                  pl.BlockSpec((1, d, D_EXPERT), lambda t, e: (e, 0, 0)),
                  pl.BlockSpec((1, d, D_EXPERT), lambda t, e: (e, 0, 0)),
                  pl.BlockSpec((1, D_EXPERT, d), lambda t, e: (e, 0, 0)),
                  pl.BlockSpec((1, d), const),
                  pl.BlockSpec((1, d), const)],
        out_specs=pl.BlockSpec((tm, d), lambda t, e: (t, 0)),
        scratch_shapes=[pltpu.VMEM((tm, d), BF16), pltpu.VMEM((tm, 128), F32), pltpu.VMEM((tm, tm), BF16),
                        pltpu.VMEM((tm, tm), BF16), pltpu.VMEM((tm, d), F32), pltpu.SMEM((2 * N_GROUPS,), jnp.int32)],
        compiler_params=_cparams(("parallel", "arbitrary")),
        name="moe",
    )(h.reshape(rows, d), mods_l, mods_l, mods_l, w_router, b_router, w_gate, w_up, w_down,
      ln_g.astype(F32).reshape(1, d), ln_b.astype(F32).reshape(1, d))
    return out.reshape(nb, s, d)


def _pad_lanes(x, n):
    return jnp.pad(x, [(0, 0)] * (x.ndim - 1) + [(0, n - x.shape[-1])])


def kernel(x, c, ctx, c_ctx, mod_w, mod_b, ln_g, ln_b, even_w_in, even_w_out, s5_lam_re, s5_lam_im, s5_log_dt, s5_b_re, s5_b_im, s5_c_re, s5_c_im, s5_d, s5_glu_w, s5_glu_b, na_rpb, odd_w_in, odd_w_out, gla_gate_w, gla_gate_b, gla_norm_g, diff_lq1, diff_lk1, diff_lq2, diff_lk2, diff_norm_g, moe_w_grp, moe_b_grp, moe_w_exp, moe_b_exp, moe_w_gate, moe_w_up, moe_w_down):
    nb, t, d = x.shape
    assert d == D_MODEL and ctx.shape[1] == CTX and nb < 8 and t % (NA_QR * GRID_W) == 0
    s = CTX + t
    rows = t // GRID_W
    h = jnp.concatenate([ctx, x], axis=1).astype(F32)
    cc = jnp.zeros((8, d), F32).at[:nb].set(c.astype(F32)).at[nb].set(c_ctx.astype(F32))
    mods = _mods(cc, mod_w.astype(F32), mod_b.astype(F32)).reshape(DEPTH, 8, 1, 6 * d)
    rope_tabs = _rope_tables(s)
    for i in range(DEPTH):
        j = i // 2
        mods_l = mods[i]
        if i % 2 == 0:
            a = _inproj(h, mods_l, even_w_in[j].astype(BF16))
            tables = _s5_tables(s5_lam_re[j], s5_lam_im[j], s5_log_dt[j], s5_b_re[j], s5_b_im[j],
                                s5_c_re[j], s5_c_im[j], s5_d[j])
            y_a = _s5(a[..., :HALF], tables)
            y_b = _natten(a, _na_bias_tables(na_rpb[j], rows))
            h = _outproj(y_a, y_b, h, mods_l, even_w_out[j], ln_g[i, 0], ln_b[i, 0],
                         glu_w=s5_glu_w[j], glu_b=s5_glu_b[j])
        else:
            lam_init = 0.8 - 0.6 * math.exp(-0.3 * i)
            w = odd_w_in[j]
            nqk = 2 * GLA_HEADS * GLA_DK + 2 * HALF
            w = jnp.concatenate([w[:, :nqk], w[:, nqk + 2 * GLA_RANK:],
                                 _pad_lanes(w[:, nqk:nqk + 2 * GLA_RANK], 128)], axis=1).astype(BF16)
            a = _inproj(h, mods_l, w)
            gw = gla_gate_w[j].astype(F32)
            wg = jnp.zeros((2, 128, GLA_HEADS * GLA_DK), F32)
            for dd in range(2):
                wg = wg.at[dd, dd * GLA_RANK:(dd + 1) * GLA_RANK].set(gw[dd])
            wg = wg.reshape(2, 128, 2, 2 * GLA_DK).transpose(0, 2, 1, 3)
            bg = gla_gate_b[j].astype(F32).reshape(2, 2, 1, 2 * GLA_DK)
            y_a = _gla(a, wg, bg)
            qr, kr = _rope(a, rope_tabs)
            lam_vecs = jnp.zeros((8, DIFF_DH), F32).at[0].set(diff_lq1[j]).at[1].set(diff_lk1[j]) \
                .at[2].set(diff_lq2[j]).at[3].set(diff_lk2[j])
            y_b = _diff(qr, kr, a, lam_vecs, diff_norm_g[j].astype(F32).reshape(1, DIFF_DV), lam_init)
            h = _outproj(y_a, y_b, h, mods_l, odd_w_out[j], ln_g[i, 0], ln_b[i, 0],
                         proj=a, norm_g=gla_norm_g[j])
        w_router = _pad_lanes(jnp.concatenate([moe_w_grp[i], moe_w_exp[i]], axis=1).astype(F32), 128)
        b_router = _pad_lanes(jnp.concatenate([moe_b_grp[i], moe_b_exp[i]]).astype(F32)[None], 128)
        h = _moe(h, mods_l, w_router, b_router, moe_w_gate[i], moe_w_up[i], moe_w_down[i], ln_g[i, 1], ln_b[i, 1])
    return h[:, CTX:].astype(x.dtype)
```

```python
import functools
import math

import numpy as np
import jax
import jax.numpy as jnp
from jax import lax
from jax.experimental import pallas as pl
from jax.experimental.pallas import tpu as pltpu

F32 = jnp.float32
BF16 = jnp.bfloat16
HIGHEST = lax.Precision.HIGHEST

D_MODEL = 1024
DEPTH = 4
GRID_W = 64
CTX = 256
HALF = D_MODEL // 2
S5_H = 16
S5_G = HALF // S5_H
S5_P = 64
S5_L = 16
NA_HEADS = 8
NA_DH = 64
NA_WR = 8
NA_WC = 16
NA_QR = 4
NA_KR = 12
GLA_HEADS = 4
GLA_DK = 64
GLA_DV = 128
GLA_RANK = 16
GLA_TAU = 16.0
GLA_CHUNK = 64
DIFF_HEADS = 4
DIFF_DV = 128
DIFF_DH = 64
DIFF_KB = 512
N_GROUPS = 4
EPG = 8
N_EXPERTS = N_GROUPS * EPG
D_EXPERT = D_MODEL // 4
MOE_EB = 4
ROPE_BASE = 10000.0
EPS = 1e-5
ALPHA = (2.0 * DEPTH) ** 0.25
EVEN_IN = 4 * HALF
ODD_IN_PAD = 3200
NEG = -1e30
LOG2E = math.log2(math.e)
ROW_TILE = 256
VMEM_LIMIT = 56 * 1024 * 1024


def _cparams(sem):
    return pltpu.CompilerParams(dimension_semantics=sem, vmem_limit_bytes=VMEM_LIMIT)


def _bdot(a, b):
    return jnp.dot(a.astype(BF16), b.astype(BF16), preferred_element_type=F32)


def _bdot_nt(a, b):
    return lax.dot_general(a.astype(BF16), b.astype(BF16), (((1,), (1,)), ((), ())),
                           preferred_element_type=F32)


def _hdot(a, b):
    return jnp.dot(a, b, precision=HIGHEST, preferred_element_type=F32)


def _layer_norm(x, g, b):
    mu = jnp.mean(x, axis=-1, keepdims=True)
    xc = x - mu
    var = jnp.mean(xc * xc, axis=-1, keepdims=True)
    return xc * lax.rsqrt(var + EPS) * g + b


def _silu(x):
    return x * jax.nn.sigmoid(x)


def _mods_kernel(cc_ref, w_ref, b_ref, o_ref):
    o_ref[0] = _hdot(_silu(cc_ref[...]), w_ref[0]) + b_ref[0]


def _mods(cc, mod_w, mod_b):
    nblk = 6
    return pl.pallas_call(
        _mods_kernel,
        out_shape=jax.ShapeDtypeStruct((DEPTH, 8, 6 * D_MODEL), F32),
        grid=(DEPTH, nblk),
        in_specs=[pl.BlockSpec((8, D_MODEL), lambda l, j: (0, 0)),
                  pl.BlockSpec((1, D_MODEL, D_MODEL), lambda l, j: (l, 0, j)),
                  pl.BlockSpec((1, 1, D_MODEL), lambda l, j: (l, 0, j))],
        out_specs=pl.BlockSpec((1, 8, D_MODEL), lambda l, j: (l, 0, j)),
        compiler_params=_cparams(("arbitrary", "arbitrary")),
        name="mods",
    )(cc, mod_w, mod_b.reshape(DEPTH, 1, 6 * D_MODEL))


def _mod_row(nbatch):
    return lambda b, j: jnp.where(j == 0, nbatch, b)


def _inproj_kernel(h_ref, sh_ref, sc_ref, w_ref, o_ref):
    a = h_ref[0] * (1.0 + sc_ref[0]) + sh_ref[0]
    o_ref[0] = _bdot(a, w_ref[...])


def _inproj(h, mods_l, w):
    nb, s, _ = h.shape
    n = w.shape[1]
    row = _mod_row(nb)
    return pl.pallas_call(
        _inproj_kernel,
        out_shape=jax.ShapeDtypeStruct((nb, s, n), F32),
        grid=(nb, s // ROW_TILE),
        in_specs=[pl.BlockSpec((1, ROW_TILE, D_MODEL), lambda b, j: (b, j, 0)),
                  pl.BlockSpec((1, 1, D_MODEL), lambda b, j: (row(b, j), 0, 0)),
                  pl.BlockSpec((1, 1, D_MODEL), lambda b, j: (row(b, j), 0, 1)),
                  pl.BlockSpec((D_MODEL, n), lambda b, j: (0, 0))],
        out_specs=pl.BlockSpec((1, ROW_TILE, n), lambda b, j: (b, j, 0)),
        compiler_params=_cparams(("parallel", "parallel")),
        name="inproj",
    )(h, mods_l, mods_l, w)


def _s5_tables(lam_re, lam_im, log_dt, b_re, b_im, c_re, c_im, d_skip):
    L, H, P, G = S5_L, S5_H, S5_P, S5_G
    lam_re, lam_im = lam_re.astype(F32), lam_im.astype(F32)
    dt = jnp.exp(log_dt.astype(F32))[..., None]
    mag = jnp.exp(lam_re * dt)
    a_re = mag * jnp.cos(lam_im * dt)
    a_im = mag * jnp.sin(lam_im * dt)
    den = lam_re * lam_re + lam_im * lam_im
    f_re = ((a_re - 1.0) * lam_re + a_im * lam_im) / den
    f_im = (a_im * lam_re - (a_re - 1.0) * lam_im) / den
    b_re, b_im = b_re.astype(F32), b_im.astype(F32)
    bb_re = f_re[..., None] * b_re - f_im[..., None] * b_im
    bb_im = f_re[..., None] * b_im + f_im[..., None] * b_re
    tau = jnp.arange(L + 1, dtype=F32)[:, None, None, None]
    mag_t = jnp.exp(lam_re[None] * dt[None] * tau)
    ang_t = lam_im[None] * dt[None] * tau
    p_re = mag_t * jnp.cos(ang_t)
    p_im = mag_t * jnp.sin(ang_t)
    c_re, c_im = c_re.astype(F32), c_im.astype(F32)
    e_re = c_re[None] * p_re[:, :, :, None, :] - c_im[None] * p_im[:, :, :, None, :]
    e_im = c_re[None] * p_im[:, :, :, None, :] + c_im[None] * p_re[:, :, :, None, :]
    k = (jnp.einsum('tdgjp,dgpi->tdgji', e_re, bb_re, precision=HIGHEST)
         - jnp.einsum('tdgjp,dgpi->tdgji', e_im, bb_im, precision=HIGHEST))
    s_i, t_i = np.meshgrid(np.arange(L), np.arange(L), indexing='ij')
    kf = k[np.clip(t_i - s_i, 0, L), 0] * jnp.asarray(t_i >= s_i, F32)[..., None, None, None]
    kb = k[np.clip(s_i - t_i, 0, L), 1] * jnp.asarray(s_i >= t_i, F32)[..., None, None, None]
    skip = (jnp.asarray(s_i == t_i, F32)[..., None, None, None]
            * (jnp.eye(H, dtype=F32)[None] * d_skip.astype(F32).reshape(G, 1, H))[None, None])
    intra = (kf + kb + skip).transpose(2, 0, 4, 1, 3).reshape(G, L * H, L * H)

    def state_in(d, pw):
        pr, pi = p_re[pw, d], p_im[pw, d]
        re = pr[..., None] * bb_re[d][None] - pi[..., None] * bb_im[d][None]
        im = pr[..., None] * bb_im[d][None] + pi[..., None] * bb_re[d][None]
        return jnp.concatenate([re, im], axis=2).transpose(1, 0, 3, 2).reshape(G, L * H, 2 * P)

    sf = state_in(0, np.arange(L)[::-1].copy())
    sb = state_in(1, np.arange(L))
    w1 = jnp.concatenate([intra, sf, sb], axis=-1)

    def state_out(d, pw):
        er, ei = e_re[pw, d], e_im[pw, d]
        return jnp.concatenate([er, -ei], axis=-1).transpose(1, 3, 0, 2).reshape(G, 2 * P, L * H)

    w2 = jnp.concatenate([state_out(0, np.arange(1, L + 1)),
                          state_out(1, np.arange(L, 0, -1))], axis=1)
    ar, ai = p_re[L], p_im[L]
    rows = []
    for d in range(2):
        rows += [jnp.concatenate([ar[d], ar[d]], -1), jnp.concatenate([-ai[d], ai[d]], -1),
                 jnp.concatenate([ai[d], -ai[d]], -1)]
    rows += [jnp.zeros_like(rows[0])] * 2
    al = jnp.stack(rows, axis=1)
    return w1.astype(BF16), w2.astype(BF16), al


def _s5_kernel(x_ref, w1_ref, w2_ref, al_ref, y_ref, xw_ref, sw_ref, hh_ref, *, nch, nctx):
    lh = S5_L * S5_H
    p2 = 2 * S5_P
    xw_ref[...] = jnp.dot(x_ref[0], w1_ref[0], preferred_element_type=F32)
    sw_ref[:, 0:p2] = pltpu.roll(xw_ref[:, lh:lh + p2], S5_P, 1)
    sw_ref[:, p2:2 * p2] = pltpu.roll(xw_ref[:, lh + p2:lh + 2 * p2], S5_P, 1)
    al = al_ref[0]
    a1f, a2f, a3f, a1b, a2b, a3b = [al[i:i + 1] for i in range(6)]

    def body(k, carry):
        vf, wf, vb, wb = carry
        kb = jnp.where(k < nctx, nctx - 1 - k, nch + nctx - 1 - k)
        rf = pl.ds(pl.multiple_of(k * 8, 8), 8)
        rb = pl.ds(pl.multiple_of(kb * 8, 8), 8)
        hh_ref[rf, 0:p2] = vf
        hh_ref[rb, p2:2 * p2] = vb
        s_f = xw_ref[rf, lh:lh + p2]
        s_fs = sw_ref[rf, 0:p2]
        s_b = xw_ref[rb, lh + p2:lh + 2 * p2]
        s_bs = sw_ref[rb, p2:2 * p2]
        return (vf * a1f + wf * a2f + s_f, wf * a1f + vf * a3f + s_fs,
                vb * a1b + wb * a2b + s_b, wb * a1b + vb * a3b + s_bs)

    zero = jnp.zeros((8, p2), F32)
    lax.fori_loop(0, nch, body, (zero,) * 4)
    y_ref[0] = xw_ref[:, 0:lh] + _bdot(hh_ref[...], w2_ref[0])


def _s5(u, tables):
    nb, s, _ = u.shape
    nch = s // S5_L
    lh = S5_L * S5_H
    r = nch * 8
    x = u.astype(BF16).reshape(nb, nch, S5_L, S5_G, S5_H).transpose(3, 1, 0, 2, 4)
    x = jnp.pad(x, ((0, 0), (0, 0), (0, 8 - nb), (0, 0), (0, 0))).reshape(S5_G, r, lh)
    w1, w2, al = tables
    y = pl.pallas_call(
        functools.partial(_s5_kernel, nch=nch, nctx=CTX // S5_L),
        out_shape=jax.ShapeDtypeStruct((S5_G, r, lh), F32),
        grid=(S5_G,),
        in_specs=[pl.BlockSpec((1, r, lh), lambda g: (g, 0, 0)),
                  pl.BlockSpec((1, lh, 2 * lh), lambda g: (g, 0, 0)),
                  pl.BlockSpec((1, lh, lh), lambda g: (g, 0, 0)),
                  pl.BlockSpec((1, 8, 2 * S5_P), lambda g: (g, 0, 0))],
        out_specs=pl.BlockSpec((1, r, lh), lambda g: (g, 0, 0)),
        scratch_shapes=[pltpu.VMEM((r, 2 * lh), F32), pltpu.VMEM((r, lh), F32), pltpu.VMEM((r, lh), F32)],
        compiler_params=_cparams(("parallel",)),
        name="s5",
    )(x, w1, w2, al)
    y = y.reshape(S5_G, nch, 8, S5_L, S5_H)[:, :, :nb]
    return y.transpose(2, 1, 3, 0, 4).reshape(nb, s, HALF)


def _na_bias_tables(rpb, rows):
    w = GRID_W
    nh, ndr, ndc = rpb.shape
    v = jnp.pad(rpb.astype(F32), ((0, 0), (0, 0), (w - NA_WC, 2 * w - ndc - (w - NA_WC))))
    skew = jnp.tile(v, (1, 1, w))[:, :, :w * (2 * w - 1)].reshape(nh, ndr, w, 2 * w - 1)
    tc = skew[..., w - 1:]
    valid, dr = [], []
    for r0 in (0, NA_QR, rows - NA_QR):
        k0 = int(np.clip(r0 - NA_WR // 2, 0, rows - NA_KR))
        rq = r0 + np.arange(NA_QR)[:, None, None, None]
        wq = np.arange(w)[None, :, None, None]
        kr = k0 + np.arange(NA_KR)[None, None, :, None]
        kc = np.arange(w)[None, None, None, :]
        rs = np.clip(rq - NA_WR // 2, 0, rows - NA_WR)
        cs = np.clip(wq - NA_WC // 2, 0, w - NA_WC)
        valid.append((kr >= rs) & (kr < rs + NA_WR) & (kc >= cs) & (kc < cs + NA_WC))
        dr.append(np.clip(kr - rq + NA_WR - 1, 0, ndr - 1)[:, 0, :, 0])
    dr = np.stack(dr).reshape(-1)
    bias = jnp.take(tc, jnp.asarray(dr), axis=1)
    bias = bias.reshape(nh, 3, NA_QR, NA_KR, w, w).transpose(0, 1, 2, 4, 3, 5)
    bias = jnp.where(jnp.asarray(np.stack(valid))[None], bias, NEG)
    return bias.reshape(nh, 3, NA_QR * w, NA_KR * w)


def _natten_kernel(q_ref, k_ref, v_ref, bias_ref, o_ref, qb_ref, kb_ref, vb_ref, *, rows):
    nq = NA_QR * GRID_W
    nk = NA_KR * GRID_W
    nblk = rows // NA_QR
    scale = NA_DH ** -0.5
    qb_ref[...] = (q_ref[0] * scale).astype(BF16)
    kb_ref[...] = k_ref[0].astype(BF16)
    vb_ref[...] = v_ref[0].astype(BF16)
    lane = lax.broadcasted_iota(jnp.int32, (1, 2 * NA_DH), 1)
    head_mask = [lane < NA_DH, lane >= NA_DH]
    kc = kb_ref[0:CTX, :]
    vc = vb_ref[0:CTX, :]

    def attend(q, extra):
        outs = []
        for h in range(2):
            qm = jnp.where(head_mask[h], q, jnp.zeros_like(q))
            s_c = _bdot_nt(qm, kc)
            m = jnp.max(s_c, axis=-1, keepdims=True)
            if extra is not None:
                k_loc, v_loc, bias = extra
                s_l = _bdot_nt(qm, k_loc) + bias[h]
                m = jnp.maximum(m, jnp.max(s_l, axis=-1, keepdims=True))
                p_l = jnp.exp(s_l - m)
            p_c = jnp.exp(s_c - m)
            den = jnp.sum(p_c, axis=-1, keepdims=True)
            acc = _bdot(p_c, vc)
            if extra is not None:
                den = den + jnp.sum(p_l, axis=-1, keepdims=True)
                acc = acc + _bdot(p_l, v_loc)
            outs.append(acc / den)
        return jnp.where(head_mask[0], outs[0], outs[1])

    o_ref[0, 0:CTX, :] = attend(qb_ref[0:CTX, :], None)

    def body(i, carry):
        r0 = i * NA_QR
        k0 = jnp.clip(r0 - NA_WR // 2, 0, rows - NA_KR)
        pat = jnp.where(i == 0, 0, jnp.where(i == nblk - 1, 2, 1))
        qs = pl.multiple_of(CTX + r0 * GRID_W, GRID_W)
        ks = pl.multiple_of(CTX + k0 * GRID_W, GRID_W)
        q = qb_ref[pl.ds(qs, nq), :]
        k_loc = kb_ref[pl.ds(ks, nk), :]
        v_loc = vb_ref[pl.ds(ks, nk), :]
        bias = [bias_ref[h, pat] for h in range(2)]
        o_ref[0, pl.ds(qs, nq), :] = attend(q, (k_loc, v_loc, bias))
        return carry

    lax.fori_loop(0, nblk, body, 0)


def _natten(a, bias):
    nb, s, _ = a.shape
    rows = (s - CTX) // GRID_W
    pw = 2 * NA_DH
    nq, nk = NA_QR * GRID_W, NA_KR * GRID_W
    hb = HALF // pw
    return pl.pallas_call(
        functools.partial(_natten_kernel, rows=rows),
        out_shape=jax.ShapeDtypeStruct((nb, s, HALF), F32),
        grid=(nb, hb),
        in_specs=[pl.BlockSpec((1, s, pw), lambda b, h: (b, 0, hb + h)),
                  pl.BlockSpec((1, s, pw), lambda b, h: (b, 0, 2 * hb + h)),
                  pl.BlockSpec((1, s, pw), lambda b, h: (b, 0, 3 * hb + h)),
                  pl.BlockSpec((2, 3, nq, nk), lambda b, h: (h, 0, 0, 0))],
        out_specs=pl.BlockSpec((1, s, pw), lambda b, h: (b, 0, h)),
        scratch_shapes=[pltpu.VMEM((s, pw), BF16)] * 3,
        compiler_params=_cparams(("parallel", "parallel")),
        name="natten",
    )(a, a, a, bias)


def _gelu_tanh(x):
    return 0.5 * x * (1.0 + jnp.tanh(math.sqrt(2.0 / math.pi) * (x + 0.044715 * (x * x * x))))


def _out_kernel(ya_ref, yb_ref, h_ref, g_ref, e0_ref, e1_ref, wo_ref, lng_ref, lnb_ref, o_ref, *, glu):
    if glu:
        z = _bdot(_gelu_tanh(ya_ref[0]), e0_ref[...]) + e1_ref[...]
        ya = z[:, :HALF] * jax.nn.sigmoid(z[:, HALF:])
    else:
        parts = []
        for hd in range(GLA_HEADS):
            cols = slice(hd * GLA_DV, (hd + 1) * GLA_DV)
            o = ya_ref[0, :, cols]
            o = o * lax.rsqrt(jnp.mean(o * o, axis=-1, keepdims=True) + EPS) * e1_ref[:, cols]
            parts.append(o * _silu(e0_ref[0, :, cols]))
        ya = jnp.concatenate(parts, axis=1)
    y = _bdot(ya, wo_ref[0:HALF, :]) + _bdot(yb_ref[0], wo_ref[HALF:, :])
    o_ref[0] = _layer_norm(ALPHA * h_ref[0] + g_ref[0] * y, lng_ref[...], lnb_ref[...])


def _outproj(ya, yb, h, mods_l, w_out, ln_g, ln_b, *, glu_w=None, glu_b=None, proj=None, norm_g=None):
    nb, s, _ = h.shape
    row = _mod_row(nb)
    glu = glu_w is not None
    tile = lambda b, j: (b, j, 0)
    const = lambda b, j: (0, 0)
    if glu:
        extra_specs = [pl.BlockSpec((HALF, D_MODEL), const), pl.BlockSpec((1, D_MODEL), const)]
        extra = [glu_w.astype(BF16), glu_b.astype(F32).reshape(1, D_MODEL)]
    else:
        rblk = (2 * GLA_HEADS * GLA_DK + HALF) // HALF
        extra_specs = [pl.BlockSpec((1, ROW_TILE, HALF), lambda b, j: (b, j, rblk)),
                       pl.BlockSpec((1, HALF), const)]
        extra = [proj, norm_g.astype(F32).reshape(1, HALF)]
    in_specs = [pl.BlockSpec((1, ROW_TILE, HALF), tile),
                pl.BlockSpec((1, ROW_TILE, HALF), tile),
                pl.BlockSpec((1, ROW_TILE, D_MODEL), tile),
                pl.BlockSpec((1, 1, D_MODEL), lambda b, j: (row(b, j), 0, 2))] + extra_specs + [
                pl.BlockSpec((D_MODEL, D_MODEL), const), pl.BlockSpec((1, D_MODEL), const),
                pl.BlockSpec((1, D_MODEL), const)]
    return pl.pallas_call(
        functools.partial(_out_kernel, glu=glu),
        out_shape=jax.ShapeDtypeStruct((nb, s, D_MODEL), F32),
        grid=(nb, s // ROW_TILE),
        in_specs=in_specs,
        out_specs=pl.BlockSpec((1, ROW_TILE, D_MODEL), tile),
        compiler_params=_cparams(("parallel", "parallel")),
        name="outproj_glu" if glu else "outproj_gate",
    )(ya, yb, h, mods_l, *extra, w_out.astype(BF16), ln_g.astype(F32).reshape(1, D_MODEL),
      ln_b.astype(F32).reshape(1, D_MODEL))


def _log_sigmoid(x):
    return jnp.minimum(x, 0.0) - jnp.log1p(jnp.exp(-jnp.abs(x)))


def _gla_kernel(q_ref, k_ref, v_ref, z_ref, wg_ref, bg_ref, o_ref, gf_ref, gb_ref, ob_ref, st_ref, *, nch):
    c = GLA_CHUNK
    dk2 = 2 * GLA_DK
    nctx = CTX // c
    z = z_ref[0]
    gf_ref[...] = _log_sigmoid(_hdot(z, wg_ref[0, 0]) + bg_ref[0, 0]) * (1.0 / GLA_TAU)
    gb_ref[...] = _log_sigmoid(_hdot(z, wg_ref[1, 0]) + bg_ref[1, 0]) * (1.0 / GLA_TAU)
    ri = lax.broadcasted_iota(jnp.int32, (c, c), 0)
    ci = lax.broadcasted_iota(jnp.int32, (c, c), 1)
    lane = lax.broadcasted_iota(jnp.int32, (1, dk2), 1)
    head_mask = [lane < GLA_DK, lane >= GLA_DK]
    ones = jnp.ones((c, dk2), F32)
    scale = GLA_DK ** -0.5

    st_ref[...] = jnp.zeros_like(st_ref)

    def chunk(direction, n):
        g_ref = gf_ref if direction == 0 else gb_ref
        keep = (ci <= ri) if direction == 0 else (ci >= ri)
        rs = pl.ds(pl.multiple_of(n * c, c), c)
        g = g_ref[rs, :]
        gc = _hdot(keep.astype(F32), g)
        gend = gc[c - 1:c, :] if direction == 0 else gc[0:1, :]
        q = q_ref[0, rs, :] * scale
        k = k_ref[0, rs, :]
        qd = q * jnp.exp(gc)
        kd = k * jnp.exp(-gc)
        kdec = k * jnp.exp(gend - gc)
        dcol = jnp.exp(lax.dot_general(g, ones, (((0,), (0,)), ((), ())), precision=HIGHEST,
                                       preferred_element_type=F32))
        kdec_t = kdec.T.astype(BF16)
        for h in range(2):
            qm = jnp.where(head_mask[h], qd, 0.0).astype(BF16)
            att = jnp.where(keep, _bdot_nt(qm, kd), 0.0)
            vh = v_ref[0, rs, h * GLA_DV:(h + 1) * GLA_DV].astype(BF16)
            st = st_ref[direction, h]
            o = _bdot(att, vh) + _bdot(qm, st)
            st_ref[direction, h] = dcol * st + jnp.dot(kdec_t, vh, preferred_element_type=F32)
            cols = slice(h * GLA_DV, (h + 1) * GLA_DV)
            if direction == 0:
                o_ref[0, rs, cols] = o
            else:
                ob_ref[rs, cols] = o

    def body(i, carry):
        chunk(0, i)
        chunk(1, jnp.where(i < nctx, nctx - 1 - i, nch + nctx - 1 - i))
        return carry

    lax.fori_loop(0, nch, body, 0)
    o_ref[0] += ob_ref[...]


def _gla(a, wg, bg):
    nb, s, _ = a.shape
    nch = s // GLA_CHUNK
    hp = GLA_HEADS // 2
    dk2, dv2 = 2 * GLA_DK, 2 * GLA_DV
    zblk = (2 * GLA_HEADS * GLA_DK + 5 * HALF) // 128
    return pl.pallas_call(
        functools.partial(_gla_kernel, nch=nch),
        out_shape=jax.ShapeDtypeStruct((nb, s, HALF), F32),
        grid=(nb, hp),
        in_specs=[pl.BlockSpec((1, s, dk2), lambda b, h: (b, 0, h)),
                  pl.BlockSpec((1, s, dk2), lambda b, h: (b, 0, hp + h)),
                  pl.BlockSpec((1, s, dv2), lambda b, h: (b, 0, hp + h)),
                  pl.BlockSpec((1, s, 128), lambda b, h: (b, 0, zblk)),
                  pl.BlockSpec((2, 1, 128, dk2), lambda b, h: (0, h, 0, 0)),
                  pl.BlockSpec((2, 1, 1, dk2), lambda b, h: (0, h, 0, 0))],
        out_specs=pl.BlockSpec((1, s, dv2), lambda b, h: (b, 0, h)),
        scratch_shapes=[pltpu.VMEM((s, dk2), F32), pltpu.VMEM((s, dk2), F32), pltpu.VMEM((s, dv2), F32),
                        pltpu.VMEM((2, 2, dk2, GLA_DV), F32)],
        compiler_params=_cparams(("parallel", "parallel")),
        name="gla",
    )(a, a, a, a, wg, bg)


def _rope_tables(s):
    t = np.arange(s - CTX)
    n = DIFF_DH // 4
    inv = ROPE_BASE ** (-np.arange(n, dtype=np.float64) / n)
    ang_r = (t // GRID_W)[:, None] * inv[None]
    ang_c = (t % GRID_W)[:, None] * inv[None]
    ang = np.concatenate([ang_r, ang_r, ang_c, ang_c], axis=1)
    ang = np.concatenate([ang, ang], axis=1)
    cos = np.concatenate([np.ones((CTX, 128)), np.cos(ang)], axis=0)
    sin = np.concatenate([np.zeros((CTX, 128)), np.sin(ang)], axis=0)
    first = ((np.arange(128) // n) % 2 == 0)[None]
    sin_up = np.where(first, -sin, 0.0)
    sin_dn = np.where(first, 0.0, sin)
    f = lambda z: jnp.asarray(z, F32)
    return f(cos), f(sin_up), f(sin_dn)


def _rope_kernel(q_ref, k_ref, cos_ref, su_ref, sd_ref, qo_ref, ko_ref):
    n = DIFF_DH // 4
    cos = jnp.concatenate([cos_ref[...]] * DIFF_HEADS, axis=1)
    su = jnp.concatenate([su_ref[...]] * DIFF_HEADS, axis=1)
    sd = jnp.concatenate([sd_ref[...]] * DIFF_HEADS, axis=1)

    def rot(x):
        w = x.shape[1]
        return x * cos + pltpu.roll(x, w - n, 1) * su + pltpu.roll(x, n, 1) * sd

    qo_ref[0] = rot(q_ref[0] * (DIFF_DH ** -0.5 * LOG2E)).astype(BF16)
    ko_ref[0] = rot(k_ref[0]).astype(BF16)


def _rope(a, tables):
    nb, s, _ = a.shape
    qblk = (2 * GLA_HEADS * GLA_DK + 2 * HALF) // HALF
    tile = lambda b, j: (b, j, 0)
    tab = pl.BlockSpec((ROW_TILE, 128), lambda b, j: (j, 0))
    return pl.pallas_call(
        _rope_kernel,
        out_shape=[jax.ShapeDtypeStruct((nb, s, HALF), BF16)] * 2,
        grid=(nb, s // ROW_TILE),
        in_specs=[pl.BlockSpec((1, ROW_TILE, HALF), lambda b, j: (b, j, qblk)),
                  pl.BlockSpec((1, ROW_TILE, HALF), lambda b, j: (b, j, qblk + 1)),
                  tab, tab, tab],
        out_specs=[pl.BlockSpec((1, ROW_TILE, HALF), tile)] * 2,
        compiler_params=_cparams(("parallel", "parallel")),
        name="rope",
    )(a, a, *tables)


def _diff_kernel(q_ref, k_ref, v_ref, lam_ref, ng_ref, o_ref, vb_ref, *, lam_init):
    j = pl.program_id(2)
    lane = lax.broadcasted_iota(jnp.int32, (1, 2 * DIFF_DH), 1)
    map_mask = [lane < DIFF_DH, lane >= DIFF_DH]
    lv = lam_ref[...]
    lam = (jnp.exp(jnp.sum(lv[0:1] * lv[1:2], axis=-1, keepdims=True))
           - jnp.exp(jnp.sum(lv[2:3] * lv[3:4], axis=-1, keepdims=True)) + lam_init)

    @pl.when(j == 0)
    def _():
        vb_ref[:, 0:DIFF_DV] = v_ref[0].astype(BF16)
        vb_ref[:, DIFF_DV:] = jnp.ones((vb_ref.shape[0], DIFF_DV), BF16)

    def attend(nkeys):
        q = q_ref[0]
        qm = [jnp.where(map_mask[m], q, jnp.zeros_like(q)) for m in range(2)]
        m_run = [jnp.full((q.shape[0], 1), NEG, F32)] * 2
        acc = [jnp.zeros((q.shape[0], 2 * DIFF_DV), F32)] * 2
        starts = [0] + list(range(CTX, nkeys, DIFF_KB))
        for k0 in starts:
            kn = CTX if k0 == 0 else DIFF_KB
            k = k_ref[0, k0:k0 + kn, :]
            v1 = vb_ref[k0:k0 + kn, :]
            for m in range(2):
                s = _bdot_nt(qm[m], k)
                m_new = jnp.maximum(m_run[m], jnp.max(s, axis=-1, keepdims=True))
                p = jnp.exp2(s - m_new).astype(BF16)
                acc[m] = jnp.exp2(m_run[m] - m_new) * acc[m] + jnp.dot(p, v1, preferred_element_type=F32)
                m_run[m] = m_new
        o = (acc[0][:, :DIFF_DV] * (1.0 / acc[0][:, DIFF_DV:DIFF_DV + 1])
             - acc[1][:, :DIFF_DV] * (lam / acc[1][:, DIFF_DV:DIFF_DV + 1]))
        o = o * lax.rsqrt(jnp.mean(o * o, axis=-1, keepdims=True) + EPS) * ng_ref[...]
        o_ref[0] = o * (1.0 - lam_init)

    @pl.when(j == 0)
    def _():
        attend(CTX)

    @pl.when(j > 0)
    def _():
        attend(k_ref.shape[1])


def _diff(qr, kr, a, lam_vecs, ng, lam_init):
    nb, s, _ = qr.shape
    dw = 2 * DIFF_DH
    vblk = (2 * GLA_HEADS * GLA_DK + 4 * HALF) // dw
    return pl.pallas_call(
        functools.partial(_diff_kernel, lam_init=lam_init),
        out_shape=jax.ShapeDtypeStruct((nb, s, HALF), F32),
        grid=(nb, DIFF_HEADS, s // ROW_TILE),
        in_specs=[pl.BlockSpec((1, ROW_TILE, dw), lambda b, h, j: (b, j, h)),
                  pl.BlockSpec((1, s, dw), lambda b, h, j: (b, 0, h)),
                  pl.BlockSpec((1, s, DIFF_DV), lambda b, h, j: (b, 0, vblk + h)),
                  pl.BlockSpec((8, DIFF_DH), lambda b, h, j: (0, 0)),
                  pl.BlockSpec((1, DIFF_DV), lambda b, h, j: (0, 0))],
        out_specs=pl.BlockSpec((1, ROW_TILE, DIFF_DV), lambda b, h, j: (b, j, h)),
        scratch_shapes=[pltpu.VMEM((s, 2 * DIFF_DV), BF16)],
        compiler_params=_cparams(("parallel", "parallel", "arbitrary")),
        name="diff_attn",
    )(qr, kr, a, lam_vecs, ng)


def _split3(x):
    hi = x.astype(BF16)
    r1 = x - hi.astype(F32)
    mid = r1.astype(BF16)
    lo = (r1 - mid.astype(F32)).astype(BF16)
    return hi, mid, lo


def _moe_kernel(h_ref, m0_ref, m1_ref, mc_ref, wr_ref, br_ref, wg_ref, wu_ref, wd_ref, lng_ref, lnb_ref, o_ref,
                as_ref, cs_ref, ps_ref, ys_ref, win_ref, *, seq, win):
    t = pl.program_id(0)
    e = pl.program_id(1)
    tm = h_ref.shape[0]
    lane = lax.broadcasted_iota(jnp.int32, (1, 128), 1).astype(F32)
    d = D_MODEL

    def mod(col):
        b0 = (t * tm) // seq
        grow = t * tm + lax.broadcasted_iota(jnp.int32, (tm, 1), 0)
        in_next = grow >= (b0 + 1) * seq
        local = grow - jnp.where(in_next, b0 + 1, b0) * seq
        cols = slice(col * d, (col + 1) * d)
        return jnp.where(local < CTX, mc_ref[0, :, cols],
                         jnp.where(in_next, m1_ref[0, :, cols], m0_ref[0, :, cols]))

    @pl.when(e == 0)
    def _():
        a = h_ref[...] * (1.0 + mod(1)) + mod(0)
        logits = _hdot(a, wr_ref[...]) + br_ref[...]
        is_g = lane < N_GROUPS
        lg = jnp.where(is_g, logits, NEG)
        mg = jnp.max(lg, axis=-1, keepdims=True)
        g_sel = jnp.min(jnp.where(lg == mg, lane, 128.0), axis=-1, keepdims=True)
        p_grp = 1.0 / jnp.sum(jnp.where(is_g, jnp.exp(lg - mg), 0.0), axis=-1, keepdims=True)
        lo = N_GROUPS + g_sel * EPG
        in_grp = (lane >= lo) & (lane < lo + EPG)
        le = jnp.where(in_grp, logits, NEG)
        v1 = jnp.max(le, axis=-1, keepdims=True)
        i1 = jnp.min(jnp.where(le == v1, lane, 128.0), axis=-1, keepdims=True)
        le2 = jnp.where(lane == i1, NEG, le)
        v2 = jnp.max(le2, axis=-1, keepdims=True)
        i2 = jnp.min(jnp.where(le2 == v2, lane, 128.0), axis=-1, keepdims=True)
        e2 = jnp.exp(v2 - v1)
        w1 = p_grp / (1.0 + e2)
        w2 = p_grp * e2 / (1.0 + e2)
        comb = jnp.where(lane == i1, w1, jnp.where(lane == i2, w2, 0.0))

        onehot = jnp.where(lane == g_sel, 1.0, 0.0).astype(BF16)
        ri = lax.broadcasted_iota(jnp.int32, (tm, tm), 0)
        ci = lax.broadcasted_iota(jnp.int32, (tm, tm), 1)
        ranks = jnp.dot(jnp.where(ci <= ri, 1.0, 0.0).astype(BF16), onehot, preferred_element_type=F32)
        cnt = ranks[tm - 1:tm, :]
        n = [jnp.sum(jnp.where(lane == float(g), cnt, 0.0), axis=-1, keepdims=True) for g in range(N_GROUPS)]
        off = [jnp.zeros_like(n[0]), n[0], n[0] + n[1], n[0] + n[1] + n[2]]
        off_vec = sum(jnp.where(lane == float(g), off[g], 0.0) for g in range(1, N_GROUPS))
        dest = jnp.sum(jnp.where(lane == g_sel, ranks + off_vec, 0.0), axis=-1, keepdims=True) - 1.0

        pos = lax.broadcasted_iota(jnp.int32, (1, tm), 1).astype(F32)
        ps_ref[...] = jnp.where(dest == pos, 1.0, 0.0).astype(BF16)
        hi = jnp.floor(dest * (1.0 / 32.0))
        digits = jnp.where(lane == 0.0, hi, jnp.where(lane == 1.0, dest - 32.0 * hi, 0.0)).astype(BF16)
        ri8 = lax.broadcasted_iota(jnp.int32, (8, 128), 0)
        ci8 = lax.broadcasted_iota(jnp.int32, (8, 128), 1)
        dig_t = _bdot_nt(jnp.where(ri8 == ci8, 1.0, 0.0), digits)
        dest_row = 32.0 * dig_t[0:1, :] + dig_t[1:2, :]
        rows = lax.broadcasted_iota(jnp.int32, (tm, 1), 0).astype(F32)
        pg = jnp.where(dest_row == rows, 1.0, 0.0).astype(BF16)
        as_ref[...] = jnp.dot(pg, a.astype(BF16), preferred_element_type=F32).astype(BF16)
        c3 = _split3(comb)
        cs_ref[...] = (jnp.dot(pg, c3[0], preferred_element_type=F32)
                       + jnp.dot(pg, c3[1], preferred_element_type=F32)
                       + jnp.dot(pg, c3[2], preferred_element_type=F32))
        ys_ref[...] = jnp.zeros_like(ys_ref)
        for g in range(N_GROUPS):
            start = jnp.minimum((jnp.sum(off[g]).astype(jnp.int32) // 16) * 16, tm - win)
            end = jnp.sum(off[g] + n[g]).astype(jnp.int32)
            win_ref[g] = start
            win_ref[N_GROUPS + g] = jnp.where(end > start, (end - start + win - 1) // win, 0)

    g = (e * MOE_EB) // EPG
    start = win_ref[g]
    fblk = lax.broadcasted_iota(jnp.int32, (1, MOE_EB * D_EXPERT), 1) // D_EXPERT

    def window(w, carry):
        lo = start + w * win
        rs = pl.ds(pl.multiple_of(jnp.minimum(lo, tm - win), 16), win)
        x = as_ref[rs, :]
        row = jnp.minimum(lo, tm - win) + lax.broadcasted_iota(jnp.int32, (win, 1), 0)
        cs = jnp.where(row >= lo, cs_ref[rs, :], 0.0)
        cw = jnp.zeros((win, MOE_EB * D_EXPERT), F32)
        for j in range(MOE_EB):
            sel = lane == (e * MOE_EB + j + N_GROUPS).astype(F32)
            cw = jnp.where(fblk == j, jnp.sum(jnp.where(sel, cs, 0.0), axis=-1, keepdims=True), cw)
        hg = jnp.dot(x, wg_ref[0], preferred_element_type=F32)
        hu = jnp.dot(x, wu_ref[0], preferred_element_type=F32)
        act = (_silu(hg) * hu * cw).astype(BF16)
        ys_ref[rs, :] += jnp.dot(act, wd_ref[0], preferred_element_type=F32)
        return carry

    lax.fori_loop(0, win_ref[N_GROUPS + g], window, 0)

    @pl.when(e == N_EXPERTS // MOE_EB - 1)
    def _():
        y3 = _split3(ys_ref[...])
        ps = ps_ref[...]
        y = jnp.dot(ps, y3[0], preferred_element_type=F32) + jnp.dot(ps, y3[1], preferred_element_type=F32)
        o_ref[...] = _layer_norm(ALPHA * h_ref[...] + mod(2) * y, lng_ref[...], lnb_ref[...])


def _moe(h, mods_l, w_router, b_router, w_gate, w_up, w_down, ln_g, ln_b):
    nb, s, d = h.shape
    rows = nb * s
    tm = 1024 if rows % 1024 == 0 else 256
    win = tm // N_GROUPS + tm // 16
    nt = rows // tm
    const = lambda t, e: (0, 0)
    b0 = lambda t: (t * tm) // s
    neb = N_EXPERTS // MOE_EB
    ef = MOE_EB * D_EXPERT
    blocked = lambda w: w.reshape(neb, MOE_EB, d, D_EXPERT).transpose(0, 2, 1, 3).reshape(neb, d, ef).astype(BF16)
    w_gate, w_up, w_down = blocked(w_gate), blocked(w_up), w_down.reshape(neb, ef, d).astype(BF16)
    out = pl.pallas_call(
        functools.partial(_moe_kernel, seq=s, win=win),
        out_shape=jax.ShapeDtypeStruct((rows, d), F32),
        grid=(nt, neb),
        in_specs=[pl.BlockSpec((tm, d), lambda t, e: (t, 0)),
                  pl.BlockSpec((1, 1, 3 * d), lambda t, e: (b0(t), 0, 1)),
                  pl.BlockSpec((1, 1, 3 * d), lambda t, e: (jnp.minimum(b0(t) + 1, nb - 1), 0, 1)),
                  pl.BlockSpec((1, 1, 3 * d), lambda t, e: (nb, 0, 1)),
                  pl.BlockSpec((d, 128), const),
                  pl.BlockSpec((1, 128), const),
                  pl.BlockSpec((1, d, ef), lambda t, e: (e, 0, 0)),
                  pl.BlockSpec((1, d, ef), lambda t, e: (e, 0, 0)),
                  pl.BlockSpec((1, ef, d), lambda t, e: (e, 0, 0)),
                  pl.BlockSpec((1, d), const),
                  pl.BlockSpec((1, d), const)],
        out_specs=pl.BlockSpec((tm, d), lambda t, e: (t, 0)),
        scratch_shapes=[pltpu.VMEM((tm, d), BF16), pltpu.VMEM((tm, 128), F32), pltpu.VMEM((tm, tm), BF16),
                        pltpu.VMEM((tm, d), F32), pltpu.SMEM((2 * N_GROUPS,), jnp.int32)],
        compiler_params=_cparams(("parallel", "arbitrary")),
        name="moe",
    )(h.reshape(rows, d), mods_l, mods_l, mods_l, w_router, b_router, w_gate, w_up, w_down,
      ln_g.astype(F32).reshape(1, d), ln_b.astype(F32).reshape(1, d))
    return out.reshape(nb, s, d)


def _pad_lanes(x, n):
    return jnp.pad(x, [(0, 0)] * (x.ndim - 1) + [(0, n - x.shape[-1])])


def kernel(x, c, ctx, c_ctx, mod_w, mod_b, ln_g, ln_b, even_w_in, even_w_out, s5_lam_re, s5_lam_im, s5_log_dt, s5_b_re, s5_b_im, s5_c_re, s5_c_im, s5_d, s5_glu_w, s5_glu_b, na_rpb, odd_w_in, odd_w_out, gla_gate_w, gla_gate_b, gla_norm_g, diff_lq1, diff_lk1, diff_lq2, diff_lk2, diff_norm_g, moe_w_grp, moe_b_grp, moe_w_exp, moe_b_exp, moe_w_gate, moe_w_up, moe_w_down):
    nb, t, d = x.shape
    assert d == D_MODEL and ctx.shape[1] == CTX and nb < 8 and t % (NA_QR * GRID_W) == 0
    s = CTX + t
    rows = t // GRID_W
    h = jnp.concatenate([ctx, x], axis=1).astype(F32)
    cc = jnp.zeros((8, d), F32).at[:nb].set(c.astype(F32)).at[nb].set(c_ctx.astype(F32))
    mods = _mods(cc, mod_w.astype(F32), mod_b.astype(F32)).reshape(DEPTH, 8, 1, 6 * d)
    rope_tabs = _rope_tables(s)
    for i in range(DEPTH):
        j = i // 2
        mods_l = mods[i]
        if i % 2 == 0:
            a = _inproj(h, mods_l, even_w_in[j].astype(BF16))
            tables = _s5_tables(s5_lam_re[j], s5_lam_im[j], s5_log_dt[j], s5_b_re[j], s5_b_im[j],
                                s5_c_re[j], s5_c_im[j], s5_d[j])
            y_a = _s5(a[..., :HALF], tables)
            y_b = _natten(a, _na_bias_tables(na_rpb[j], rows))
            h = _outproj(y_a, y_b, h, mods_l, even_w_out[j], ln_g[i, 0], ln_b[i, 0],
                         glu_w=s5_glu_w[j], glu_b=s5_glu_b[j])
        else:
            lam_init = 0.8 - 0.6 * math.exp(-0.3 * i)
            w = odd_w_in[j]
            nqk = 2 * GLA_HEADS * GLA_DK + 2 * HALF
            w = jnp.concatenate([w[:, :nqk], w[:, nqk + 2 * GLA_RANK:],
                                 _pad_lanes(w[:, nqk:nqk + 2 * GLA_RANK], 128)], axis=1).astype(BF16)
            a = _inproj(h, mods_l, w)
            gw = gla_gate_w[j].astype(F32)
            wg = jnp.zeros((2, 128, GLA_HEADS * GLA_DK), F32)
            for dd in range(2):
                wg = wg.at[dd, dd * GLA_RANK:(dd + 1) * GLA_RANK].set(gw[dd])
            wg = wg.reshape(2, 128, 2, 2 * GLA_DK).transpose(0, 2, 1, 3)
            bg = gla_gate_b[j].astype(F32).reshape(2, 2, 1, 2 * GLA_DK)
            y_a = _gla(a, wg, bg)
            qr, kr = _rope(a, rope_tabs)
            lam_vecs = jnp.zeros((8, DIFF_DH), F32).at[0].set(diff_lq1[j]).at[1].set(diff_lk1[j]) \
                .at[2].set(diff_lq2[j]).at[3].set(diff_lk2[j])
            y_b = _diff(qr, kr, a, lam_vecs, diff_norm_g[j].astype(F32).reshape(1, DIFF_DV), lam_init)
            h = _outproj(y_a, y_b, h, mods_l, odd_w_out[j], ln_g[i, 0], ln_b[i, 0],
                         proj=a, norm_g=gla_norm_g[j])
        w_router = _pad_lanes(jnp.concatenate([moe_w_grp[i], moe_w_exp[i]], axis=1).astype(F32), 128)
        b_router = _pad_lanes(jnp.concatenate([moe_b_grp[i], moe_b_exp[i]]).astype(F32)[None], 128)
        h = _moe(h, mods_l, w_router, b_router, moe_w_gate[i], moe_w_up[i], moe_w_down[i], ln_g[i, 1], ln_b[i, 1])
    return h[:, CTX:].astype(x.dtype)
```

```python
import functools
import math

import numpy as np
import jax
import jax.numpy as jnp
from jax import lax
from jax.experimental import pallas as pl
from jax.experimental.pallas import tpu as pltpu

F32 = jnp.float32
BF16 = jnp.bfloat16
HIGHEST = lax.Precision.HIGHEST

D_MODEL = 1024
DEPTH = 4
GRID_W = 64
CTX = 256
HALF = D_MODEL // 2
S5_H = 16
S5_G = HALF // S5_H
S5_P = 64
S5_L = 16
NA_HEADS = 8
NA_DH = 64
NA_WR = 8
NA_WC = 16
NA_QR = 4
NA_KR = 12
GLA_HEADS = 4
GLA_DK = 64
GLA_DV = 128
GLA_RANK = 16
GLA_TAU = 16.0
GLA_CHUNK = 64
DIFF_HEADS = 4
DIFF_DV = 128
DIFF_DH = 64
DIFF_KB = 512
DIFF_QTILES = 4
N_GROUPS = 4
EPG = 8
N_EXPERTS = N_GROUPS * EPG
D_EXPERT = D_MODEL // 4
MOE_EB = 4
ROPE_BASE = 10000.0
EPS = 1e-5
ALPHA = (2.0 * DEPTH) ** 0.25
EVEN_IN = 4 * HALF
ODD_IN_PAD = 3200
NEG = -1e30
LOG2E = math.log2(math.e)
ROW_TILE = 256
VMEM_LIMIT = 56 * 1024 * 1024


def _cparams(sem):
    return pltpu.CompilerParams(dimension_semantics=sem, vmem_limit_bytes=VMEM_LIMIT)


def _bdot(a, b):
    return jnp.dot(a.astype(BF16), b.astype(BF16), preferred_element_type=F32)


def _bdot_nt(a, b):
    return lax.dot_general(a.astype(BF16), b.astype(BF16), (((1,), (1,)), ((), ())),
                           preferred_element_type=F32)


def _hdot(a, b):
    return jnp.dot(a, b, precision=HIGHEST, preferred_element_type=F32)


def _layer_norm(x, g, b):
    mu = jnp.mean(x, axis=-1, keepdims=True)
    xc = x - mu
    var = jnp.mean(xc * xc, axis=-1, keepdims=True)
    return xc * lax.rsqrt(var + EPS) * g + b


def _silu(x):
    return x * jax.nn.sigmoid(x)


def _mods_kernel(cc_ref, w_ref, b_ref, o_ref):
    o_ref[0] = _hdot(_silu(cc_ref[...]), w_ref[0]) + b_ref[0]


def _mods(cc, mod_w, mod_b):
    nblk = 6
    return pl.pallas_call(
        _mods_kernel,
        out_shape=jax.ShapeDtypeStruct((DEPTH, 8, 6 * D_MODEL), F32),
        grid=(DEPTH, nblk),
        in_specs=[pl.BlockSpec((8, D_MODEL), lambda l, j: (0, 0)),
                  pl.BlockSpec((1, D_MODEL, D_MODEL), lambda l, j: (l, 0, j)),
                  pl.BlockSpec((1, 1, D_MODEL), lambda l, j: (l, 0, j))],
        out_specs=pl.BlockSpec((1, 8, D_MODEL), lambda l, j: (l, 0, j)),
        compiler_params=_cparams(("arbitrary", "arbitrary")),
        name="mods",
    )(cc, mod_w, mod_b.reshape(DEPTH, 1, 6 * D_MODEL))


def _mod_row(nbatch):
    return lambda b, j: jnp.where(j == 0, nbatch, b)


def _inproj_kernel(h_ref, sh_ref, sc_ref, w_ref, o_ref):
    a = h_ref[0] * (1.0 + sc_ref[0]) + sh_ref[0]
    o_ref[0] = _bdot(a, w_ref[...])


def _inproj(h, mods_l, w):
    nb, s, _ = h.shape
    n = w.shape[1]
    row = _mod_row(nb)
    return pl.pallas_call(
        _inproj_kernel,
        out_shape=jax.ShapeDtypeStruct((nb, s, n), F32),
        grid=(nb, s // ROW_TILE),
        in_specs=[pl.BlockSpec((1, ROW_TILE, D_MODEL), lambda b, j: (b, j, 0)),
                  pl.BlockSpec((1, 1, D_MODEL), lambda b, j: (row(b, j), 0, 0)),
                  pl.BlockSpec((1, 1, D_MODEL), lambda b, j: (row(b, j), 0, 1)),
                  pl.BlockSpec((D_MODEL, n), lambda b, j: (0, 0))],
        out_specs=pl.BlockSpec((1, ROW_TILE, n), lambda b, j: (b, j, 0)),
        compiler_params=_cparams(("parallel", "parallel")),
        name="inproj",
    )(h, mods_l, mods_l, w)


def _s5_tables(lam_re, lam_im, log_dt, b_re, b_im, c_re, c_im, d_skip):
    L, H, P, G = S5_L, S5_H, S5_P, S5_G
    lam_re, lam_im = lam_re.astype(F32), lam_im.astype(F32)
    dt = jnp.exp(log_dt.astype(F32))[..., None]
    mag = jnp.exp(lam_re * dt)
    a_re = mag * jnp.cos(lam_im * dt)
    a_im = mag * jnp.sin(lam_im * dt)
    den = lam_re * lam_re + lam_im * lam_im
    f_re = ((a_re - 1.0) * lam_re + a_im * lam_im) / den
    f_im = (a_im * lam_re - (a_re - 1.0) * lam_im) / den
    b_re, b_im = b_re.astype(F32), b_im.astype(F32)
    bb_re = f_re[..., None] * b_re - f_im[..., None] * b_im
    bb_im = f_re[..., None] * b_im + f_im[..., None] * b_re
    tau = jnp.arange(L + 1, dtype=F32)[:, None, None, None]
    mag_t = jnp.exp(lam_re[None] * dt[None] * tau)
    ang_t = lam_im[None] * dt[None] * tau
    p_re = mag_t * jnp.cos(ang_t)
    p_im = mag_t * jnp.sin(ang_t)
    c_re, c_im = c_re.astype(F32), c_im.astype(F32)
    e_re = c_re[None] * p_re[:, :, :, None, :] - c_im[None] * p_im[:, :, :, None, :]
    e_im = c_re[None] * p_im[:, :, :, None, :] + c_im[None] * p_re[:, :, :, None, :]
    k = (jnp.einsum('tdgjp,dgpi->tdgji', e_re, bb_re, precision=HIGHEST)
         - jnp.einsum('tdgjp,dgpi->tdgji', e_im, bb_im, precision=HIGHEST))
    s_i, t_i = np.meshgrid(np.arange(L), np.arange(L), indexing='ij')
    kf = k[np.clip(t_i - s_i, 0, L), 0] * jnp.asarray(t_i >= s_i, F32)[..., None, None, None]
    kb = k[np.clip(s_i - t_i, 0, L), 1] * jnp.asarray(s_i >= t_i, F32)[..., None, None, None]
    skip = (jnp.asarray(s_i == t_i, F32)[..., None, None, None]
            * (jnp.eye(H, dtype=F32)[None] * d_skip.astype(F32).reshape(G, 1, H))[None, None])
    intra = (kf + kb + skip).transpose(2, 0, 4, 1, 3).reshape(G, L * H, L * H)

    def state_in(d, pw):
        pr, pi = p_re[pw, d], p_im[pw, d]
        re = pr[..., None] * bb_re[d][None] - pi[..., None] * bb_im[d][None]
        im = pr[..., None] * bb_im[d][None] + pi[..., None] * bb_re[d][None]
        return jnp.concatenate([re, im], axis=2).transpose(1, 0, 3, 2).reshape(G, L * H, 2 * P)

    sf = state_in(0, np.arange(L)[::-1].copy())
    sb = state_in(1, np.arange(L))
    w1 = jnp.concatenate([intra, sf, sb], axis=-1)

    def state_out(d, pw):
        er, ei = e_re[pw, d], e_im[pw, d]
        return jnp.concatenate([er, -ei], axis=-1).transpose(1, 3, 0, 2).reshape(G, 2 * P, L * H)

    w2 = jnp.concatenate([state_out(0, np.arange(1, L + 1)),
                          state_out(1, np.arange(L, 0, -1))], axis=1)
    ar, ai = p_re[L], p_im[L]
    rows = []
    for d in range(2):
        rows += [jnp.concatenate([ar[d], ar[d]], -1), jnp.concatenate([-ai[d], ai[d]], -1),
                 jnp.concatenate([ai[d], -ai[d]], -1)]
    rows += [jnp.zeros_like(rows[0])] * 2
    al = jnp.stack(rows, axis=1)
    return w1.astype(BF16), w2.astype(BF16), al


def _s5_kernel(x_ref, w1_ref, w2_ref, al_ref, y_ref, xw_ref, sw_ref, hh_ref, *, nch, nctx):
    lh = S5_L * S5_H
    p2 = 2 * S5_P
    xw_ref[...] = jnp.dot(x_ref[0], w1_ref[0], preferred_element_type=F32)
    sw_ref[:, 0:p2] = pltpu.roll(xw_ref[:, lh:lh + p2], S5_P, 1)
    sw_ref[:, p2:2 * p2] = pltpu.roll(xw_ref[:, lh + p2:lh + 2 * p2], S5_P, 1)
    al = al_ref[0]
    a1f, a2f, a3f, a1b, a2b, a3b = [al[i:i + 1] for i in range(6)]

    def body(k, carry):
        vf, wf, vb, wb = carry
        kb = jnp.where(k < nctx, nctx - 1 - k, nch + nctx - 1 - k)
        rf = pl.ds(pl.multiple_of(k * 8, 8), 8)
        rb = pl.ds(pl.multiple_of(kb * 8, 8), 8)
        hh_ref[rf, 0:p2] = vf
        hh_ref[rb, p2:2 * p2] = vb
        s_f = xw_ref[rf, lh:lh + p2]
        s_fs = sw_ref[rf, 0:p2]
        s_b = xw_ref[rb, lh + p2:lh + 2 * p2]
        s_bs = sw_ref[rb, p2:2 * p2]
        return (vf * a1f + wf * a2f + s_f, wf * a1f + vf * a3f + s_fs,
                vb * a1b + wb * a2b + s_b, wb * a1b + vb * a3b + s_bs)

    zero = jnp.zeros((8, p2), F32)
    lax.fori_loop(0, nch, body, (zero,) * 4)
    y_ref[0] = xw_ref[:, 0:lh] + _bdot(hh_ref[...], w2_ref[0])


def _s5(u, tables):
    nb, s, _ = u.shape
    nch = s // S5_L
    lh = S5_L * S5_H
    r = nch * 8
    x = u.astype(BF16).reshape(nb, nch, S5_L, S5_G, S5_H).transpose(3, 1, 0, 2, 4)
    x = jnp.pad(x, ((0, 0), (0, 0), (0, 8 - nb), (0, 0), (0, 0))).reshape(S5_G, r, lh)
    w1, w2, al = tables
    y = pl.pallas_call(
        functools.partial(_s5_kernel, nch=nch, nctx=CTX // S5_L),
        out_shape=jax.ShapeDtypeStruct((S5_G, r, lh), F32),
        grid=(S5_G,),
        in_specs=[pl.BlockSpec((1, r, lh), lambda g: (g, 0, 0)),
                  pl.BlockSpec((1, lh, 2 * lh), lambda g: (g, 0, 0)),
                  pl.BlockSpec((1, lh, lh), lambda g: (g, 0, 0)),
                  pl.BlockSpec((1, 8, 2 * S5_P), lambda g: (g, 0, 0))],
        out_specs=pl.BlockSpec((1, r, lh), lambda g: (g, 0, 0)),
        scratch_shapes=[pltpu.VMEM((r, 2 * lh), F32), pltpu.VMEM((r, lh), F32), pltpu.VMEM((r, lh), F32)],
        compiler_params=_cparams(("parallel",)),
        name="s5",
    )(x, w1, w2, al)
    y = y.reshape(S5_G, nch, 8, S5_L, S5_H)[:, :, :nb]
    return y.transpose(2, 1, 3, 0, 4).reshape(nb, s, HALF)


def _na_bias_tables(rpb, rows):
    w = GRID_W
    nh, ndr, ndc = rpb.shape
    v = jnp.pad(rpb.astype(F32), ((0, 0), (0, 0), (w - NA_WC, 2 * w - ndc - (w - NA_WC))))
    skew = jnp.tile(v, (1, 1, w))[:, :, :w * (2 * w - 1)].reshape(nh, ndr, w, 2 * w - 1)
    tc = skew[..., w - 1:]
    valid, dr = [], []
    for r0 in (0, NA_QR, rows - NA_QR):
        k0 = int(np.clip(r0 - NA_WR // 2, 0, rows - NA_KR))
        rq = r0 + np.arange(NA_QR)[:, None, None, None]
        wq = np.arange(w)[None, :, None, None]
        kr = k0 + np.arange(NA_KR)[None, None, :, None]
        kc = np.arange(w)[None, None, None, :]
        rs = np.clip(rq - NA_WR // 2, 0, rows - NA_WR)
        cs = np.clip(wq - NA_WC // 2, 0, w - NA_WC)
        valid.append((kr >= rs) & (kr < rs + NA_WR) & (kc >= cs) & (kc < cs + NA_WC))
        dr.append(np.clip(kr - rq + NA_WR - 1, 0, ndr - 1)[:, 0, :, 0])
    dr = np.stack(dr).reshape(-1)
    bias = jnp.take(tc, jnp.asarray(dr), axis=1)
    bias = bias.reshape(nh, 3, NA_QR, NA_KR, w, w).transpose(0, 1, 2, 4, 3, 5)
    bias = jnp.where(jnp.asarray(np.stack(valid))[None], bias * LOG2E, NEG)
    return bias.reshape(nh, 3, NA_QR * w, NA_KR * w)


def _natten_kernel(q_ref, k_ref, v_ref, bias_ref, o_ref, qb_ref, kb_ref, vb_ref, *, rows):
    nq = NA_QR * GRID_W
    nk = NA_KR * GRID_W
    nblk = rows // NA_QR
    pw = 2 * NA_DH
    qb_ref[...] = (q_ref[0] * (NA_DH ** -0.5 * LOG2E)).astype(BF16)
    kb_ref[...] = k_ref[0].astype(BF16)
    vb_ref[:, 0:pw] = v_ref[0].astype(BF16)
    vb_ref[:, pw:] = jnp.ones((vb_ref.shape[0], pw), BF16)
    lane = lax.broadcasted_iota(jnp.int32, (1, pw), 1)
    head_mask = [lane < NA_DH, lane >= NA_DH]
    kc = kb_ref[0:CTX, :]
    vc = vb_ref[0:CTX, :]

    def attend(q, extra):
        n = q.shape[0]
        qs = jnp.concatenate([jnp.where(head_mask[0], q, jnp.zeros_like(q)),
                              jnp.where(head_mask[1], q, jnp.zeros_like(q))], axis=0)
        s_c = _bdot_nt(qs, kc)
        m = jnp.max(s_c, axis=-1, keepdims=True)
        if extra is not None:
            k_loc, v_loc, bias = extra
            s_l = _bdot_nt(qs, k_loc) + bias
            m = jnp.maximum(m, jnp.max(s_l, axis=-1, keepdims=True))
        acc = jnp.dot(jnp.exp2(s_c - m).astype(BF16), vc, preferred_element_type=F32)
        if extra is not None:
            acc = acc + jnp.dot(jnp.exp2(s_l - m).astype(BF16), v_loc, preferred_element_type=F32)
        o2 = acc[:, 0:pw] * (1.0 / acc[:, pw:pw + 1])
        return jnp.where(head_mask[0], o2[0:n], o2[n:])

    o_ref[0, 0:CTX, :] = attend(qb_ref[0:CTX, :], None)

    def body(i, carry):
        r0 = i * NA_QR
        k0 = jnp.clip(r0 - NA_WR // 2, 0, rows - NA_KR)
        pat = jnp.where(i == 0, 0, jnp.where(i == nblk - 1, 2, 1))
        qs = pl.multiple_of(CTX + r0 * GRID_W, GRID_W)
        ks = pl.multiple_of(CTX + k0 * GRID_W, GRID_W)
        q = qb_ref[pl.ds(qs, nq), :]
        k_loc = kb_ref[pl.ds(ks, nk), :]
        v_loc = vb_ref[pl.ds(ks, nk), :]
        bias = jnp.concatenate([bias_ref[0, pat], bias_ref[1, pat]], axis=0)
        o_ref[0, pl.ds(qs, nq), :] = attend(q, (k_loc, v_loc, bias))
        return carry

    lax.fori_loop(0, nblk, body, 0)


def _natten(a, bias):
    nb, s, _ = a.shape
    rows = (s - CTX) // GRID_W
    pw = 2 * NA_DH
    nq, nk = NA_QR * GRID_W, NA_KR * GRID_W
    hb = HALF // pw
    return pl.pallas_call(
        functools.partial(_natten_kernel, rows=rows),
        out_shape=jax.ShapeDtypeStruct((nb, s, HALF), F32),
        grid=(nb, hb),
        in_specs=[pl.BlockSpec((1, s, pw), lambda b, h: (b, 0, hb + h)),
                  pl.BlockSpec((1, s, pw), lambda b, h: (b, 0, 2 * hb + h)),
                  pl.BlockSpec((1, s, pw), lambda b, h: (b, 0, 3 * hb + h)),
                  pl.BlockSpec((2, 3, nq, nk), lambda b, h: (h, 0, 0, 0))],
        out_specs=pl.BlockSpec((1, s, pw), lambda b, h: (b, 0, h)),
        scratch_shapes=[pltpu.VMEM((s, pw), BF16), pltpu.VMEM((s, pw), BF16), pltpu.VMEM((s, 2 * pw), BF16)],
        compiler_params=_cparams(("parallel", "parallel")),
        name="natten",
    )(a, a, a, bias)


def _gelu_tanh(x):
    return 0.5 * x * (1.0 + jnp.tanh(math.sqrt(2.0 / math.pi) * (x + 0.044715 * (x * x * x))))


def _out_kernel(ya_ref, yb_ref, h_ref, g_ref, e0_ref, e1_ref, wo_ref, lng_ref, lnb_ref, o_ref, *, glu):
    if glu:
        z = _bdot(_gelu_tanh(ya_ref[0]), e0_ref[...]) + e1_ref[...]
        ya = z[:, :HALF] * jax.nn.sigmoid(z[:, HALF:])
    else:
        parts = []
        for hd in range(GLA_HEADS):
            cols = slice(hd * GLA_DV, (hd + 1) * GLA_DV)
            o = ya_ref[0, :, cols]
            o = o * lax.rsqrt(jnp.mean(o * o, axis=-1, keepdims=True) + EPS) * e1_ref[:, cols]
            parts.append(o * _silu(e0_ref[0, :, cols]))
        ya = jnp.concatenate(parts, axis=1)
    y = _bdot(ya, wo_ref[0:HALF, :]) + _bdot(yb_ref[0], wo_ref[HALF:, :])
    o_ref[0] = _layer_norm(ALPHA * h_ref[0] + g_ref[0] * y, lng_ref[...], lnb_ref[...])


def _outproj(ya, yb, h, mods_l, w_out, ln_g, ln_b, *, glu_w=None, glu_b=None, proj=None, norm_g=None):
    nb, s, _ = h.shape
    row = _mod_row(nb)
    glu = glu_w is not None
    tile = lambda b, j: (b, j, 0)
    const = lambda b, j: (0, 0)
    if glu:
        extra_specs = [pl.BlockSpec((HALF, D_MODEL), const), pl.BlockSpec((1, D_MODEL), const)]
        extra = [glu_w.astype(BF16), glu_b.astype(F32).reshape(1, D_MODEL)]
    else:
        rblk = (2 * GLA_HEADS * GLA_DK + HALF) // HALF
        extra_specs = [pl.BlockSpec((1, ROW_TILE, HALF), lambda b, j: (b, j, rblk)),
                       pl.BlockSpec((1, HALF), const)]
        extra = [proj, norm_g.astype(F32).reshape(1, HALF)]
    in_specs = [pl.BlockSpec((1, ROW_TILE, HALF), tile),
                pl.BlockSpec((1, ROW_TILE, HALF), tile),
                pl.BlockSpec((1, ROW_TILE, D_MODEL), tile),
                pl.BlockSpec((1, 1, D_MODEL), lambda b, j: (row(b, j), 0, 2))] + extra_specs + [
                pl.BlockSpec((D_MODEL, D_MODEL), const), pl.BlockSpec((1, D_MODEL), const),
                pl.BlockSpec((1, D_MODEL), const)]
    return pl.pallas_call(
        functools.partial(_out_kernel, glu=glu),
        out_shape=jax.ShapeDtypeStruct((nb, s, D_MODEL), F32),
        grid=(nb, s // ROW_TILE),
        in_specs=in_specs,
        out_specs=pl.BlockSpec((1, ROW_TILE, D_MODEL), tile),
        compiler_params=_cparams(("parallel", "parallel")),
        name="outproj_glu" if glu else "outproj_gate",
    )(ya, yb, h, mods_l, *extra, w_out.astype(BF16), ln_g.astype(F32).reshape(1, D_MODEL),
      ln_b.astype(F32).reshape(1, D_MODEL))


def _log_sigmoid(x):
    return jnp.minimum(x, 0.0) - jnp.log1p(jnp.exp(-jnp.abs(x)))


def _gla_kernel(q_ref, k_ref, v_ref, z_ref, wg_ref, bg_ref, o_ref, gf_ref, gb_ref, ob_ref, st_ref, *, nch):
    c = GLA_CHUNK
    dk2 = 2 * GLA_DK
    nctx = CTX // c
    z = z_ref[0]
    gf_ref[...] = _log_sigmoid(_hdot(z, wg_ref[0, 0]) + bg_ref[0, 0]) * (1.0 / GLA_TAU)
    gb_ref[...] = _log_sigmoid(_hdot(z, wg_ref[1, 0]) + bg_ref[1, 0]) * (1.0 / GLA_TAU)
    ri = lax.broadcasted_iota(jnp.int32, (c, c), 0)
    ci = lax.broadcasted_iota(jnp.int32, (c, c), 1)
    lane = lax.broadcasted_iota(jnp.int32, (1, dk2), 1)
    head_mask = [lane < GLA_DK, lane >= GLA_DK]
    scale = GLA_DK ** -0.5

    st_ref[...] = jnp.zeros_like(st_ref)

    def chunk(direction, n):
        g_ref = gf_ref if direction == 0 else gb_ref
        keep = (ci <= ri) if direction == 0 else (ci >= ri)
        tri = jnp.where(keep, 1.0, 0.0).astype(BF16)
        rs = pl.ds(pl.multiple_of(n * c, c), c)
        g = g_ref[rs, :]
        g_hi = g.astype(BF16)
        g_lo = (g - g_hi.astype(F32)).astype(BF16)
        gc = (jnp.dot(tri, g_hi, preferred_element_type=F32)
              + jnp.dot(tri, g_lo, preferred_element_type=F32))
        gend = gc[c - 1:c, :] if direction == 0 else gc[0:1, :]
        q = q_ref[0, rs, :] * scale
        k = k_ref[0, rs, :]
        qd = q * jnp.exp(gc)
        kd = k * jnp.exp(-gc)
        kdec = k * jnp.exp(gend - gc)
        dcol = jnp.exp(jnp.broadcast_to(gend, (dk2, dk2)).T)
        dcol = jnp.concatenate([dcol, dcol], axis=1)
        qs = jnp.concatenate([jnp.where(head_mask[0], qd, 0.0), jnp.where(head_mask[1], qd, 0.0)],
                             axis=0).astype(BF16)
        att = jnp.where(jnp.concatenate([keep, keep], axis=0), _bdot_nt(qs, kd), 0.0)
        v = v_ref[0, rs, :].astype(BF16)
        st = st_ref[direction]
        o2 = _bdot(att, v) + _bdot(qs, st)
        st_ref[direction] = dcol * st + jnp.dot(kdec.T.astype(BF16), v, preferred_element_type=F32)
        o = jnp.concatenate([o2[0:c, 0:GLA_DV], o2[c:2 * c, GLA_DV:2 * GLA_DV]], axis=1)
        if direction == 0:
            o_ref[0, rs, :] = o
        else:
            ob_ref[rs, :] = o

    def body(i, carry):
        chunk(0, i)
        chunk(1, jnp.where(i < nctx, nctx - 1 - i, nch + nctx - 1 - i))
        return carry

    lax.fori_loop(0, nch, body, 0)
    o_ref[0] += ob_ref[...]


def _gla(a, wg, bg):
    nb, s, _ = a.shape
    nch = s // GLA_CHUNK
    hp = GLA_HEADS // 2
    dk2, dv2 = 2 * GLA_DK, 2 * GLA_DV
    zblk = (2 * GLA_HEADS * GLA_DK + 5 * HALF) // 128
    return pl.pallas_call(
        functools.partial(_gla_kernel, nch=nch),
        out_shape=jax.ShapeDtypeStruct((nb, s, HALF), F32),
        grid=(nb, hp),
        in_specs=[pl.BlockSpec((1, s, dk2), lambda b, h: (b, 0, h)),
                  pl.BlockSpec((1, s, dk2), lambda b, h: (b, 0, hp + h)),
                  pl.BlockSpec((1, s, dv2), lambda b, h: (b, 0, hp + h)),
                  pl.BlockSpec((1, s, 128), lambda b, h: (b, 0, zblk)),
                  pl.BlockSpec((2, 1, 128, dk2), lambda b, h: (0, h, 0, 0)),
                  pl.BlockSpec((2, 1, 1, dk2), lambda b, h: (0, h, 0, 0))],
        out_specs=pl.BlockSpec((1, s, dv2), lambda b, h: (b, 0, h)),
        scratch_shapes=[pltpu.VMEM((s, dk2), F32), pltpu.VMEM((s, dk2), F32), pltpu.VMEM((s, dv2), F32),
                        pltpu.VMEM((2, dk2, dv2), F32)],
        compiler_params=_cparams(("parallel", "parallel")),
        name="gla",
    )(a, a, a, a, wg, bg)


def _rope_tables(s):
    t = np.arange(s - CTX)
    n = DIFF_DH // 4
    inv = ROPE_BASE ** (-np.arange(n, dtype=np.float64) / n)
    ang_r = (t // GRID_W)[:, None] * inv[None]
    ang_c = (t % GRID_W)[:, None] * inv[None]
    ang = np.concatenate([ang_r, ang_r, ang_c, ang_c], axis=1)
    ang = np.concatenate([ang, ang], axis=1)
    cos = np.concatenate([np.ones((CTX, 128)), np.cos(ang)], axis=0)
    sin = np.concatenate([np.zeros((CTX, 128)), np.sin(ang)], axis=0)
    first = ((np.arange(128) // n) % 2 == 0)[None]
    sin_up = np.where(first, -sin, 0.0)
    sin_dn = np.where(first, 0.0, sin)
    f = lambda z: jnp.asarray(z, F32)
    return f(cos), f(sin_up), f(sin_dn)


def _rope_kernel(q_ref, k_ref, cos_ref, su_ref, sd_ref, qo_ref, ko_ref):
    n = DIFF_DH // 4
    cos = jnp.concatenate([cos_ref[...]] * DIFF_HEADS, axis=1)
    su = jnp.concatenate([su_ref[...]] * DIFF_HEADS, axis=1)
    sd = jnp.concatenate([sd_ref[...]] * DIFF_HEADS, axis=1)

    def rot(x):
        w = x.shape[1]
        return x * cos + pltpu.roll(x, w - n, 1) * su + pltpu.roll(x, n, 1) * sd

    qo_ref[0] = rot(q_ref[0] * (DIFF_DH ** -0.5 * LOG2E)).astype(BF16)
    ko_ref[0] = rot(k_ref[0]).astype(BF16)


def _rope(a, tables):
    nb, s, _ = a.shape
    qblk = (2 * GLA_HEADS * GLA_DK + 2 * HALF) // HALF
    tile = lambda b, j: (b, j, 0)
    tab = pl.BlockSpec((ROW_TILE, 128), lambda b, j: (j, 0))
    return pl.pallas_call(
        _rope_kernel,
        out_shape=[jax.ShapeDtypeStruct((nb, s, HALF), BF16)] * 2,
        grid=(nb, s // ROW_TILE),
        in_specs=[pl.BlockSpec((1, ROW_TILE, HALF), lambda b, j: (b, j, qblk)),
                  pl.BlockSpec((1, ROW_TILE, HALF), lambda b, j: (b, j, qblk + 1)),
                  tab, tab, tab],
        out_specs=[pl.BlockSpec((1, ROW_TILE, HALF), tile)] * 2,
        compiler_params=_cparams(("parallel", "parallel")),
        name="rope",
    )(a, a, *tables)


def _diff_kernel(q_ref, k_ref, v_ref, lam_ref, ng_ref, o_ref, vb_ref, m_ref, acc_ref, *, lam_init):
    j = pl.program_id(2)
    qt = q_ref.shape[1]
    nkeys = k_ref.shape[1]
    lane = lax.broadcasted_iota(jnp.int32, (1, 2 * DIFF_DH), 1)
    map_mask = [lane < DIFF_DH, lane >= DIFF_DH]
    lv = lam_ref[...]
    lam = (jnp.exp(jnp.sum(lv[0:1] * lv[1:2], axis=-1, keepdims=True))
           - jnp.exp(jnp.sum(lv[2:3] * lv[3:4], axis=-1, keepdims=True)) + lam_init)

    @pl.when(j == 0)
    def _():
        vb_ref[:, 0:DIFF_DV] = v_ref[0].astype(BF16)
        vb_ref[:, DIFF_DV:] = jnp.ones((vb_ref.shape[0], DIFF_DV), BF16)

    def attend(has_ctx_rows):
        q = q_ref[0]
        qs = jnp.concatenate([jnp.where(map_mask[0], q, jnp.zeros_like(q)),
                              jnp.where(map_mask[1], q, jnp.zeros_like(q))], axis=0)
        r = lax.broadcasted_iota(jnp.int32, (2 * qt, 1), 0)
        ctx_row = (r < CTX) | ((r >= qt) & (r < qt + CTX))

        def block(k, v1, m_run, acc, mask_ctx_rows):
            s = _bdot_nt(qs, k)
            if mask_ctx_rows:
                s = jnp.where(ctx_row, NEG, s)
            m_new = jnp.maximum(m_run, jnp.max(s, axis=-1, keepdims=True))
            p = jnp.exp2(s - m_new).astype(BF16)
            return m_new, jnp.exp2(m_run - m_new) * acc + jnp.dot(p, v1, preferred_element_type=F32)

        m0, a0 = block(k_ref[0, 0:CTX, :], vb_ref[0:CTX, :], jnp.full((2 * qt, 1), NEG, F32),
                       jnp.zeros((2 * qt, 2 * DIFF_DV), F32), False)
        m_ref[...] = m0
        acc_ref[...] = a0

        def body(i, carry):
            ks = pl.ds(pl.multiple_of(CTX + i * DIFF_KB, DIFF_KB // 2), DIFF_KB)
            m1, a1 = block(k_ref[0, ks, :], vb_ref[ks, :], m_ref[...], acc_ref[...], has_ctx_rows)
            m_ref[...] = m1
            acc_ref[...] = a1
            return carry

        lax.fori_loop(0, (nkeys - CTX) // DIFF_KB, body, 0)
        den = acc_ref[:, DIFF_DV:DIFF_DV + 1]
        o = acc_ref[0:qt, 0:DIFF_DV] * (1.0 / den[0:qt]) - acc_ref[qt:, 0:DIFF_DV] * (lam / den[qt:])
        o = o * lax.rsqrt(jnp.mean(o * o, axis=-1, keepdims=True) + EPS) * ng_ref[...]
        o_ref[0] = o * (1.0 - lam_init)

    @pl.when(j == 0)
    def _():
        attend(True)

    @pl.when(j > 0)
    def _():
        attend(False)


def _diff(qr, kr, a, lam_vecs, ng, lam_init):
    nb, s, _ = qr.shape
    dw = 2 * DIFF_DH
    vblk = (2 * GLA_HEADS * GLA_DK + 4 * HALF) // dw
    qt = s // DIFF_QTILES
    assert s % DIFF_QTILES == 0 and qt % 8 == 0 and qt >= CTX and (s - CTX) % DIFF_KB == 0
    return pl.pallas_call(
        functools.partial(_diff_kernel, lam_init=lam_init),
        out_shape=jax.ShapeDtypeStruct((nb, s, HALF), F32),
        grid=(nb, DIFF_HEADS, DIFF_QTILES),
        in_specs=[pl.BlockSpec((1, qt, dw), lambda b, h, j: (b, j, h)),
                  pl.BlockSpec((1, s, dw), lambda b, h, j: (b, 0, h)),
                  pl.BlockSpec((1, s, DIFF_DV), lambda b, h, j: (b, 0, vblk + h)),
                  pl.BlockSpec((8, DIFF_DH), lambda b, h, j: (0, 0)),
                  pl.BlockSpec((1, DIFF_DV), lambda b, h, j: (0, 0))],
        out_specs=pl.BlockSpec((1, qt, DIFF_DV), lambda b, h, j: (b, j, h)),
        scratch_shapes=[pltpu.VMEM((s, 2 * DIFF_DV), BF16), pltpu.VMEM((2 * qt, 1), F32),
                        pltpu.VMEM((2 * qt, 2 * DIFF_DV), F32)],
        compiler_params=_cparams(("parallel", "parallel", "arbitrary")),
        name="diff_attn",
    )(qr, kr, a, lam_vecs, ng)


def _split3(x):
    hi = x.astype(BF16)
    r1 = x - hi.astype(F32)
    mid = r1.astype(BF16)
    lo = (r1 - mid.astype(F32)).astype(BF16)
    return hi, mid, lo


def _moe_kernel(h_ref, m0_ref, m1_ref, mc_ref, wr_ref, br_ref, wg_ref, wu_ref, wd_ref, lng_ref, lnb_ref, o_ref,
                as_ref, cs_ref, ps_ref, ys_ref, win_ref, *, seq, win):
    t = pl.program_id(0)
    e = pl.program_id(1)
    tm = h_ref.shape[0]
    lane = lax.broadcasted_iota(jnp.int32, (1, 128), 1).astype(F32)
    d = D_MODEL

    def mod(col):
        b0 = (t * tm) // seq
        grow = t * tm + lax.broadcasted_iota(jnp.int32, (tm, 1), 0)
        in_next = grow >= (b0 + 1) * seq
        local = grow - jnp.where(in_next, b0 + 1, b0) * seq
        cols = slice(col * d, (col + 1) * d)
        return jnp.where(local < CTX, mc_ref[0, :, cols],
                         jnp.where(in_next, m1_ref[0, :, cols], m0_ref[0, :, cols]))

    @pl.when(e == 0)
    def _():
        a = h_ref[...] * (1.0 + mod(1)) + mod(0)
        logits = _hdot(a, wr_ref[...]) + br_ref[...]
        is_g = lane < N_GROUPS
        lg = jnp.where(is_g, logits, NEG)
        mg = jnp.max(lg, axis=-1, keepdims=True)
        g_sel = jnp.min(jnp.where(lg == mg, lane, 128.0), axis=-1, keepdims=True)
        p_grp = 1.0 / jnp.sum(jnp.where(is_g, jnp.exp(lg - mg), 0.0), axis=-1, keepdims=True)
        lo = N_GROUPS + g_sel * EPG
        in_grp = (lane >= lo) & (lane < lo + EPG)
        le = jnp.where(in_grp, logits, NEG)
        v1 = jnp.max(le, axis=-1, keepdims=True)
        i1 = jnp.min(jnp.where(le == v1, lane, 128.0), axis=-1, keepdims=True)
        le2 = jnp.where(lane == i1, NEG, le)
        v2 = jnp.max(le2, axis=-1, keepdims=True)
        i2 = jnp.min(jnp.where(le2 == v2, lane, 128.0), axis=-1, keepdims=True)
        e2 = jnp.exp(v2 - v1)
        w1 = p_grp / (1.0 + e2)
        w2 = p_grp * e2 / (1.0 + e2)
        comb = jnp.where(lane == i1, w1, jnp.where(lane == i2, w2, 0.0))

        onehot = jnp.where(lane == g_sel, 1.0, 0.0).astype(BF16)
        ri = lax.broadcasted_iota(jnp.int32, (tm, tm), 0)
        ci = lax.broadcasted_iota(jnp.int32, (tm, tm), 1)
        ranks = jnp.dot(jnp.where(ci <= ri, 1.0, 0.0).astype(BF16), onehot, preferred_element_type=F32)
        cnt = ranks[tm - 1:tm, :]
        n = [jnp.sum(jnp.where(lane == float(g), cnt, 0.0), axis=-1, keepdims=True) for g in range(N_GROUPS)]
        off = [jnp.zeros_like(n[0]), n[0], n[0] + n[1], n[0] + n[1] + n[2]]
        off_vec = sum(jnp.where(lane == float(g), off[g], 0.0) for g in range(1, N_GROUPS))
        dest = jnp.sum(jnp.where(lane == g_sel, ranks + off_vec, 0.0), axis=-1, keepdims=True) - 1.0

        pos = lax.broadcasted_iota(jnp.int32, (1, tm), 1).astype(F32)
        ps_ref[...] = jnp.where(dest == pos, 1.0, 0.0).astype(BF16)
        hi = jnp.floor(dest * (1.0 / 32.0))
        digits = jnp.where(lane == 0.0, hi, jnp.where(lane == 1.0, dest - 32.0 * hi, 0.0)).astype(BF16)
        ri8 = lax.broadcasted_iota(jnp.int32, (8, 128), 0)
        ci8 = lax.broadcasted_iota(jnp.int32, (8, 128), 1)
        dig_t = _bdot_nt(jnp.where(ri8 == ci8, 1.0, 0.0), digits)
        dest_row = 32.0 * dig_t[0:1, :] + dig_t[1:2, :]
        rows = lax.broadcasted_iota(jnp.int32, (tm, 1), 0).astype(F32)
        pg = jnp.where(dest_row == rows, 1.0, 0.0).astype(BF16)
        as_ref[...] = jnp.dot(pg, a.astype(BF16), preferred_element_type=F32).astype(BF16)
        c3 = _split3(comb)
        cs_ref[...] = (jnp.dot(pg, c3[0], preferred_element_type=F32)
                       + jnp.dot(pg, c3[1], preferred_element_type=F32)
                       + jnp.dot(pg, c3[2], preferred_element_type=F32))
        ys_ref[...] = jnp.zeros_like(ys_ref)
        for g in range(N_GROUPS):
            start = jnp.minimum((jnp.sum(off[g]).astype(jnp.int32) // 16) * 16, tm - win)
            end = jnp.sum(off[g] + n[g]).astype(jnp.int32)
            win_ref[g] = start
            win_ref[N_GROUPS + g] = jnp.where(end > start, (end - start + win - 1) // win, 0)

    g = (e * MOE_EB) // EPG
    start = win_ref[g]
    fblk = lax.broadcasted_iota(jnp.int32, (1, MOE_EB * D_EXPERT), 1) // D_EXPERT

    def window(w, carry):
        lo = start + w * win
        rs = pl.ds(pl.multiple_of(jnp.minimum(lo, tm - win), 16), win)
        x = as_ref[rs, :]
        row = jnp.minimum(lo, tm - win) + lax.broadcasted_iota(jnp.int32, (win, 1), 0)
        cs = jnp.where(row >= lo, cs_ref[rs, :], 0.0)
        cw = jnp.zeros((win, MOE_EB * D_EXPERT), F32)
        for j in range(MOE_EB):
            sel = lane == (e * MOE_EB + j + N_GROUPS).astype(F32)
            cw = jnp.where(fblk == j, jnp.sum(jnp.where(sel, cs, 0.0), axis=-1, keepdims=True), cw)
        hg = jnp.dot(x, wg_ref[0], preferred_element_type=F32)
        hu = jnp.dot(x, wu_ref[0], preferred_element_type=F32)
        act = (_silu(hg) * hu * cw).astype(BF16)
        ys_ref[rs, :] += jnp.dot(act, wd_ref[0], preferred_element_type=F32)
        return carry

    lax.fori_loop(0, win_ref[N_GROUPS + g], window, 0)

    @pl.when(e == N_EXPERTS // MOE_EB - 1)
    def _():
        y3 = _split3(ys_ref[...])
        ps = ps_ref[...]
        y = jnp.dot(ps, y3[0], preferred_element_type=F32) + jnp.dot(ps, y3[1], preferred_element_type=F32)
        o_ref[...] = _layer_norm(ALPHA * h_ref[...] + mod(2) * y, lng_ref[...], lnb_ref[...])


def _moe(h, mods_l, w_router, b_router, w_gate, w_up, w_down, ln_g, ln_b):
    nb, s, d = h.shape
    rows = nb * s
    tm = 1024 if rows % 1024 == 0 else 256
    win = tm // N_GROUPS + tm // 16
    nt = rows // tm
    const = lambda t, e: (0, 0)
    b0 = lambda t: (t * tm) // s
    neb = N_EXPERTS // MOE_EB
    ef = MOE_EB * D_EXPERT
    blocked = lambda w: w.reshape(neb, MOE_EB, d, D_EXPERT).transpose(0, 2, 1, 3).reshape(neb, d, ef).astype(BF16)
    w_gate, w_up, w_down = blocked(w_gate), blocked(w_up), w_down.reshape(neb, ef, d).astype(BF16)
    out = pl.pallas_call(
        functools.partial(_moe_kernel, seq=s, win=win),
        out_shape=jax.ShapeDtypeStruct((rows, d), F32),
        grid=(nt, neb),
        in_specs=[pl.BlockSpec((tm, d), lambda t, e: (t, 0)),
                  pl.BlockSpec((1, 1, 3 * d), lambda t, e: (b0(t), 0, 1)),
                  pl.BlockSpec((1, 1, 3 * d), lambda t, e: (jnp.minimum(b0(t) + 1, nb - 1), 0, 1)),
                  pl.BlockSpec((1, 1, 3 * d), lambda t, e: (nb, 0, 1)),
                  pl.BlockSpec((d, 128), const),
                  pl.BlockSpec((1, 128), const),
                  pl.BlockSpec((1, d, ef), lambda t, e: (e, 0, 0)),
                  pl.BlockSpec((1, d, ef), lambda t, e: (e, 0, 0)),
                  pl.BlockSpec((1, ef, d), lambda t, e: (e, 0, 0)),
                  pl.BlockSpec((1, d), const),
                  pl.BlockSpec((1, d), const)],
        out_specs=pl.BlockSpec((tm, d), lambda t, e: (t, 0)),
        scratch_shapes=[pltpu.VMEM((tm, d), BF16), pltpu.VMEM((tm, 128), F32), pltpu.VMEM((tm, tm), BF16),
                        pltpu.VMEM((tm, d), F32), pltpu.SMEM((2 * N_GROUPS,), jnp.int32)],
        compiler_params=_cparams(("parallel", "arbitrary")),
        name="moe",
    )(h.reshape(rows, d), mods_l, mods_l, mods_l, w_router, b_router, w_gate, w_up, w_down,
      ln_g.astype(F32).reshape(1, d), ln_b.astype(F32).reshape(1, d))
    return out.reshape(nb, s, d)


def _pad_lanes(x, n):
    return jnp.pad(x, [(0, 0)] * (x.ndim - 1) + [(0, n - x.shape[-1])])


def kernel(x, c, ctx, c_ctx, mod_w, mod_b, ln_g, ln_b, even_w_in, even_w_out, s5_lam_re, s5_lam_im, s5_log_dt, s5_b_re, s5_b_im, s5_c_re, s5_c_im, s5_d, s5_glu_w, s5_glu_b, na_rpb, odd_w_in, odd_w_out, gla_gate_w, gla_gate_b, gla_norm_g, diff_lq1, diff_lk1, diff_lq2, diff_lk2, diff_norm_g, moe_w_grp, moe_b_grp, moe_w_exp, moe_b_exp, moe_w_gate, moe_w_up, moe_w_down):
    nb, t, d = x.shape
    assert d == D_MODEL and ctx.shape[1] == CTX and nb < 8 and t % (NA_QR * GRID_W) == 0
    s = CTX + t
    rows = t // GRID_W
    h = jnp.concatenate([ctx, x], axis=1).astype(F32)
    cc = jnp.zeros((8, d), F32).at[:nb].set(c.astype(F32)).at[nb].set(c_ctx.astype(F32))
    mods = _mods(cc, mod_w.astype(F32), mod_b.astype(F32)).reshape(DEPTH, 8, 1, 6 * d)
    rope_tabs = _rope_tables(s)
    for i in range(DEPTH):
        j = i // 2
        mods_l = mods[i]
        if i % 2 == 0:
            a = _inproj(h, mods_l, even_w_in[j].astype(BF16))
            tables = _s5_tables(s5_lam_re[j], s5_lam_im[j], s5_log_dt[j], s5_b_re[j], s5_b_im[j],
                                s5_c_re[j], s5_c_im[j], s5_d[j])
            y_a = _s5(a[..., :HALF], tables)
            y_b = _natten(a, _na_bias_tables(na_rpb[j], rows))
            h = _outproj(y_a, y_b, h, mods_l, even_w_out[j], ln_g[i, 0], ln_b[i, 0],
                         glu_w=s5_glu_w[j], glu_b=s5_glu_b[j])
        else:
            lam_init = 0.8 - 0.6 * math.exp(-0.3 * i)
            w = odd_w_in[j]
            nqk = 2 * GLA_HEADS * GLA_DK + 2 * HALF
            w = jnp.concatenate([w[:, :nqk], w[:, nqk + 2 * GLA_RANK:],
                                 _pad_lanes(w[:, nqk:nqk + 2 * GLA_RANK], 128)], axis=1).astype(BF16)
            a = _inproj(h, mods_l, w)
            gw = gla_gate_w[j].astype(F32)
            wg = jnp.zeros((2, 128, GLA_HEADS * GLA_DK), F32)
            for dd in range(2):
                wg = wg.at[dd, dd * GLA_RANK:(dd + 1) * GLA_RANK].set(gw[dd])
            wg = wg.reshape(2, 128, 2, 2 * GLA_DK).transpose(0, 2, 1, 3)
            bg = gla_gate_b[j].astype(F32).reshape(2, 2, 1, 2 * GLA_DK)
            y_a = _gla(a, wg, bg)
            qr, kr = _rope(a, rope_tabs)
            lam_vecs = jnp.zeros((8, DIFF_DH), F32).at[0].set(diff_lq1[j]).at[1].set(diff_lk1[j]) \
                .at[2].set(diff_lq2[j]).at[3].set(diff_lk2[j])
            y_b = _diff(qr, kr, a, lam_vecs, diff_norm_g[j].astype(F32).reshape(1, DIFF_DV), lam_init)
            h = _outproj(y_a, y_b, h, mods_l, odd_w_out[j], ln_g[i, 0], ln_b[i, 0],
                         proj=a, norm_g=gla_norm_g[j])
        w_router = _pad_lanes(jnp.concatenate([moe_w_grp[i], moe_w_exp[i]], axis=1).astype(F32), 128)
        b_router = _pad_lanes(jnp.concatenate([moe_b_grp[i], moe_b_exp[i]]).astype(F32)[None], 128)
        h = _moe(h, mods_l, w_router, b_router, moe_w_gate[i], moe_w_up[i], moe_w_down[i], ln_g[i, 1], ln_b[i, 1])
    return h[:, CTX:].astype(x.dtype)
```

```python
import functools
import math

import numpy as np
import jax
import jax.numpy as jnp
from jax import lax
from jax.experimental import pallas as pl
from jax.experimental.pallas import tpu as pltpu

F32 = jnp.float32
BF16 = jnp.bfloat16
HIGHEST = lax.Precision.HIGHEST

D_MODEL = 1024
DEPTH = 4
GRID_W = 64
CTX = 256
HALF = D_MODEL // 2
S5_H = 16
S5_G = HALF // S5_H
S5_P = 64
S5_L = 16
NA_HEADS = 8
NA_DH = 64
NA_WR = 8
NA_WC = 16
NA_QR = 4
NA_KR = 12
GLA_HEADS = 4
GLA_DK = 64
GLA_DV = 128
GLA_RANK = 16
GLA_TAU = 16.0
GLA_CHUNK = 64
DIFF_HEADS = 4
DIFF_DV = 128
DIFF_DH = 64
N_GROUPS = 4
EPG = 8
N_EXPERTS = N_GROUPS * EPG
D_EXPERT = D_MODEL // 4
MOE_EB = 4
ROPE_BASE = 10000.0
EPS = 1e-5
ALPHA = (2.0 * DEPTH) ** 0.25
EVEN_IN = 4 * HALF
ODD_IN_PAD = 3200
NEG = -1e30
LOG2E = math.log2(math.e)
ROW_TILE = 256
VMEM_LIMIT = 56 * 1024 * 1024


def _cparams(sem):
    return pltpu.CompilerParams(dimension_semantics=sem, vmem_limit_bytes=VMEM_LIMIT)


def _bdot(a, b):
    return jnp.dot(a.astype(BF16), b.astype(BF16), preferred_element_type=F32)


def _bdot_nt(a, b):
    return lax.dot_general(a.astype(BF16), b.astype(BF16), (((1,), (1,)), ((), ())),
                           preferred_element_type=F32)


def _hdot(a, b):
    return jnp.dot(a, b, precision=HIGHEST, preferred_element_type=F32)


def _layer_norm(x, g, b):
    mu = jnp.mean(x, axis=-1, keepdims=True)
    xc = x - mu
    var = jnp.mean(xc * xc, axis=-1, keepdims=True)
    return xc * lax.rsqrt(var + EPS) * g + b


def _silu(x):
    return x * jax.nn.sigmoid(x)


def _mods_kernel(cc_ref, w_ref, b_ref, o_ref):
    o_ref[0] = _hdot(_silu(cc_ref[...]), w_ref[0]) + b_ref[0]


def _mods(cc, mod_w, mod_b):
    nblk = 6
    return pl.pallas_call(
        _mods_kernel,
        out_shape=jax.ShapeDtypeStruct((DEPTH, 8, 6 * D_MODEL), F32),
        grid=(DEPTH, nblk),
        in_specs=[pl.BlockSpec((8, D_MODEL), lambda l, j: (0, 0)),
                  pl.BlockSpec((1, D_MODEL, D_MODEL), lambda l, j: (l, 0, j)),
                  pl.BlockSpec((1, 1, D_MODEL), lambda l, j: (l, 0, j))],
        out_specs=pl.BlockSpec((1, 8, D_MODEL), lambda l, j: (l, 0, j)),
        compiler_params=_cparams(("arbitrary", "arbitrary")),
        name="mods",
    )(cc, mod_w, mod_b.reshape(DEPTH, 1, 6 * D_MODEL))


def _mod_row(nbatch):
    return lambda b, j: jnp.where(j == 0, nbatch, b)


def _inproj_kernel(h_ref, sh_ref, sc_ref, w_ref, o_ref):
    a = h_ref[0] * (1.0 + sc_ref[0]) + sh_ref[0]
    o_ref[0] = _bdot(a, w_ref[...])


def _inproj(h, mods_l, w):
    nb, s, _ = h.shape
    n = w.shape[1]
    row = _mod_row(nb)
    return pl.pallas_call(
        _inproj_kernel,
        out_shape=jax.ShapeDtypeStruct((nb, s, n), F32),
        grid=(nb, s // ROW_TILE),
        in_specs=[pl.BlockSpec((1, ROW_TILE, D_MODEL), lambda b, j: (b, j, 0)),
                  pl.BlockSpec((1, 1, D_MODEL), lambda b, j: (row(b, j), 0, 0)),
                  pl.BlockSpec((1, 1, D_MODEL), lambda b, j: (row(b, j), 0, 1)),
                  pl.BlockSpec((D_MODEL, n), lambda b, j: (0, 0))],
        out_specs=pl.BlockSpec((1, ROW_TILE, n), lambda b, j: (b, j, 0)),
        compiler_params=_cparams(("parallel", "parallel")),
        name="inproj",
    )(h, mods_l, mods_l, w)


def _s5_tables(lam_re, lam_im, log_dt, b_re, b_im, c_re, c_im, d_skip):
    L, H, P, G = S5_L, S5_H, S5_P, S5_G
    lam_re, lam_im = lam_re.astype(F32), lam_im.astype(F32)
    dt = jnp.exp(log_dt.astype(F32))[..., None]
    mag = jnp.exp(lam_re * dt)
    a_re = mag * jnp.cos(lam_im * dt)
    a_im = mag * jnp.sin(lam_im * dt)
    den = lam_re * lam_re + lam_im * lam_im
    f_re = ((a_re - 1.0) * lam_re + a_im * lam_im) / den
    f_im = (a_im * lam_re - (a_re - 1.0) * lam_im) / den
    b_re, b_im = b_re.astype(F32), b_im.astype(F32)
    bb_re = f_re[..., None] * b_re - f_im[..., None] * b_im
    bb_im = f_re[..., None] * b_im + f_im[..., None] * b_re
    tau = jnp.arange(L + 1, dtype=F32)[:, None, None, None]
    mag_t = jnp.exp(lam_re[None] * dt[None] * tau)
    ang_t = lam_im[None] * dt[None] * tau
    p_re = mag_t * jnp.cos(ang_t)
    p_im = mag_t * jnp.sin(ang_t)
    c_re, c_im = c_re.astype(F32), c_im.astype(F32)
    e_re = c_re[None] * p_re[:, :, :, None, :] - c_im[None] * p_im[:, :, :, None, :]
    e_im = c_re[None] * p_im[:, :, :, None, :] + c_im[None] * p_re[:, :, :, None, :]
    k = (jnp.einsum('tdgjp,dgpi->tdgji', e_re, bb_re, precision=HIGHEST)
         - jnp.einsum('tdgjp,dgpi->tdgji', e_im, bb_im, precision=HIGHEST))
    s_i, t_i = np.meshgrid(np.arange(L), np.arange(L), indexing='ij')
    kf = k[np.clip(t_i - s_i, 0, L), 0] * jnp.asarray(t_i >= s_i, F32)[..., None, None, None]
    kb = k[np.clip(s_i - t_i, 0, L), 1] * jnp.asarray(s_i >= t_i, F32)[..., None, None, None]
    skip = (jnp.asarray(s_i == t_i, F32)[..., None, None, None]
            * (jnp.eye(H, dtype=F32)[None] * d_skip.astype(F32).reshape(G, 1, H))[None, None])
    intra = (kf + kb + skip).transpose(2, 0, 4, 1, 3).reshape(G, L * H, L * H)

    def state_in(d, pw):
        pr, pi = p_re[pw, d], p_im[pw, d]
        re = pr[..., None] * bb_re[d][None] - pi[..., None] * bb_im[d][None]
        im = pr[..., None] * bb_im[d][None] + pi[..., None] * bb_re[d][None]
        return jnp.concatenate([re, im], axis=2).transpose(1, 0, 3, 2).reshape(G, L * H, 2 * P)

    sf = state_in(0, np.arange(L)[::-1].copy())
    sb = state_in(1, np.arange(L))
    w1 = jnp.concatenate([intra, sf, sb], axis=-1)

    def state_out(d, pw):
        er, ei = e_re[pw, d], e_im[pw, d]
        return jnp.concatenate([er, -ei], axis=-1).transpose(1, 3, 0, 2).reshape(G, 2 * P, L * H)

    w2 = jnp.concatenate([state_out(0, np.arange(1, L + 1)),
                          state_out(1, np.arange(L, 0, -1))], axis=1)
    ar, ai = p_re[L], p_im[L]
    rows = []
    for d in range(2):
        rows += [jnp.concatenate([ar[d], ar[d]], -1), jnp.concatenate([-ai[d], ai[d]], -1),
                 jnp.concatenate([ai[d], -ai[d]], -1)]
    rows += [jnp.zeros_like(rows[0])] * 2
    al = jnp.stack(rows, axis=1)
    return w1.astype(BF16), w2.astype(BF16), al


def _s5_kernel(x_ref, w1_ref, w2_ref, al_ref, y_ref, xw_ref, sw_ref, hh_ref, *, nch, nctx):
    lh = S5_L * S5_H
    p2 = 2 * S5_P
    xw_ref[...] = jnp.dot(x_ref[0], w1_ref[0], preferred_element_type=F32)
    sw_ref[:, 0:p2] = pltpu.roll(xw_ref[:, lh:lh + p2], S5_P, 1)
    sw_ref[:, p2:2 * p2] = pltpu.roll(xw_ref[:, lh + p2:lh + 2 * p2], S5_P, 1)
    al = al_ref[0]
    a1f, a2f, a3f, a1b, a2b, a3b = [al[i:i + 1] for i in range(6)]

    def body(k, carry):
        vf, wf, vb, wb = carry
        kb = jnp.where(k < nctx, nctx - 1 - k, nch + nctx - 1 - k)
        rf = pl.ds(pl.multiple_of(k * 8, 8), 8)
        rb = pl.ds(pl.multiple_of(kb * 8, 8), 8)
        hh_ref[rf, 0:p2] = vf
        hh_ref[rb, p2:2 * p2] = vb
        s_f = xw_ref[rf, lh:lh + p2]
        s_fs = sw_ref[rf, 0:p2]
        s_b = xw_ref[rb, lh + p2:lh + 2 * p2]
        s_bs = sw_ref[rb, p2:2 * p2]
        return (vf * a1f + wf * a2f + s_f, wf * a1f + vf * a3f + s_fs,
                vb * a1b + wb * a2b + s_b, wb * a1b + vb * a3b + s_bs)

    zero = jnp.zeros((8, p2), F32)
    lax.fori_loop(0, nch, body, (zero,) * 4)
    y_ref[0] = (xw_ref[:, 0:lh] + _bdot(hh_ref[...], w2_ref[0])).astype(BF16)


def _s5(u, tables):
    nb, s, _ = u.shape
    nch = s // S5_L
    lh = S5_L * S5_H
    r = nch * 8
    x = u.astype(BF16).reshape(nb, nch, S5_L, S5_G, S5_H).transpose(3, 1, 0, 2, 4)
    x = jnp.pad(x, ((0, 0), (0, 0), (0, 8 - nb), (0, 0), (0, 0))).reshape(S5_G, r, lh)
    w1, w2, al = tables
    y = pl.pallas_call(
        functools.partial(_s5_kernel, nch=nch, nctx=CTX // S5_L),
        out_shape=jax.ShapeDtypeStruct((S5_G, r, lh), BF16),
        grid=(S5_G,),
        in_specs=[pl.BlockSpec((1, r, lh), lambda g: (g, 0, 0)),
                  pl.BlockSpec((1, lh, 2 * lh), lambda g: (g, 0, 0)),
                  pl.BlockSpec((1, lh, lh), lambda g: (g, 0, 0)),
                  pl.BlockSpec((1, 8, 2 * S5_P), lambda g: (g, 0, 0))],
        out_specs=pl.BlockSpec((1, r, lh), lambda g: (g, 0, 0)),
        scratch_shapes=[pltpu.VMEM((r, 2 * lh), F32), pltpu.VMEM((r, lh), F32), pltpu.VMEM((r, lh), F32)],
        compiler_params=_cparams(("parallel",)),
        name="s5",
    )(x, w1, w2, al)
    y = y.reshape(S5_G, nch, 8, S5_L, S5_H)[:, :, :nb]
    return y.transpose(2, 1, 3, 0, 4).reshape(nb, s, HALF)


def _na_bias_tables(rpb, rows):
    w = GRID_W
    nh, ndr, ndc = rpb.shape
    v = jnp.pad(rpb.astype(F32), ((0, 0), (0, 0), (w - NA_WC, 2 * w - ndc - (w - NA_WC))))
    skew = jnp.tile(v, (1, 1, w))[:, :, :w * (2 * w - 1)].reshape(nh, ndr, w, 2 * w - 1)
    tc = skew[..., w - 1:]
    valid, dr = [], []
    for r0 in (0, NA_QR, rows - NA_QR):
        k0 = int(np.clip(r0 - NA_WR // 2, 0, rows - NA_KR))
        rq = r0 + np.arange(NA_QR)[:, None, None, None]
        wq = np.arange(w)[None, :, None, None]
        kr = k0 + np.arange(NA_KR)[None, None, :, None]
        kc = np.arange(w)[None, None, None, :]
        rs = np.clip(rq - NA_WR // 2, 0, rows - NA_WR)
        cs = np.clip(wq - NA_WC // 2, 0, w - NA_WC)
        valid.append((kr >= rs) & (kr < rs + NA_WR) & (kc >= cs) & (kc < cs + NA_WC))
        dr.append(np.clip(kr - rq + NA_WR - 1, 0, ndr - 1)[:, 0, :, 0])
    dr = np.stack(dr).reshape(-1)
    bias = jnp.take(tc, jnp.asarray(dr), axis=1)
    bias = bias.reshape(nh, 3, NA_QR, NA_KR, w, w).transpose(0, 1, 2, 4, 3, 5)
    bias = jnp.where(jnp.asarray(np.stack(valid))[None], bias, NEG)
    return bias.reshape(nh, 3, NA_QR * w, NA_KR * w)


def _natten_kernel(q_ref, k_ref, v_ref, bias_ref, o_ref, qb_ref, kb_ref, vb_ref, *, rows):
    nq = NA_QR * GRID_W
    nk = NA_KR * GRID_W
    nblk = rows // NA_QR
    scale = NA_DH ** -0.5
    qb_ref[...] = (q_ref[0] * scale).astype(BF16)
    kb_ref[...] = k_ref[0].astype(BF16)
    vb_ref[...] = v_ref[0].astype(BF16)
    lane = lax.broadcasted_iota(jnp.int32, (1, 2 * NA_DH), 1)
    head_mask = [lane < NA_DH, lane >= NA_DH]
    kc = kb_ref[0:CTX, :]
    vc = vb_ref[0:CTX, :]

    def attend(q, extra):
        outs = []
        for h in range(2):
            qm = jnp.where(head_mask[h], q, jnp.zeros_like(q))
            s_c = _bdot_nt(qm, kc)
            m = jnp.max(s_c, axis=-1, keepdims=True)
            if extra is not None:
                k_loc, v_loc, bias = extra
                s_l = _bdot_nt(qm, k_loc) + bias[h]
                m = jnp.maximum(m, jnp.max(s_l, axis=-1, keepdims=True))
                p_l = jnp.exp(s_l - m)
            p_c = jnp.exp(s_c - m)
            den = jnp.sum(p_c, axis=-1, keepdims=True)
            acc = _bdot(p_c, vc)
            if extra is not None:
                den = den + jnp.sum(p_l, axis=-1, keepdims=True)
                acc = acc + _bdot(p_l, v_loc)
            outs.append(acc / den)
        return jnp.where(head_mask[0], outs[0], outs[1])

    o_ref[0, 0:CTX, :] = attend(qb_ref[0:CTX, :], None)

    def body(i, carry):
        r0 = i * NA_QR
        k0 = jnp.clip(r0 - NA_WR // 2, 0, rows - NA_KR)
        pat = jnp.where(i == 0, 0, jnp.where(i == nblk - 1, 2, 1))
        qs = pl.multiple_of(CTX + r0 * GRID_W, GRID_W)
        ks = pl.multiple_of(CTX + k0 * GRID_W, GRID_W)
        q = qb_ref[pl.ds(qs, nq), :]
        k_loc = kb_ref[pl.ds(ks, nk), :]
        v_loc = vb_ref[pl.ds(ks, nk), :]
        bias = [bias_ref[h, pat] for h in range(2)]
        o_ref[0, pl.ds(qs, nq), :] = attend(q, (k_loc, v_loc, bias))
        return carry

    lax.fori_loop(0, nblk, body, 0)


def _natten(a, bias):
    nb, s, _ = a.shape
    rows = (s - CTX) // GRID_W
    pw = 2 * NA_DH
    nq, nk = NA_QR * GRID_W, NA_KR * GRID_W
    hb = HALF // pw
    return pl.pallas_call(
        functools.partial(_natten_kernel, rows=rows),
        out_shape=jax.ShapeDtypeStruct((nb, s, HALF), F32),
        grid=(nb, hb),
        in_specs=[pl.BlockSpec((1, s, pw), lambda b, h: (b, 0, hb + h)),
                  pl.BlockSpec((1, s, pw), lambda b, h: (b, 0, 2 * hb + h)),
                  pl.BlockSpec((1, s, pw), lambda b, h: (b, 0, 3 * hb + h)),
                  pl.BlockSpec((2, 3, nq, nk), lambda b, h: (h, 0, 0, 0))],
        out_specs=pl.BlockSpec((1, s, pw), lambda b, h: (b, 0, h)),
        scratch_shapes=[pltpu.VMEM((s, pw), BF16)] * 3,
        compiler_params=_cparams(("parallel", "parallel")),
        name="natten",
    )(a, a, a, bias)


def _gelu_tanh(x):
    return 0.5 * x * (1.0 + jnp.tanh(math.sqrt(2.0 / math.pi) * (x + 0.044715 * (x * x * x))))


def _out_kernel(ya_ref, yb_ref, h_ref, g_ref, e0_ref, e1_ref, wo_ref, lng_ref, lnb_ref, o_ref, *, glu):
    if glu:
        z = _bdot(_gelu_tanh(ya_ref[0].astype(F32)), e0_ref[...]) + e1_ref[...]
        ya = z[:, :HALF] * jax.nn.sigmoid(z[:, HALF:])
    else:
        parts = []
        for hd in range(GLA_HEADS):
            cols = slice(hd * GLA_DV, (hd + 1) * GLA_DV)
            o = ya_ref[0, :, cols]
            o = o * lax.rsqrt(jnp.mean(o * o, axis=-1, keepdims=True) + EPS) * e1_ref[:, cols]
            parts.append(o * _silu(e0_ref[0, :, cols]))
        ya = jnp.concatenate(parts, axis=1)
    y = _bdot(ya, wo_ref[0:HALF, :]) + _bdot(yb_ref[0], wo_ref[HALF:, :])
    o_ref[0] = _layer_norm(ALPHA * h_ref[0] + g_ref[0] * y, lng_ref[...], lnb_ref[...])


def _outproj(ya, yb, h, mods_l, w_out, ln_g, ln_b, *, glu_w=None, glu_b=None, proj=None, norm_g=None):
    nb, s, _ = h.shape
    row = _mod_row(nb)
    glu = glu_w is not None
    tile = lambda b, j: (b, j, 0)
    const = lambda b, j: (0, 0)
    if glu:
        extra_specs = [pl.BlockSpec((HALF, D_MODEL), const), pl.BlockSpec((1, D_MODEL), const)]
        extra = [glu_w.astype(BF16), glu_b.astype(F32).reshape(1, D_MODEL)]
    else:
        rblk = (2 * GLA_HEADS * GLA_DK + HALF) // HALF
        extra_specs = [pl.BlockSpec((1, ROW_TILE, HALF), lambda b, j: (b, j, rblk)),
                       pl.BlockSpec((1, HALF), const)]
        extra = [proj, norm_g.astype(F32).reshape(1, HALF)]
    in_specs = [pl.BlockSpec((1, ROW_TILE, HALF), tile),
                pl.BlockSpec((1, ROW_TILE, HALF), tile),
                pl.BlockSpec((1, ROW_TILE, D_MODEL), tile),
                pl.BlockSpec((1, 1, D_MODEL), lambda b, j: (row(b, j), 0, 2))] + extra_specs + [
                pl.BlockSpec((D_MODEL, D_MODEL), const), pl.BlockSpec((1, D_MODEL), const),
                pl.BlockSpec((1, D_MODEL), const)]
    return pl.pallas_call(
        functools.partial(_out_kernel, glu=glu),
        out_shape=jax.ShapeDtypeStruct((nb, s, D_MODEL), F32),
        grid=(nb, s // ROW_TILE),
        in_specs=in_specs,
        out_specs=pl.BlockSpec((1, ROW_TILE, D_MODEL), tile),
        compiler_params=_cparams(("parallel", "parallel")),
        name="outproj_glu" if glu else "outproj_gate",
    )(ya, yb, h, mods_l, *extra, w_out.astype(BF16), ln_g.astype(F32).reshape(1, D_MODEL),
      ln_b.astype(F32).reshape(1, D_MODEL))


def _log_sigmoid(x):
    return jnp.minimum(x, 0.0) - jnp.log1p(jnp.exp(-jnp.abs(x)))


def _gla_kernel(q_ref, k_ref, v_ref, z_ref, wg_ref, bg_ref, o_ref, gf_ref, gb_ref, ob_ref, st_ref, *, nch):
    c = GLA_CHUNK
    dk2 = 2 * GLA_DK
    nctx = CTX // c
    z = z_ref[0]
    gf_ref[...] = _log_sigmoid(_hdot(z, wg_ref[0, 0]) + bg_ref[0, 0]) * (1.0 / GLA_TAU)
    gb_ref[...] = _log_sigmoid(_hdot(z, wg_ref[1, 0]) + bg_ref[1, 0]) * (1.0 / GLA_TAU)
    ri = lax.broadcasted_iota(jnp.int32, (c, c), 0)
    ci = lax.broadcasted_iota(jnp.int32, (c, c), 1)
    lane = lax.broadcasted_iota(jnp.int32, (1, dk2), 1)
    head_mask = [lane < GLA_DK, lane >= GLA_DK]
    scale = GLA_DK ** -0.5

    st_ref[...] = jnp.zeros_like(st_ref)

    def chunk(direction, n):
        g_ref = gf_ref if direction == 0 else gb_ref
        keep = (ci <= ri) if direction == 0 else (ci >= ri)
        tri = jnp.where(keep, 1.0, 0.0).astype(BF16)
        rs = pl.ds(pl.multiple_of(n * c, c), c)
        g = g_ref[rs, :]
        g_hi = g.astype(BF16)
        g_lo = (g - g_hi.astype(F32)).astype(BF16)
        gc = (jnp.dot(tri, g_hi, preferred_element_type=F32)
              + jnp.dot(tri, g_lo, preferred_element_type=F32))
        gend = gc[c - 1:c, :] if direction == 0 else gc[0:1, :]
        q = q_ref[0, rs, :] * scale
        k = k_ref[0, rs, :]
        qd = q * jnp.exp(gc)
        kd = k * jnp.exp(-gc)
        kdec = k * jnp.exp(gend - gc)
        dcol = jnp.exp(jnp.broadcast_to(gend, (dk2, dk2)).T)
        dcol = jnp.concatenate([dcol, dcol], axis=1)
        qs = jnp.concatenate([jnp.where(head_mask[0], qd, 0.0), jnp.where(head_mask[1], qd, 0.0)],
                             axis=0).astype(BF16)
        att = jnp.where(jnp.concatenate([keep, keep], axis=0), _bdot_nt(qs, kd), 0.0)
        v = v_ref[0, rs, :].astype(BF16)
        st = st_ref[direction]
        o2 = _bdot(att, v) + _bdot(qs, st)
        st_ref[direction] = dcol * st + jnp.dot(kdec.T.astype(BF16), v, preferred_element_type=F32)
        o = jnp.concatenate([o2[0:c, 0:GLA_DV], o2[c:2 * c, GLA_DV:2 * GLA_DV]], axis=1)
        if direction == 0:
            o_ref[0, rs, :] = o
        else:
            ob_ref[rs, :] = o

    def body(i, carry):
        chunk(0, i)
        chunk(1, jnp.where(i < nctx, nctx - 1 - i, nch + nctx - 1 - i))
        return carry

    lax.fori_loop(0, nch, body, 0)
    o_ref[0] += ob_ref[...]


def _gla(a, wg, bg):
    nb, s, _ = a.shape
    nch = s // GLA_CHUNK
    hp = GLA_HEADS // 2
    dk2, dv2 = 2 * GLA_DK, 2 * GLA_DV
    zblk = (2 * GLA_HEADS * GLA_DK + 5 * HALF) // 128
    return pl.pallas_call(
        functools.partial(_gla_kernel, nch=nch),
        out_shape=jax.ShapeDtypeStruct((nb, s, HALF), F32),
        grid=(nb, hp),
        in_specs=[pl.BlockSpec((1, s, dk2), lambda b, h: (b, 0, h)),
                  pl.BlockSpec((1, s, dk2), lambda b, h: (b, 0, hp + h)),
                  pl.BlockSpec((1, s, dv2), lambda b, h: (b, 0, hp + h)),
                  pl.BlockSpec((1, s, 128), lambda b, h: (b, 0, zblk)),
                  pl.BlockSpec((2, 1, 128, dk2), lambda b, h: (0, h, 0, 0)),
                  pl.BlockSpec((2, 1, 1, dk2), lambda b, h: (0, h, 0, 0))],
        out_specs=pl.BlockSpec((1, s, dv2), lambda b, h: (b, 0, h)),
        scratch_shapes=[pltpu.VMEM((s, dk2), F32), pltpu.VMEM((s, dk2), F32), pltpu.VMEM((s, dv2), F32),
                        pltpu.VMEM((2, dk2, dv2), F32)],
        compiler_params=_cparams(("parallel", "parallel")),
        name="gla",
    )(a, a, a, a, wg, bg)


def _rope_tables(s):
    t = np.arange(s - CTX)
    n = DIFF_DH // 4
    inv = ROPE_BASE ** (-np.arange(n, dtype=np.float64) / n)
    ang_r = (t // GRID_W)[:, None] * inv[None]
    ang_c = (t % GRID_W)[:, None] * inv[None]
    ang = np.concatenate([ang_r, ang_r, ang_c, ang_c], axis=1)
    ang = np.concatenate([ang, ang], axis=1)
    cos = np.concatenate([np.ones((CTX, 128)), np.cos(ang)], axis=0)
    sin = np.concatenate([np.zeros((CTX, 128)), np.sin(ang)], axis=0)
    first = ((np.arange(128) // n) % 2 == 0)[None]
    sin_up = np.where(first, -sin, 0.0)
    sin_dn = np.where(first, 0.0, sin)
    f = lambda z: jnp.asarray(z, F32)
    return f(cos), f(sin_up), f(sin_dn)


def _rope_kernel(q_ref, k_ref, cos_ref, su_ref, sd_ref, qo_ref, ko_ref):
    n = DIFF_DH // 4
    cos = jnp.concatenate([cos_ref[...]] * DIFF_HEADS, axis=1)
    su = jnp.concatenate([su_ref[...]] * DIFF_HEADS, axis=1)
    sd = jnp.concatenate([sd_ref[...]] * DIFF_HEADS, axis=1)

    def rot(x):
        w = x.shape[1]
        return x * cos + pltpu.roll(x, w - n, 1) * su + pltpu.roll(x, n, 1) * sd

    qo_ref[0] = rot(q_ref[0] * (DIFF_DH ** -0.5 * LOG2E)).astype(BF16)
    ko_ref[0] = rot(k_ref[0]).astype(BF16)


def _rope(a, tables):
    nb, s, _ = a.shape
    qblk = (2 * GLA_HEADS * GLA_DK + 2 * HALF) // HALF
    tile = lambda b, j: (b, j, 0)
    tab = pl.BlockSpec((ROW_TILE, 128), lambda b, j: (j, 0))
    return pl.pallas_call(
        _rope_kernel,
        out_shape=[jax.ShapeDtypeStruct((nb, s, HALF), BF16)] * 2,
        grid=(nb, s // ROW_TILE),
        in_specs=[pl.BlockSpec((1, ROW_TILE, HALF), lambda b, j: (b, j, qblk)),
                  pl.BlockSpec((1, ROW_TILE, HALF), lambda b, j: (b, j, qblk + 1)),
                  tab, tab, tab],
        out_specs=[pl.BlockSpec((1, ROW_TILE, HALF), tile)] * 2,
        compiler_params=_cparams(("parallel", "parallel")),
        name="rope",
    )(a, a, *tables)


def _diff_kernel(q_ref, k_ref, v_ref, lam_ref, ng_ref, o_ref, vb_ref, *, lam_init):
    j = pl.program_id(2)
    lane = lax.broadcasted_iota(jnp.int32, (1, 2 * DIFF_DH), 1)
    map_mask = [lane < DIFF_DH, lane >= DIFF_DH]
    lv = lam_ref[...]
    lam = (jnp.exp(jnp.sum(lv[0:1] * lv[1:2], axis=-1, keepdims=True))
           - jnp.exp(jnp.sum(lv[2:3] * lv[3:4], axis=-1, keepdims=True)) + lam_init)

    @pl.when(j == 0)
    def _():
        vb_ref[...] = v_ref[0].astype(BF16)

    def attend(nkeys):
        q = q_ref[0]
        k = k_ref[0, 0:nkeys, :]
        w = None
        for m in range(2):
            s = _bdot_nt(jnp.where(map_mask[m], q, jnp.zeros_like(q)), k)
            p = jnp.exp2(s - jnp.max(s, axis=-1, keepdims=True))
            inv = 1.0 / jnp.sum(p, axis=-1, keepdims=True)
            w = p * inv if m == 0 else w - p * (lam * inv)
        o = _bdot(w, vb_ref[0:nkeys, :])
        o = o * lax.rsqrt(jnp.mean(o * o, axis=-1, keepdims=True) + EPS) * ng_ref[...]
        o_ref[0] = o * (1.0 - lam_init)

    @pl.when(j == 0)
    def _():
        attend(CTX)

    @pl.when(j > 0)
    def _():
        attend(k_ref.shape[1])


def _diff(qr, kr, a, lam_vecs, ng, lam_init):
    nb, s, _ = qr.shape
    dw = 2 * DIFF_DH
    vblk = (2 * GLA_HEADS * GLA_DK + 4 * HALF) // dw
    return pl.pallas_call(
        functools.partial(_diff_kernel, lam_init=lam_init),
        out_shape=jax.ShapeDtypeStruct((nb, s, HALF), F32),
        grid=(nb, DIFF_HEADS, s // ROW_TILE),
        in_specs=[pl.BlockSpec((1, ROW_TILE, dw), lambda b, h, j: (b, j, h)),
                  pl.BlockSpec((1, s, dw), lambda b, h, j: (b, 0, h)),
                  pl.BlockSpec((1, s, DIFF_DV), lambda b, h, j: (b, 0, vblk + h)),
                  pl.BlockSpec((8, DIFF_DH), lambda b, h, j: (0, 0)),
                  pl.BlockSpec((1, DIFF_DV), lambda b, h, j: (0, 0))],
        out_specs=pl.BlockSpec((1, ROW_TILE, DIFF_DV), lambda b, h, j: (b, j, h)),
        scratch_shapes=[pltpu.VMEM((s, DIFF_DV), BF16)],
        compiler_params=_cparams(("parallel", "parallel", "arbitrary")),
        name="diff_attn",
    )(qr, kr, a, lam_vecs, ng)


def _split3(x):
    hi = x.astype(BF16)
    r1 = x - hi.astype(F32)
    mid = r1.astype(BF16)
    lo = (r1 - mid.astype(F32)).astype(BF16)
    return hi, mid, lo


def _moe_kernel(h_ref, m0_ref, m1_ref, mc_ref, wr_ref, br_ref, wg_ref, wu_ref, wd_ref, lng_ref, lnb_ref, o_ref,
                as_ref, cs_ref, ps_ref, ys_ref, win_ref, *, seq, win):
    t = pl.program_id(0)
    e = pl.program_id(1)
    tm = h_ref.shape[0]
    lane = lax.broadcasted_iota(jnp.int32, (1, 128), 1).astype(F32)
    d = D_MODEL

    def mod(col):
        b0 = (t * tm) // seq
        grow = t * tm + lax.broadcasted_iota(jnp.int32, (tm, 1), 0)
        in_next = grow >= (b0 + 1) * seq
        local = grow - jnp.where(in_next, b0 + 1, b0) * seq
        cols = slice(col * d, (col + 1) * d)
        return jnp.where(local < CTX, mc_ref[0, :, cols],
                         jnp.where(in_next, m1_ref[0, :, cols], m0_ref[0, :, cols]))

    @pl.when(e == 0)
    def _():
        a = h_ref[...] * (1.0 + mod(1)) + mod(0)
        logits = _hdot(a, wr_ref[...]) + br_ref[...]
        is_g = lane < N_GROUPS
        lg = jnp.where(is_g, logits, NEG)
        mg = jnp.max(lg, axis=-1, keepdims=True)
        g_sel = jnp.min(jnp.where(lg == mg, lane, 128.0), axis=-1, keepdims=True)
        p_grp = 1.0 / jnp.sum(jnp.where(is_g, jnp.exp(lg - mg), 0.0), axis=-1, keepdims=True)
        lo = N_GROUPS + g_sel * EPG
        in_grp = (lane >= lo) & (lane < lo + EPG)
        le = jnp.where(in_grp, logits, NEG)
        v1 = jnp.max(le, axis=-1, keepdims=True)
        i1 = jnp.min(jnp.where(le == v1, lane, 128.0), axis=-1, keepdims=True)
        le2 = jnp.where(lane == i1, NEG, le)
        v2 = jnp.max(le2, axis=-1, keepdims=True)
        i2 = jnp.min(jnp.where(le2 == v2, lane, 128.0), axis=-1, keepdims=True)
        e2 = jnp.exp(v2 - v1)
        w1 = p_grp / (1.0 + e2)
        w2 = p_grp * e2 / (1.0 + e2)
        comb = jnp.where(lane == i1, w1, jnp.where(lane == i2, w2, 0.0))

        onehot = jnp.where(lane == g_sel, 1.0, 0.0).astype(BF16)
        ri = lax.broadcasted_iota(jnp.int32, (tm, tm), 0)
        ci = lax.broadcasted_iota(jnp.int32, (tm, tm), 1)
        ranks = jnp.dot(jnp.where(ci <= ri, 1.0, 0.0).astype(BF16), onehot, preferred_element_type=F32)
        cnt = ranks[tm - 1:tm, :]
        n = [jnp.sum(jnp.where(lane == float(g), cnt, 0.0), axis=-1, keepdims=True) for g in range(N_GROUPS)]
        off = [jnp.zeros_like(n[0]), n[0], n[0] + n[1], n[0] + n[1] + n[2]]
        off_vec = sum(jnp.where(lane == float(g), off[g], 0.0) for g in range(1, N_GROUPS))
        dest = jnp.sum(jnp.where(lane == g_sel, ranks + off_vec, 0.0), axis=-1, keepdims=True) - 1.0

        pos = lax.broadcasted_iota(jnp.int32, (1, tm), 1).astype(F32)
        ps_ref[...] = jnp.where(dest == pos, 1.0, 0.0).astype(BF16)
        hi = jnp.floor(dest * (1.0 / 32.0))
        digits = jnp.where(lane == 0.0, hi, jnp.where(lane == 1.0, dest - 32.0 * hi, 0.0)).astype(BF16)
        ri8 = lax.broadcasted_iota(jnp.int32, (8, 128), 0)
        ci8 = lax.broadcasted_iota(jnp.int32, (8, 128), 1)
        dig_t = _bdot_nt(jnp.where(ri8 == ci8, 1.0, 0.0), digits)
        dest_row = 32.0 * dig_t[0:1, :] + dig_t[1:2, :]
        rows = lax.broadcasted_iota(jnp.int32, (tm, 1), 0).astype(F32)
        pg = jnp.where(dest_row == rows, 1.0, 0.0).astype(BF16)
        as_ref[...] = jnp.dot(pg, a.astype(BF16), preferred_element_type=F32).astype(BF16)
        c3 = _split3(comb)
        cs_ref[...] = (jnp.dot(pg, c3[0], preferred_element_type=F32)
                       + jnp.dot(pg, c3[1], preferred_element_type=F32)
                       + jnp.dot(pg, c3[2], preferred_element_type=F32))
        ys_ref[...] = jnp.zeros_like(ys_ref)
        for g in range(N_GROUPS):
            start = jnp.minimum((jnp.sum(off[g]).astype(jnp.int32) // 16) * 16, tm - win)
            end = jnp.sum(off[g] + n[g]).astype(jnp.int32)
            win_ref[g] = start
            win_ref[N_GROUPS + g] = jnp.where(end > start, (end - start + win - 1) // win, 0)

    g = (e * MOE_EB) // EPG
    start = win_ref[g]

    def window(w, carry):
        lo = start + w * win
        rs = pl.ds(pl.multiple_of(jnp.minimum(lo, tm - win), 16), win)
        x = as_ref[rs, :]
        row = jnp.minimum(lo, tm - win) + lax.broadcasted_iota(jnp.int32, (win, 1), 0)
        cs = jnp.where(row >= lo, cs_ref[rs, :], 0.0)
        upd = None
        for j in range(MOE_EB):
            sel = lane == (e * MOE_EB + j + N_GROUPS).astype(F32)
            cw = jnp.sum(jnp.where(sel, cs, 0.0), axis=-1, keepdims=True)
            hg = jnp.dot(x, wg_ref[j], preferred_element_type=F32)
            hu = jnp.dot(x, wu_ref[j], preferred_element_type=F32)
            dn = jnp.dot((_silu(hg) * hu * cw).astype(BF16), wd_ref[j], preferred_element_type=F32)
            upd = dn if upd is None else upd + dn
        ys_ref[rs, :] += upd
        return carry

    lax.fori_loop(0, win_ref[N_GROUPS + g], window, 0)

    @pl.when(e == N_EXPERTS // MOE_EB - 1)
    def _():
        y3 = _split3(ys_ref[...])
        ps = ps_ref[...]
        y = jnp.dot(ps, y3[0], preferred_element_type=F32) + jnp.dot(ps, y3[1], preferred_element_type=F32)
        o_ref[...] = _layer_norm(ALPHA * h_ref[...] + mod(2) * y, lng_ref[...], lnb_ref[...])


def _moe(h, mods_l, w_router, b_router, w_gate, w_up, w_down, ln_g, ln_b):
    nb, s, d = h.shape
    rows = nb * s
    tm = 1024 if rows % 1024 == 0 else 256
    win = tm // N_GROUPS + tm // 16
    nt = rows // tm
    const = lambda t, e: (0, 0)
    b0 = lambda t: (t * tm) // s
    neb = N_EXPERTS // MOE_EB
    w_gate, w_up, w_down = w_gate.astype(BF16), w_up.astype(BF16), w_down.astype(BF16)
    out = pl.pallas_call(
        functools.partial(_moe_kernel, seq=s, win=win),
        out_shape=jax.ShapeDtypeStruct((rows, d), F32),
        grid=(nt, neb),
        in_specs=[pl.BlockSpec((tm, d), lambda t, e: (t, 0)),
                  pl.BlockSpec((1, 1, 3 * d), lambda t, e: (b0(t), 0, 1)),
                  pl.BlockSpec((1, 1, 3 * d), lambda t, e: (jnp.minimum(b0(t) + 1, nb - 1), 0, 1)),
                  pl.BlockSpec((1, 1, 3 * d), lambda t, e: (nb, 0, 1)),
                  pl.BlockSpec((d, 128), const),
                  pl.BlockSpec((1, 128), const),
                  pl.BlockSpec((MOE_EB, d, D_EXPERT), lambda t, e: (e, 0, 0)),
                  pl.BlockSpec((MOE_EB, d, D_EXPERT), lambda t, e: (e, 0, 0)),
                  pl.BlockSpec((MOE_EB, D_EXPERT, d), lambda t, e: (e, 0, 0)),
                  pl.BlockSpec((1, d), const),
                  pl.BlockSpec((1, d), const)],
        out_specs=pl.BlockSpec((tm, d), lambda t, e: (t, 0)),
        scratch_shapes=[pltpu.VMEM((tm, d), BF16), pltpu.VMEM((tm, 128), F32), pltpu.VMEM((tm, tm), BF16),
                        pltpu.VMEM((tm, d), F32), pltpu.SMEM((2 * N_GROUPS,), jnp.int32)],
        compiler_params=_cparams(("parallel", "arbitrary")),
        name="moe",
    )(h.reshape(rows, d), mods_l, mods_l, mods_l, w_router, b_router, w_gate, w_up, w_down,
      ln_g.astype(F32).reshape(1, d), ln_b.astype(F32).reshape(1, d))
    return out.reshape(nb, s, d)


def _pad_lanes(x, n):
    return jnp.pad(x, [(0, 0)] * (x.ndim - 1) + [(0, n - x.shape[-1])])


def kernel(x, c, ctx, c_ctx, mod_w, mod_b, ln_g, ln_b, even_w_in, even_w_out, s5_lam_re, s5_lam_im, s5_log_dt, s5_b_re, s5_b_im, s5_c_re, s5_c_im, s5_d, s5_glu_w, s5_glu_b, na_rpb, odd_w_in, odd_w_out, gla_gate_w, gla_gate_b, gla_norm_g, diff_lq1, diff_lk1, diff_lq2, diff_lk2, diff_norm_g, moe_w_grp, moe_b_grp, moe_w_exp, moe_b_exp, moe_w_gate, moe_w_up, moe_w_down):
    nb, t, d = x.shape
    assert d == D_MODEL and ctx.shape[1] == CTX and nb < 8 and t % (NA_QR * GRID_W) == 0
    s = CTX + t
    rows = t // GRID_W
    h = jnp.concatenate([ctx, x], axis=1).astype(F32)
    cc = jnp.zeros((8, d), F32).at[:nb].set(c.astype(F32)).at[nb].set(c_ctx.astype(F32))
    mods = _mods(cc, mod_w.astype(F32), mod_b.astype(F32)).reshape(DEPTH, 8, 1, 6 * d)
    rope_tabs = _rope_tables(s)
    for i in range(DEPTH):
        j = i // 2
        mods_l = mods[i]
        if i % 2 == 0:
            a = _inproj(h, mods_l, even_w_in[j].astype(BF16))
            tables = _s5_tables(s5_lam_re[j], s5_lam_im[j], s5_log_dt[j], s5_b_re[j], s5_b_im[j],
                                s5_c_re[j], s5_c_im[j], s5_d[j])
            y_a = _s5(a[..., :HALF], tables)
            y_b = _natten(a, _na_bias_tables(na_rpb[j], rows))
            h = _outproj(y_a, y_b, h, mods_l, even_w_out[j], ln_g[i, 0], ln_b[i, 0],
                         glu_w=s5_glu_w[j], glu_b=s5_glu_b[j])
        else:
            lam_init = 0.8 - 0.6 * math.exp(-0.3 * i)
            w = odd_w_in[j]
            nqk = 2 * GLA_HEADS * GLA_DK + 2 * HALF
            w = jnp.concatenate([w[:, :nqk], w[:, nqk + 2 * GLA_RANK:],
                                 _pad_lanes(w[:, nqk:nqk + 2 * GLA_RANK], 128)], axis=1).astype(BF16)
            a = _inproj(h, mods_l, w)
            gw = gla_gate_w[j].astype(F32)
            wg = jnp.zeros((2, 128, GLA_HEADS * GLA_DK), F32)
            for dd in range(2):
                wg = wg.at[dd, dd * GLA_RANK:(dd + 1) * GLA_RANK].set(gw[dd])
            wg = wg.reshape(2, 128, 2, 2 * GLA_DK).transpose(0, 2, 1, 3)
            bg = gla_gate_b[j].astype(F32).reshape(2, 2, 1, 2 * GLA_DK)
            y_a = _gla(a, wg, bg)
            qr, kr = _rope(a, rope_tabs)
            lam_vecs = jnp.zeros((8, DIFF_DH), F32).at[0].set(diff_lq1[j]).at[1].set(diff_lk1[j]) \
                .at[2].set(diff_lq2[j]).at[3].set(diff_lk2[j])
            y_b = _diff(qr, kr, a, lam_vecs, diff_norm_g[j].astype(F32).reshape(1, DIFF_DV), lam_init)
            h = _outproj(y_a, y_b, h, mods_l, odd_w_out[j], ln_g[i, 0], ln_b[i, 0],
                         proj=a, norm_g=gla_norm_g[j])
        w_router = _pad_lanes(jnp.concatenate([moe_w_grp[i], moe_w_exp[i]], axis=1).astype(F32), 128)
        b_router = _pad_lanes(jnp.concatenate([moe_b_grp[i], moe_b_exp[i]]).astype(F32)[None], 128)
        h = _moe(h, mods_l, w_router, b_router, moe_w_gate[i], moe_w_up[i], moe_w_down[i], ln_g[i, 1], ln_b[i, 1])
    return h[:, CTX:].astype(x.dtype)
```

```python
import functools
import math

import numpy as np
import jax
import jax.numpy as jnp
from jax import lax
from jax.experimental import pallas as pl
from jax.experimental.pallas import tpu as pltpu

F32 = jnp.float32
BF16 = jnp.bfloat16
HIGHEST = lax.Precision.HIGHEST

D_MODEL = 1024
DEPTH = 4
GRID_W = 64
CTX = 256
HALF = D_MODEL // 2
S5_H = 16
S5_G = HALF // S5_H
S5_P = 64
S5_L = 16
NA_HEADS = 8
NA_DH = 64
NA_WR = 8
NA_WC = 16
NA_QR = 4
NA_KR = 12
GLA_HEADS = 4
GLA_DK = 64
GLA_DV = 128
GLA_RANK = 16
GLA_TAU = 16.0
GLA_CHUNK = 64
DIFF_HEADS = 4
DIFF_DV = 128
DIFF_DH = 64
N_GROUPS = 4
EPG = 8
N_EXPERTS = N_GROUPS * EPG
D_EXPERT = D_MODEL // 4
MOE_EB = 4
ROPE_BASE = 10000.0
EPS = 1e-5
ALPHA = (2.0 * DEPTH) ** 0.25
EVEN_IN = 4 * HALF
ODD_IN_PAD = 3200
NEG = -1e30
LOG2E = math.log2(math.e)
ROW_TILE = 256
VMEM_LIMIT = 56 * 1024 * 1024


def _cparams(sem):
    return pltpu.CompilerParams(dimension_semantics=sem, vmem_limit_bytes=VMEM_LIMIT)


def _bdot(a, b):
    return jnp.dot(a.astype(BF16), b.astype(BF16), preferred_element_type=F32)


def _bdot_nt(a, b):
    return lax.dot_general(a.astype(BF16), b.astype(BF16), (((1,), (1,)), ((), ())),
                           preferred_element_type=F32)


def _hdot(a, b):
    return jnp.dot(a, b, precision=HIGHEST, preferred_element_type=F32)


def _layer_norm(x, g, b):
    mu = jnp.mean(x, axis=-1, keepdims=True)
    xc = x - mu
    var = jnp.mean(xc * xc, axis=-1, keepdims=True)
    return xc * lax.rsqrt(var + EPS) * g + b


def _silu(x):
    return x * jax.nn.sigmoid(x)


def _mods_kernel(cc_ref, w_ref, b_ref, o_ref):
    o_ref[0] = _hdot(_silu(cc_ref[...]), w_ref[0]) + b_ref[0]


def _mods(cc, mod_w, mod_b):
    nblk = 6
    return pl.pallas_call(
        _mods_kernel,
        out_shape=jax.ShapeDtypeStruct((DEPTH, 8, 6 * D_MODEL), F32),
        grid=(DEPTH, nblk),
        in_specs=[pl.BlockSpec((8, D_MODEL), lambda l, j: (0, 0)),
                  pl.BlockSpec((1, D_MODEL, D_MODEL), lambda l, j: (l, 0, j)),
                  pl.BlockSpec((1, 1, D_MODEL), lambda l, j: (l, 0, j))],
        out_specs=pl.BlockSpec((1, 8, D_MODEL), lambda l, j: (l, 0, j)),
        compiler_params=_cparams(("arbitrary", "arbitrary")),
        name="mods",
    )(cc, mod_w, mod_b.reshape(DEPTH, 1, 6 * D_MODEL))


def _mod_row(nbatch):
    return lambda b, j: jnp.where(j == 0, nbatch, b)


def _inproj_kernel(h_ref, sh_ref, sc_ref, w_ref, o_ref):
    a = h_ref[0] * (1.0 + sc_ref[0]) + sh_ref[0]
    o_ref[0] = _bdot(a, w_ref[...])


def _inproj(h, mods_l, w):
    nb, s, _ = h.shape
    n = w.shape[1]
    row = _mod_row(nb)
    return pl.pallas_call(
        _inproj_kernel,
        out_shape=jax.ShapeDtypeStruct((nb, s, n), F32),
        grid=(nb, s // ROW_TILE),
        in_specs=[pl.BlockSpec((1, ROW_TILE, D_MODEL), lambda b, j: (b, j, 0)),
                  pl.BlockSpec((1, 1, D_MODEL), lambda b, j: (row(b, j), 0, 0)),
                  pl.BlockSpec((1, 1, D_MODEL), lambda b, j: (row(b, j), 0, 1)),
                  pl.BlockSpec((D_MODEL, n), lambda b, j: (0, 0))],
        out_specs=pl.BlockSpec((1, ROW_TILE, n), lambda b, j: (b, j, 0)),
        compiler_params=_cparams(("parallel", "parallel")),
        name="inproj",
    )(h, mods_l, mods_l, w)


def _s5_tables(lam_re, lam_im, log_dt, b_re, b_im, c_re, c_im, d_skip):
    L, H, P, G = S5_L, S5_H, S5_P, S5_G
    lam_re, lam_im = lam_re.astype(F32), lam_im.astype(F32)
    dt = jnp.exp(log_dt.astype(F32))[..., None]
    mag = jnp.exp(lam_re * dt)
    a_re = mag * jnp.cos(lam_im * dt)
    a_im = mag * jnp.sin(lam_im * dt)
    den = lam_re * lam_re + lam_im * lam_im
    f_re = ((a_re - 1.0) * lam_re + a_im * lam_im) / den
    f_im = (a_im * lam_re - (a_re - 1.0) * lam_im) / den
    b_re, b_im = b_re.astype(F32), b_im.astype(F32)
    bb_re = f_re[..., None] * b_re - f_im[..., None] * b_im
    bb_im = f_re[..., None] * b_im + f_im[..., None] * b_re
    tau = jnp.arange(L + 1, dtype=F32)[:, None, None, None]
    mag_t = jnp.exp(lam_re[None] * dt[None] * tau)
    ang_t = lam_im[None] * dt[None] * tau
    p_re = mag_t * jnp.cos(ang_t)
    p_im = mag_t * jnp.sin(ang_t)
    c_re, c_im = c_re.astype(F32), c_im.astype(F32)
    e_re = c_re[None] * p_re[:, :, :, None, :] - c_im[None] * p_im[:, :, :, None, :]
    e_im = c_re[None] * p_im[:, :, :, None, :] + c_im[None] * p_re[:, :, :, None, :]
    k = (jnp.einsum('tdgjp,dgpi->tdgji', e_re, bb_re, precision=HIGHEST)
         - jnp.einsum('tdgjp,dgpi->tdgji', e_im, bb_im, precision=HIGHEST))
    s_i, t_i = np.meshgrid(np.arange(L), np.arange(L), indexing='ij')
    kf = k[np.clip(t_i - s_i, 0, L), 0] * jnp.asarray(t_i >= s_i, F32)[..., None, None, None]
    kb = k[np.clip(s_i - t_i, 0, L), 1] * jnp.asarray(s_i >= t_i, F32)[..., None, None, None]
    skip = (jnp.asarray(s_i == t_i, F32)[..., None, None, None]
            * (jnp.eye(H, dtype=F32)[None] * d_skip.astype(F32).reshape(G, 1, H))[None, None])
    intra = (kf + kb + skip).transpose(2, 0, 4, 1, 3).reshape(G, L * H, L * H)

    def state_in(d, pw):
        pr, pi = p_re[pw, d], p_im[pw, d]
        re = pr[..., None] * bb_re[d][None] - pi[..., None] * bb_im[d][None]
        im = pr[..., None] * bb_im[d][None] + pi[..., None] * bb_re[d][None]
        return jnp.concatenate([re, im], axis=2).transpose(1, 0, 3, 2).reshape(G, L * H, 2 * P)

    sf = state_in(0, np.arange(L)[::-1].copy())
    sb = state_in(1, np.arange(L))
    w1 = jnp.concatenate([intra, sf, sb], axis=-1)

    def state_out(d, pw):
        er, ei = e_re[pw, d], e_im[pw, d]
        return jnp.concatenate([er, -ei], axis=-1).transpose(1, 3, 0, 2).reshape(G, 2 * P, L * H)

    w2 = jnp.concatenate([state_out(0, np.arange(1, L + 1)),
                          state_out(1, np.arange(L, 0, -1))], axis=1)
    ar, ai = p_re[L], p_im[L]
    rows = []
    for d in range(2):
        rows += [jnp.concatenate([ar[d], ar[d]], -1), jnp.concatenate([-ai[d], ai[d]], -1),
                 jnp.concatenate([ai[d], -ai[d]], -1)]
    rows += [jnp.zeros_like(rows[0])] * 2
    al = jnp.stack(rows, axis=1)
    return w1.astype(BF16), w2.astype(BF16), al


def _s5_kernel(x_ref, w1_ref, w2_ref, al_ref, y_ref, xw_ref, sw_ref, hh_ref, *, nch, nctx):
    lh = S5_L * S5_H
    p2 = 2 * S5_P
    xw_ref[...] = jnp.dot(x_ref[0], w1_ref[0], preferred_element_type=F32)
    sw_ref[:, 0:p2] = pltpu.roll(xw_ref[:, lh:lh + p2], S5_P, 1)
    sw_ref[:, p2:2 * p2] = pltpu.roll(xw_ref[:, lh + p2:lh + 2 * p2], S5_P, 1)
    al = al_ref[0]
    a1f, a2f, a3f, a1b, a2b, a3b = [al[i:i + 1] for i in range(6)]

    def body(k, carry):
        vf, wf, vb, wb = carry
        kb = jnp.where(k < nctx, nctx - 1 - k, nch + nctx - 1 - k)
        rf = pl.ds(pl.multiple_of(k * 8, 8), 8)
        rb = pl.ds(pl.multiple_of(kb * 8, 8), 8)
        hh_ref[rf, 0:p2] = vf
        hh_ref[rb, p2:2 * p2] = vb
        s_f = xw_ref[rf, lh:lh + p2]
        s_fs = sw_ref[rf, 0:p2]
        s_b = xw_ref[rb, lh + p2:lh + 2 * p2]
        s_bs = sw_ref[rb, p2:2 * p2]
        return (vf * a1f + wf * a2f + s_f, wf * a1f + vf * a3f + s_fs,
                vb * a1b + wb * a2b + s_b, wb * a1b + vb * a3b + s_bs)

    zero = jnp.zeros((8, p2), F32)
    lax.fori_loop(0, nch, body, (zero,) * 4)
    y_ref[0] = (xw_ref[:, 0:lh] + _bdot(hh_ref[...], w2_ref[0])).astype(BF16)


def _s5(u, tables):
    nb, s, _ = u.shape
    nch = s // S5_L
    lh = S5_L * S5_H
    r = nch * 8
    x = u.astype(BF16).reshape(nb, nch, S5_L, S5_G, S5_H).transpose(3, 1, 0, 2, 4)
    x = jnp.pad(x, ((0, 0), (0, 0), (0, 8 - nb), (0, 0), (0, 0))).reshape(S5_G, r, lh)
    w1, w2, al = tables
    y = pl.pallas_call(
        functools.partial(_s5_kernel, nch=nch, nctx=CTX // S5_L),
        out_shape=jax.ShapeDtypeStruct((S5_G, r, lh), BF16),
        grid=(S5_G,),
        in_specs=[pl.BlockSpec((1, r, lh), lambda g: (g, 0, 0)),
                  pl.BlockSpec((1, lh, 2 * lh), lambda g: (g, 0, 0)),
                  pl.BlockSpec((1, lh, lh), lambda g: (g, 0, 0)),
                  pl.BlockSpec((1, 8, 2 * S5_P), lambda g: (g, 0, 0))],
        out_specs=pl.BlockSpec((1, r, lh), lambda g: (g, 0, 0)),
        scratch_shapes=[pltpu.VMEM((r, 2 * lh), F32), pltpu.VMEM((r, lh), F32), pltpu.VMEM((r, lh), F32)],
        compiler_params=_cparams(("parallel",)),
        name="s5",
    )(x, w1, w2, al)
    y = y.reshape(S5_G, nch, 8, S5_L, S5_H)[:, :, :nb]
    return y.transpose(2, 1, 3, 0, 4).reshape(nb, s, HALF)


def _na_bias_tables(rpb, rows):
    w = GRID_W
    nh, ndr, ndc = rpb.shape
    v = jnp.pad(rpb.astype(F32), ((0, 0), (0, 0), (w - NA_WC, 2 * w - ndc - (w - NA_WC))))
    skew = jnp.tile(v, (1, 1, w))[:, :, :w * (2 * w - 1)].reshape(nh, ndr, w, 2 * w - 1)
    tc = skew[..., w - 1:]
    valid, dr = [], []
    for r0 in (0, NA_QR, rows - NA_QR):
        k0 = int(np.clip(r0 - NA_WR // 2, 0, rows - NA_KR))
        rq = r0 + np.arange(NA_QR)[:, None, None, None]
        wq = np.arange(w)[None, :, None, None]
        kr = k0 + np.arange(NA_KR)[None, None, :, None]
        kc = np.arange(w)[None, None, None, :]
        rs = np.clip(rq - NA_WR // 2, 0, rows - NA_WR)
        cs = np.clip(wq - NA_WC // 2, 0, w - NA_WC)
        valid.append((kr >= rs) & (kr < rs + NA_WR) & (kc >= cs) & (kc < cs + NA_WC))
        dr.append(np.clip(kr - rq + NA_WR - 1, 0, ndr - 1)[:, 0, :, 0])
    dr = np.stack(dr).reshape(-1)
    bias = jnp.take(tc, jnp.asarray(dr), axis=1)
    bias = bias.reshape(nh, 3, NA_QR, NA_KR, w, w).transpose(0, 1, 2, 4, 3, 5)
    bias = jnp.where(jnp.asarray(np.stack(valid))[None], bias, NEG)
    return bias.reshape(nh, 3, NA_QR * w, NA_KR * w)


def _natten_kernel(q_ref, k_ref, v_ref, bias_ref, o_ref, qb_ref, kb_ref, vb_ref, *, rows):
    nq = NA_QR * GRID_W
    nk = NA_KR * GRID_W
    nblk = rows // NA_QR
    scale = NA_DH ** -0.5
    qb_ref[...] = (q_ref[0] * scale).astype(BF16)
    kb_ref[...] = k_ref[0].astype(BF16)
    vb_ref[...] = v_ref[0].astype(BF16)
    lane = lax.broadcasted_iota(jnp.int32, (1, 2 * NA_DH), 1)
    head_mask = [lane < NA_DH, lane >= NA_DH]
    kc = kb_ref[0:CTX, :]
    vc = vb_ref[0:CTX, :]

    def attend(q, extra):
        outs = []
        for h in range(2):
            qm = jnp.where(head_mask[h], q, jnp.zeros_like(q))
            s_c = _bdot_nt(qm, kc)
            m = jnp.max(s_c, axis=-1, keepdims=True)
            if extra is not None:
                k_loc, v_loc, bias = extra
                s_l = _bdot_nt(qm, k_loc) + bias[h]
                m = jnp.maximum(m, jnp.max(s_l, axis=-1, keepdims=True))
                p_l = jnp.exp(s_l - m)
            p_c = jnp.exp(s_c - m)
            den = jnp.sum(p_c, axis=-1, keepdims=True)
            acc = _bdot(p_c, vc)
            if extra is not None:
                den = den + jnp.sum(p_l, axis=-1, keepdims=True)
                acc = acc + _bdot(p_l, v_loc)
            outs.append(acc / den)
        return jnp.where(head_mask[0], outs[0], outs[1])

    o_ref[0, 0:CTX, :] = attend(qb_ref[0:CTX, :], None)

    def body(i, carry):
        r0 = i * NA_QR
        k0 = jnp.clip(r0 - NA_WR // 2, 0, rows - NA_KR)
        pat = jnp.where(i == 0, 0, jnp.where(i == nblk - 1, 2, 1))
        qs = pl.multiple_of(CTX + r0 * GRID_W, GRID_W)
        ks = pl.multiple_of(CTX + k0 * GRID_W, GRID_W)
        q = qb_ref[pl.ds(qs, nq), :]
        k_loc = kb_ref[pl.ds(ks, nk), :]
        v_loc = vb_ref[pl.ds(ks, nk), :]
        bias = [bias_ref[h, pat] for h in range(2)]
        o_ref[0, pl.ds(qs, nq), :] = attend(q, (k_loc, v_loc, bias))
        return carry

    lax.fori_loop(0, nblk, body, 0)


def _natten(a, bias):
    nb, s, _ = a.shape
    rows = (s - CTX) // GRID_W
    pw = 2 * NA_DH
    nq, nk = NA_QR * GRID_W, NA_KR * GRID_W
    hb = HALF // pw
    return pl.pallas_call(
        functools.partial(_natten_kernel, rows=rows),
        out_shape=jax.ShapeDtypeStruct((nb, s, HALF), F32),
        grid=(nb, hb),
        in_specs=[pl.BlockSpec((1, s, pw), lambda b, h: (b, 0, hb + h)),
                  pl.BlockSpec((1, s, pw), lambda b, h: (b, 0, 2 * hb + h)),
                  pl.BlockSpec((1, s, pw), lambda b, h: (b, 0, 3 * hb + h)),
                  pl.BlockSpec((2, 3, nq, nk), lambda b, h: (h, 0, 0, 0))],
        out_specs=pl.BlockSpec((1, s, pw), lambda b, h: (b, 0, h)),
        scratch_shapes=[pltpu.VMEM((s, pw), BF16)] * 3,
        compiler_params=_cparams(("parallel", "parallel")),
        name="natten",
    )(a, a, a, bias)


def _gelu_tanh(x):
    return 0.5 * x * (1.0 + jnp.tanh(math.sqrt(2.0 / math.pi) * (x + 0.044715 * (x * x * x))))


def _out_kernel(ya_ref, yb_ref, h_ref, g_ref, e0_ref, e1_ref, wo_ref, lng_ref, lnb_ref, o_ref, *, glu):
    if glu:
        z = _bdot(_gelu_tanh(ya_ref[0].astype(F32)), e0_ref[...]) + e1_ref[...]
        ya = z[:, :HALF] * jax.nn.sigmoid(z[:, HALF:])
    else:
        parts = []
        for hd in range(GLA_HEADS):
            cols = slice(hd * GLA_DV, (hd + 1) * GLA_DV)
            o = ya_ref[0, :, cols]
            o = o * lax.rsqrt(jnp.mean(o * o, axis=-1, keepdims=True) + EPS) * e1_ref[:, cols]
            parts.append(o * _silu(e0_ref[0, :, cols]))
        ya = jnp.concatenate(parts, axis=1)
    y = _bdot(ya, wo_ref[0:HALF, :]) + _bdot(yb_ref[0], wo_ref[HALF:, :])
    o_ref[0] = _layer_norm(ALPHA * h_ref[0] + g_ref[0] * y, lng_ref[...], lnb_ref[...])


def _outproj(ya, yb, h, mods_l, w_out, ln_g, ln_b, *, glu_w=None, glu_b=None, proj=None, norm_g=None):
    nb, s, _ = h.shape
    row = _mod_row(nb)
    glu = glu_w is not None
    tile = lambda b, j: (b, j, 0)
    const = lambda b, j: (0, 0)
    if glu:
        extra_specs = [pl.BlockSpec((HALF, D_MODEL), const), pl.BlockSpec((1, D_MODEL), const)]
        extra = [glu_w.astype(BF16), glu_b.astype(F32).reshape(1, D_MODEL)]
    else:
        rblk = (2 * GLA_HEADS * GLA_DK + HALF) // HALF
        extra_specs = [pl.BlockSpec((1, ROW_TILE, HALF), lambda b, j: (b, j, rblk)),
                       pl.BlockSpec((1, HALF), const)]
        extra = [proj, norm_g.astype(F32).reshape(1, HALF)]
    in_specs = [pl.BlockSpec((1, ROW_TILE, HALF), tile),
                pl.BlockSpec((1, ROW_TILE, HALF), tile),
                pl.BlockSpec((1, ROW_TILE, D_MODEL), tile),
                pl.BlockSpec((1, 1, D_MODEL), lambda b, j: (row(b, j), 0, 2))] + extra_specs + [
                pl.BlockSpec((D_MODEL, D_MODEL), const), pl.BlockSpec((1, D_MODEL), const),
                pl.BlockSpec((1, D_MODEL), const)]
    return pl.pallas_call(
        functools.partial(_out_kernel, glu=glu),
        out_shape=jax.ShapeDtypeStruct((nb, s, D_MODEL), F32),
        grid=(nb, s // ROW_TILE),
        in_specs=in_specs,
        out_specs=pl.BlockSpec((1, ROW_TILE, D_MODEL), tile),
        compiler_params=_cparams(("parallel", "parallel")),
        name="outproj_glu" if glu else "outproj_gate",
    )(ya, yb, h, mods_l, *extra, w_out.astype(BF16), ln_g.astype(F32).reshape(1, D_MODEL),
      ln_b.astype(F32).reshape(1, D_MODEL))


def _log_sigmoid(x):
    return jnp.minimum(x, 0.0) - jnp.log1p(jnp.exp(-jnp.abs(x)))


def _gla_kernel(q_ref, k_ref, v_ref, z_ref, wg_ref, bg_ref, o_ref, gf_ref, gb_ref, ob_ref, st_ref, *, nch):
    c = GLA_CHUNK
    dk2 = 2 * GLA_DK
    nctx = CTX // c
    z = z_ref[0]
    gf_ref[...] = _log_sigmoid(_hdot(z, wg_ref[0, 0]) + bg_ref[0, 0]) * (1.0 / GLA_TAU)
    gb_ref[...] = _log_sigmoid(_hdot(z, wg_ref[1, 0]) + bg_ref[1, 0]) * (1.0 / GLA_TAU)
    ri = lax.broadcasted_iota(jnp.int32, (c, c), 0)
    ci = lax.broadcasted_iota(jnp.int32, (c, c), 1)
    lane = lax.broadcasted_iota(jnp.int32, (1, dk2), 1)
    head_mask = [lane < GLA_DK, lane >= GLA_DK]
    scale = GLA_DK ** -0.5

    st_ref[...] = jnp.zeros_like(st_ref)

    def chunk(direction, n):
        g_ref = gf_ref if direction == 0 else gb_ref
        keep = (ci <= ri) if direction == 0 else (ci >= ri)
        tri = jnp.where(keep, 1.0, 0.0).astype(BF16)
        rs = pl.ds(pl.multiple_of(n * c, c), c)
        g = g_ref[rs, :]
        g_hi = g.astype(BF16)
        g_lo = (g - g_hi.astype(F32)).astype(BF16)
        gc = (jnp.dot(tri, g_hi, preferred_element_type=F32)
              + jnp.dot(tri, g_lo, preferred_element_type=F32))
        gend = gc[c - 1:c, :] if direction == 0 else gc[0:1, :]
        q = q_ref[0, rs, :] * scale
        k = k_ref[0, rs, :]
        qd = q * jnp.exp(gc)
        kd = k * jnp.exp(-gc)
        kdec = k * jnp.exp(gend - gc)
        dcol = jnp.exp(jnp.broadcast_to(gend, (dk2, dk2)).T)
        dcol = jnp.concatenate([dcol, dcol], axis=1)
        qs = jnp.concatenate([jnp.where(head_mask[0], qd, 0.0), jnp.where(head_mask[1], qd, 0.0)],
                             axis=0).astype(BF16)
        att = jnp.where(jnp.concatenate([keep, keep], axis=0), _bdot_nt(qs, kd), 0.0)
        v = v_ref[0, rs, :].astype(BF16)
        st = st_ref[direction]
        o2 = _bdot(att, v) + _bdot(qs, st)
        st_ref[direction] = dcol * st + jnp.dot(kdec.T.astype(BF16), v, preferred_element_type=F32)
        o = jnp.concatenate([o2[0:c, 0:GLA_DV], o2[c:2 * c, GLA_DV:2 * GLA_DV]], axis=1)
        if direction == 0:
            o_ref[0, rs, :] = o
        else:
            ob_ref[rs, :] = o

    def body(i, carry):
        chunk(0, i)
        chunk(1, jnp.where(i < nctx, nctx - 1 - i, nch + nctx - 1 - i))
        return carry

    lax.fori_loop(0, nch, body, 0, unroll=2)
    o_ref[0] += ob_ref[...]


def _gla(a, wg, bg):
    nb, s, _ = a.shape
    nch = s // GLA_CHUNK
    hp = GLA_HEADS // 2
    dk2, dv2 = 2 * GLA_DK, 2 * GLA_DV
    zblk = (2 * GLA_HEADS * GLA_DK + 5 * HALF) // 128
    return pl.pallas_call(
        functools.partial(_gla_kernel, nch=nch),
        out_shape=jax.ShapeDtypeStruct((nb, s, HALF), F32),
        grid=(nb, hp),
        in_specs=[pl.BlockSpec((1, s, dk2), lambda b, h: (b, 0, h)),
                  pl.BlockSpec((1, s, dk2), lambda b, h: (b, 0, hp + h)),
                  pl.BlockSpec((1, s, dv2), lambda b, h: (b, 0, hp + h)),
                  pl.BlockSpec((1, s, 128), lambda b, h: (b, 0, zblk)),
                  pl.BlockSpec((2, 1, 128, dk2), lambda b, h: (0, h, 0, 0)),
                  pl.BlockSpec((2, 1, 1, dk2), lambda b, h: (0, h, 0, 0))],
        out_specs=pl.BlockSpec((1, s, dv2), lambda b, h: (b, 0, h)),
        scratch_shapes=[pltpu.VMEM((s, dk2), F32), pltpu.VMEM((s, dk2), F32), pltpu.VMEM((s, dv2), F32),
                        pltpu.VMEM((2, dk2, dv2), F32)],
        compiler_params=_cparams(("parallel", "parallel")),
        name="gla",
    )(a, a, a, a, wg, bg)


def _rope_tables(s):
    t = np.arange(s - CTX)
    n = DIFF_DH // 4
    inv = ROPE_BASE ** (-np.arange(n, dtype=np.float64) / n)
    ang_r = (t // GRID_W)[:, None] * inv[None]
    ang_c = (t % GRID_W)[:, None] * inv[None]
    ang = np.concatenate([ang_r, ang_r, ang_c, ang_c], axis=1)
    ang = np.concatenate([ang, ang], axis=1)
    cos = np.concatenate([np.ones((CTX, 128)), np.cos(ang)], axis=0)
    sin = np.concatenate([np.zeros((CTX, 128)), np.sin(ang)], axis=0)
    first = ((np.arange(128) // n) % 2 == 0)[None]
    sin_up = np.where(first, -sin, 0.0)
    sin_dn = np.where(first, 0.0, sin)
    f = lambda z: jnp.asarray(z, F32)
    return f(cos), f(sin_up), f(sin_dn)


def _rope_kernel(q_ref, k_ref, cos_ref, su_ref, sd_ref, qo_ref, ko_ref):
    n = DIFF_DH // 4
    cos = jnp.concatenate([cos_ref[...]] * DIFF_HEADS, axis=1)
    su = jnp.concatenate([su_ref[...]] * DIFF_HEADS, axis=1)
    sd = jnp.concatenate([sd_ref[...]] * DIFF_HEADS, axis=1)

    def rot(x):
        w = x.shape[1]
        return x * cos + pltpu.roll(x, w - n, 1) * su + pltpu.roll(x, n, 1) * sd

    qo_ref[0] = rot(q_ref[0] * (DIFF_DH ** -0.5 * LOG2E)).astype(BF16)
    ko_ref[0] = rot(k_ref[0]).astype(BF16)


def _rope(a, tables):
    nb, s, _ = a.shape
    qblk = (2 * GLA_HEADS * GLA_DK + 2 * HALF) // HALF
    tile = lambda b, j: (b, j, 0)
    tab = pl.BlockSpec((ROW_TILE, 128), lambda b, j: (j, 0))
    return pl.pallas_call(
        _rope_kernel,
        out_shape=[jax.ShapeDtypeStruct((nb, s, HALF), BF16)] * 2,
        grid=(nb, s // ROW_TILE),
        in_specs=[pl.BlockSpec((1, ROW_TILE, HALF), lambda b, j: (b, j, qblk)),
                  pl.BlockSpec((1, ROW_TILE, HALF), lambda b, j: (b, j, qblk + 1)),
                  tab, tab, tab],
        out_specs=[pl.BlockSpec((1, ROW_TILE, HALF), tile)] * 2,
        compiler_params=_cparams(("parallel", "parallel")),
        name="rope",
    )(a, a, *tables)


def _diff_kernel(q_ref, k_ref, v_ref, lam_ref, ng_ref, o_ref, vb_ref, *, lam_init):
    j = pl.program_id(2)
    lane = lax.broadcasted_iota(jnp.int32, (1, 2 * DIFF_DH), 1)
    map_mask = [lane < DIFF_DH, lane >= DIFF_DH]
    lv = lam_ref[...]
    lam = (jnp.exp(jnp.sum(lv[0:1] * lv[1:2], axis=-1, keepdims=True))
           - jnp.exp(jnp.sum(lv[2:3] * lv[3:4], axis=-1, keepdims=True)) + lam_init)

    @pl.when(j == 0)
    def _():
        vb_ref[:, 0:DIFF_DV] = v_ref[0].astype(BF16)
        vb_ref[:, DIFF_DV:] = jnp.ones((vb_ref.shape[0], DIFF_DV), BF16)

    def attend(nkeys):
        q = q_ref[0]
        k = k_ref[0, 0:nkeys, :]
        v1 = vb_ref[0:nkeys, :]
        acc = []
        for m in range(2):
            s = _bdot_nt(jnp.where(map_mask[m], q, jnp.zeros_like(q)), k)
            p = jnp.exp2(s - jnp.max(s, axis=-1, keepdims=True)).astype(BF16)
            acc.append(jnp.dot(p, v1, preferred_element_type=F32))
        o = (acc[0][:, :DIFF_DV] * (1.0 / acc[0][:, DIFF_DV:DIFF_DV + 1])
             - acc[1][:, :DIFF_DV] * (lam / acc[1][:, DIFF_DV:DIFF_DV + 1]))
        o = o * lax.rsqrt(jnp.mean(o * o, axis=-1, keepdims=True) + EPS) * ng_ref[...]
        o_ref[0] = o * (1.0 - lam_init)

    @pl.when(j == 0)
    def _():
        attend(CTX)

    @pl.when(j > 0)
    def _():
        attend(k_ref.shape[1])


def _diff(qr, kr, a, lam_vecs, ng, lam_init):
    nb, s, _ = qr.shape
    dw = 2 * DIFF_DH
    vblk = (2 * GLA_HEADS * GLA_DK + 4 * HALF) // dw
    return pl.pallas_call(
        functools.partial(_diff_kernel, lam_init=lam_init),
        out_shape=jax.ShapeDtypeStruct((nb, s, HALF), F32),
        grid=(nb, DIFF_HEADS, s // ROW_TILE),
        in_specs=[pl.BlockSpec((1, ROW_TILE, dw), lambda b, h, j: (b, j, h)),
                  pl.BlockSpec((1, s, dw), lambda b, h, j: (b, 0, h)),
                  pl.BlockSpec((1, s, DIFF_DV), lambda b, h, j: (b, 0, vblk + h)),
                  pl.BlockSpec((8, DIFF_DH), lambda b, h, j: (0, 0)),
                  pl.BlockSpec((1, DIFF_DV), lambda b, h, j: (0, 0))],
        out_specs=pl.BlockSpec((1, ROW_TILE, DIFF_DV), lambda b, h, j: (b, j, h)),
        scratch_shapes=[pltpu.VMEM((s, 2 * DIFF_DV), BF16)],
        compiler_params=_cparams(("parallel", "parallel", "arbitrary")),
        name="diff_attn",
    )(qr, kr, a, lam_vecs, ng)


def _moe_kernel(h_ref, m0_ref, m1_ref, mc_ref, wr_ref, br_ref, wg_ref, wu_ref, wd_ref, lng_ref, lnb_ref, o_ref,
                as_ref, cs_ref, ps_ref, ys_ref, win_ref, *, seq, win):
    t = pl.program_id(0)
    e = pl.program_id(1)
    tm = h_ref.shape[0]
    lane = lax.broadcasted_iota(jnp.int32, (1, 128), 1).astype(F32)
    d = D_MODEL

    def mod(col):
        b0 = (t * tm) // seq
        grow = t * tm + lax.broadcasted_iota(jnp.int32, (tm, 1), 0)
        in_next = grow >= (b0 + 1) * seq
        local = grow - jnp.where(in_next, b0 + 1, b0) * seq
        cols = slice(col * d, (col + 1) * d)
        return jnp.where(local < CTX, mc_ref[0, :, cols],
                         jnp.where(in_next, m1_ref[0, :, cols], m0_ref[0, :, cols]))

    @pl.when(e == 0)
    def _():
        a = h_ref[...] * (1.0 + mod(1)) + mod(0)
        logits = _hdot(a, wr_ref[...]) + br_ref[...]
        is_g = lane < N_GROUPS
        lg = jnp.where(is_g, logits, NEG)
        mg = jnp.max(lg, axis=-1, keepdims=True)
        g_sel = jnp.min(jnp.where(lg == mg, lane, 128.0), axis=-1, keepdims=True)
        p_grp = 1.0 / jnp.sum(jnp.where(is_g, jnp.exp(lg - mg), 0.0), axis=-1, keepdims=True)
        lo = N_GROUPS + g_sel * EPG
        in_grp = (lane >= lo) & (lane < lo + EPG)
        le = jnp.where(in_grp, logits, NEG)
        v1 = jnp.max(le, axis=-1, keepdims=True)
        i1 = jnp.min(jnp.where(le == v1, lane, 128.0), axis=-1, keepdims=True)
        le2 = jnp.where(lane == i1, NEG, le)
        v2 = jnp.max(le2, axis=-1, keepdims=True)
        i2 = jnp.min(jnp.where(le2 == v2, lane, 128.0), axis=-1, keepdims=True)
        e2 = jnp.exp(v2 - v1)
        w1 = p_grp / (1.0 + e2)
        w2 = p_grp * e2 / (1.0 + e2)
        comb = jnp.where(lane == i1, w1, jnp.where(lane == i2, w2, 0.0))

        onehot = jnp.where(lane == g_sel, 1.0, 0.0).astype(BF16)
        ri = lax.broadcasted_iota(jnp.int32, (tm, tm), 0)
        ci = lax.broadcasted_iota(jnp.int32, (tm, tm), 1)
        ranks = jnp.dot(jnp.where(ci <= ri, 1.0, 0.0).astype(BF16), onehot, preferred_element_type=F32)
        cnt = ranks[tm - 1:tm, :]
        n = [jnp.sum(jnp.where(lane == float(g), cnt, 0.0), axis=-1, keepdims=True) for g in range(N_GROUPS)]
        off = [jnp.zeros_like(n[0]), n[0], n[0] + n[1], n[0] + n[1] + n[2]]
        off_vec = sum(jnp.where(lane == float(g), off[g], 0.0) for g in range(1, N_GROUPS))
        dest = jnp.sum(jnp.where(lane == g_sel, ranks + off_vec, 0.0), axis=-1, keepdims=True) - 1.0

        pos = lax.broadcasted_iota(jnp.int32, (1, tm), 1).astype(F32)
        ps_ref[...] = jnp.where(dest == pos, 1.0, 0.0).astype(BF16)
        hi = jnp.floor(dest * (1.0 / 32.0))
        digits = jnp.where(lane == 0.0, hi, jnp.where(lane == 1.0, dest - 32.0 * hi, 0.0)).astype(BF16)
        ri8 = lax.broadcasted_iota(jnp.int32, (8, 128), 0)
        ci8 = lax.broadcasted_iota(jnp.int32, (8, 128), 1)
        dig_t = _bdot_nt(jnp.where(ri8 == ci8, 1.0, 0.0), digits)
        dest_row = 32.0 * dig_t[0:1, :] + dig_t[1:2, :]
        rows = lax.broadcasted_iota(jnp.int32, (tm, 1), 0).astype(F32)
        pg = jnp.where(dest_row == rows, 1.0, 0.0).astype(BF16)
        as_ref[...] = jnp.dot(pg, a.astype(BF16), preferred_element_type=F32).astype(BF16)
        c_hi = comb.astype(BF16)
        c_lo = (comb - c_hi.astype(F32)).astype(BF16)
        packed = jnp.where(lane < 64.0, c_hi.astype(F32), pltpu.roll(c_lo.astype(F32), 64, 1)).astype(BF16)
        cg = jnp.dot(pg, packed, preferred_element_type=F32)
        cs_ref[...] = cg + pltpu.roll(cg, 64, 1)
        ys_ref[...] = jnp.zeros_like(ys_ref)
        for g in range(N_GROUPS):
            start = jnp.minimum((jnp.sum(off[g]).astype(jnp.int32) // 16) * 16, tm - win)
            end = jnp.sum(off[g] + n[g]).astype(jnp.int32)
            win_ref[g] = start
            win_ref[N_GROUPS + g] = jnp.where(end > start, (end - start + win - 1) // win, 0)

    g = (e * MOE_EB) // EPG
    start = win_ref[g]

    def window(w, carry):
        lo = start + w * win
        rs = pl.ds(pl.multiple_of(jnp.minimum(lo, tm - win), 16), win)
        x = as_ref[rs, :]
        row = jnp.minimum(lo, tm - win) + lax.broadcasted_iota(jnp.int32, (win, 1), 0)
        cs = jnp.where(row >= lo, cs_ref[rs, :], 0.0)
        upd = None
        for j in range(MOE_EB):
            sel = lane == (e * MOE_EB + j + N_GROUPS).astype(F32)
            cw = jnp.sum(jnp.where(sel, cs, 0.0), axis=-1, keepdims=True)
            hg = jnp.dot(x, wg_ref[j], preferred_element_type=F32)
            hu = jnp.dot(x, wu_ref[j], preferred_element_type=F32)
            dn = jnp.dot((_silu(hg) * hu * cw).astype(BF16), wd_ref[j], preferred_element_type=F32)
            upd = dn if upd is None else upd + dn
        ys_ref[rs, :] += upd
        return carry

    lax.fori_loop(0, win_ref[N_GROUPS + g], window, 0)

    @pl.when(e == N_EXPERTS // MOE_EB - 1)
    def _():
        y = jnp.dot(ps_ref[...], ys_ref[...].astype(BF16), preferred_element_type=F32)
        o_ref[...] = _layer_norm(ALPHA * h_ref[...] + mod(2) * y, lng_ref[...], lnb_ref[...])


def _moe(h, mods_l, w_router, b_router, w_gate, w_up, w_down, ln_g, ln_b):
    nb, s, d = h.shape
    rows = nb * s
    tm = 1024 if rows % 1024 == 0 else 256
    win = tm // N_GROUPS + tm // 16
    nt = rows // tm
    const = lambda t, e: (0, 0)
    b0 = lambda t: (t * tm) // s
    neb = N_EXPERTS // MOE_EB
    w_gate, w_up, w_down = w_gate.astype(BF16), w_up.astype(BF16), w_down.astype(BF16)
    out = pl.pallas_call(
        functools.partial(_moe_kernel, seq=s, win=win),
        out_shape=jax.ShapeDtypeStruct((rows, d), F32),
        grid=(nt, neb),
        in_specs=[pl.BlockSpec((tm, d), lambda t, e: (t, 0)),
                  pl.BlockSpec((1, 1, 3 * d), lambda t, e: (b0(t), 0, 1)),
                  pl.BlockSpec((1, 1, 3 * d), lambda t, e: (jnp.minimum(b0(t) + 1, nb - 1), 0, 1)),
                  pl.BlockSpec((1, 1, 3 * d), lambda t, e: (nb, 0, 1)),
                  pl.BlockSpec((d, 128), const),
                  pl.BlockSpec((1, 128), const),
                  pl.BlockSpec((MOE_EB, d, D_EXPERT), lambda t, e: (e, 0, 0)),
                  pl.BlockSpec((MOE_EB, d, D_EXPERT), lambda t, e: (e, 0, 0)),
                  pl.BlockSpec((MOE_EB, D_EXPERT, d), lambda t, e: (e, 0, 0)),
                  pl.BlockSpec((1, d), const),
                  pl.BlockSpec((1, d), const)],
        out_specs=pl.BlockSpec((tm, d), lambda t, e: (t, 0)),
        scratch_shapes=[pltpu.VMEM((tm, d), BF16), pltpu.VMEM((tm, 128), F32), pltpu.VMEM((tm, tm), BF16),
                        pltpu.VMEM((tm, d), F32), pltpu.SMEM((2 * N_GROUPS,), jnp.int32)],
        compiler_params=_cparams(("parallel", "arbitrary")),
        name="moe",
    )(h.reshape(rows, d), mods_l, mods_l, mods_l, w_router, b_router, w_gate, w_up, w_down,
      ln_g.astype(F32).reshape(1, d), ln_b.astype(F32).reshape(1, d))
    return out.reshape(nb, s, d)


def _pad_lanes(x, n):
    return jnp.pad(x, [(0, 0)] * (x.ndim - 1) + [(0, n - x.shape[-1])])


def kernel(x, c, ctx, c_ctx, mod_w, mod_b, ln_g, ln_b, even_w_in, even_w_out, s5_lam_re, s5_lam_im, s5_log_dt, s5_b_re, s5_b_im, s5_c_re, s5_c_im, s5_d, s5_glu_w, s5_glu_b, na_rpb, odd_w_in, odd_w_out, gla_gate_w, gla_gate_b, gla_norm_g, diff_lq1, diff_lk1, diff_lq2, diff_lk2, diff_norm_g, moe_w_grp, moe_b_grp, moe_w_exp, moe_b_exp, moe_w_gate, moe_w_up, moe_w_down):
    nb, t, d = x.shape
    assert d == D_MODEL and ctx.shape[1] == CTX and nb < 8 and t % (NA_QR * GRID_W) == 0
    s = CTX + t
    rows = t // GRID_W
    h = jnp.concatenate([ctx, x], axis=1).astype(F32)
    cc = jnp.zeros((8, d), F32).at[:nb].set(c.astype(F32)).at[nb].set(c_ctx.astype(F32))
    mods = _mods(cc, mod_w.astype(F32), mod_b.astype(F32)).reshape(DEPTH, 8, 1, 6 * d)
    rope_tabs = _rope_tables(s)
    for i in range(DEPTH):
        j = i // 2
        mods_l = mods[i]
        if i % 2 == 0:
            a = _inproj(h, mods_l, even_w_in[j].astype(BF16))
            tables = _s5_tables(s5_lam_re[j], s5_lam_im[j], s5_log_dt[j], s5_b_re[j], s5_b_im[j],
                                s5_c_re[j], s5_c_im[j], s5_d[j])
            y_a = _s5(a[..., :HALF], tables)
            y_b = _natten(a, _na_bias_tables(na_rpb[j], rows))
            h = _outproj(y_a, y_b, h, mods_l, even_w_out[j], ln_g[i, 0], ln_b[i, 0],
                         glu_w=s5_glu_w[j], glu_b=s5_glu_b[j])
        else:
            lam_init = 0.8 - 0.6 * math.exp(-0.3 * i)
            w = odd_w_in[j]
            nqk = 2 * GLA_HEADS * GLA_DK + 2 * HALF
            w = jnp.concatenate([w[:, :nqk], w[:, nqk + 2 * GLA_RANK:],
                                 _pad_lanes(w[:, nqk:nqk + 2 * GLA_RANK], 128)], axis=1).astype(BF16)
            a = _inproj(h, mods_l, w)
            gw = gla_gate_w[j].astype(F32)
            wg = jnp.zeros((2, 128, GLA_HEADS * GLA_DK), F32)
            for dd in range(2):
                wg = wg.at[dd, dd * GLA_RANK:(dd + 1) * GLA_RANK].set(gw[dd])
            wg = wg.reshape(2, 128, 2, 2 * GLA_DK).transpose(0, 2, 1, 3)
            bg = gla_gate_b[j].astype(F32).reshape(2, 2, 1, 2 * GLA_DK)
            y_a = _gla(a, wg, bg)
            qr, kr = _rope(a, rope_tabs)
            lam_vecs = jnp.zeros((8, DIFF_DH), F32).at[0].set(diff_lq1[j]).at[1].set(diff_lk1[j]) \
                .at[2].set(diff_lq2[j]).at[3].set(diff_lk2[j])
            y_b = _diff(qr, kr, a, lam_vecs, diff_norm_g[j].astype(F32).reshape(1, DIFF_DV), lam_init)
            h = _outproj(y_a, y_b, h, mods_l, odd_w_out[j], ln_g[i, 0], ln_b[i, 0],
                         proj=a, norm_g=gla_norm_g[j])
        w_router = _pad_lanes(jnp.concatenate([moe_w_grp[i], moe_w_exp[i]], axis=1).astype(F32), 128)
        b_router = _pad_lanes(jnp.concatenate([moe_b_grp[i], moe_b_exp[i]]).astype(F32)[None], 128)
        h = _moe(h, mods_l, w_router, b_router, moe_w_gate[i], moe_w_up[i], moe_w_down[i], ln_g[i, 1], ln_b[i, 1])
    return h[:, CTX:].astype(x.dtype)
```

```python
import functools
import math

import numpy as np
import jax
import jax.numpy as jnp
from jax import lax
from jax.experimental import pallas as pl
from jax.experimental.pallas import tpu as pltpu

F32 = jnp.float32
BF16 = jnp.bfloat16
HIGHEST = lax.Precision.HIGHEST

D_MODEL = 1024
DEPTH = 4
GRID_W = 64
CTX = 256
HALF = D_MODEL // 2
S5_H = 16
S5_G = HALF // S5_H
S5_P = 64
S5_L = 16
NA_HEADS = 8
NA_DH = 64
NA_WR = 8
NA_WC = 16
NA_QR = 4
NA_KR = 12
GLA_HEADS = 4
GLA_DK = 64
GLA_DV = 128
GLA_RANK = 16
GLA_TAU = 16.0
GLA_CHUNK = 64
DIFF_HEADS = 4
DIFF_DV = 128
DIFF_DH = 64
N_GROUPS = 4
EPG = 8
N_EXPERTS = N_GROUPS * EPG
D_EXPERT = D_MODEL // 4
MOE_EB = 4
ROPE_BASE = 10000.0
EPS = 1e-5
ALPHA = (2.0 * DEPTH) ** 0.25
EVEN_IN = 4 * HALF
ODD_IN_PAD = 3200
NEG = -1e30
LOG2E = math.log2(math.e)
ROW_TILE = 256
VMEM_LIMIT = 56 * 1024 * 1024


def _cparams(sem):
    return pltpu.CompilerParams(dimension_semantics=sem, vmem_limit_bytes=VMEM_LIMIT)


def _bdot(a, b):
    return jnp.dot(a.astype(BF16), b.astype(BF16), preferred_element_type=F32)


def _bdot_nt(a, b):
    return lax.dot_general(a.astype(BF16), b.astype(BF16), (((1,), (1,)), ((), ())),
                           preferred_element_type=F32)


def _hdot(a, b):
    return jnp.dot(a, b, precision=HIGHEST, preferred_element_type=F32)


def _layer_norm(x, g, b):
    mu = jnp.mean(x, axis=-1, keepdims=True)
    xc = x - mu
    var = jnp.mean(xc * xc, axis=-1, keepdims=True)
    return xc * lax.rsqrt(var + EPS) * g + b


def _silu(x):
    return x * jax.nn.sigmoid(x)


def _mods_kernel(cc_ref, w_ref, b_ref, o_ref):
    o_ref[0] = _hdot(_silu(cc_ref[...]), w_ref[0]) + b_ref[0]


def _mods(cc, mod_w, mod_b):
    nblk = 6
    return pl.pallas_call(
        _mods_kernel,
        out_shape=jax.ShapeDtypeStruct((DEPTH, 8, 6 * D_MODEL), F32),
        grid=(DEPTH, nblk),
        in_specs=[pl.BlockSpec((8, D_MODEL), lambda l, j: (0, 0)),
                  pl.BlockSpec((1, D_MODEL, D_MODEL), lambda l, j: (l, 0, j)),
                  pl.BlockSpec((1, 1, D_MODEL), lambda l, j: (l, 0, j))],
        out_specs=pl.BlockSpec((1, 8, D_MODEL), lambda l, j: (l, 0, j)),
        compiler_params=_cparams(("arbitrary", "arbitrary")),
        name="mods",
    )(cc, mod_w, mod_b.reshape(DEPTH, 1, 6 * D_MODEL))


def _mod_row(nbatch):
    return lambda b, j: jnp.where(j == 0, nbatch, b)


def _inproj_kernel(h_ref, sh_ref, sc_ref, w_ref, o_ref, *tail_ref):
    a = h_ref[0] * (1.0 + sc_ref[0]) + sh_ref[0]
    y = _bdot(a, w_ref[...])
    o_ref[0] = y.astype(BF16)
    if tail_ref:
        tail_ref[0][0] = y[:, y.shape[1] - 128:]


def _inproj(h, mods_l, w, f32_tail=False):
    nb, s, _ = h.shape
    n = w.shape[1]
    row = _mod_row(nb)
    tile = lambda b, j: (b, j, 0)
    out_shape = [jax.ShapeDtypeStruct((nb, s, n), BF16)]
    out_specs = [pl.BlockSpec((1, ROW_TILE, n), tile)]
    if f32_tail:
        out_shape.append(jax.ShapeDtypeStruct((nb, s, 128), F32))
        out_specs.append(pl.BlockSpec((1, ROW_TILE, 128), tile))
    out = pl.pallas_call(
        _inproj_kernel,
        out_shape=out_shape,
        grid=(nb, s // ROW_TILE),
        in_specs=[pl.BlockSpec((1, ROW_TILE, D_MODEL), tile),
                  pl.BlockSpec((1, 1, D_MODEL), lambda b, j: (row(b, j), 0, 0)),
                  pl.BlockSpec((1, 1, D_MODEL), lambda b, j: (row(b, j), 0, 1)),
                  pl.BlockSpec((D_MODEL, n), lambda b, j: (0, 0))],
        out_specs=out_specs,
        compiler_params=_cparams(("parallel", "parallel")),
        name="inproj",
    )(h, mods_l, mods_l, w)
    return out if f32_tail else out[0]


def _s5_tables(lam_re, lam_im, log_dt, b_re, b_im, c_re, c_im, d_skip):
    L, H, P, G = S5_L, S5_H, S5_P, S5_G
    lam_re, lam_im = lam_re.astype(F32), lam_im.astype(F32)
    dt = jnp.exp(log_dt.astype(F32))[..., None]
    mag = jnp.exp(lam_re * dt)
    a_re = mag * jnp.cos(lam_im * dt)
    a_im = mag * jnp.sin(lam_im * dt)
    den = lam_re * lam_re + lam_im * lam_im
    f_re = ((a_re - 1.0) * lam_re + a_im * lam_im) / den
    f_im = (a_im * lam_re - (a_re - 1.0) * lam_im) / den
    b_re, b_im = b_re.astype(F32), b_im.astype(F32)
    bb_re = f_re[..., None] * b_re - f_im[..., None] * b_im
    bb_im = f_re[..., None] * b_im + f_im[..., None] * b_re
    tau = jnp.arange(L + 1, dtype=F32)[:, None, None, None]
    mag_t = jnp.exp(lam_re[None] * dt[None] * tau)
    ang_t = lam_im[None] * dt[None] * tau
    p_re = mag_t * jnp.cos(ang_t)
    p_im = mag_t * jnp.sin(ang_t)
    c_re, c_im = c_re.astype(F32), c_im.astype(F32)
    e_re = c_re[None] * p_re[:, :, :, None, :] - c_im[None] * p_im[:, :, :, None, :]
    e_im = c_re[None] * p_im[:, :, :, None, :] + c_im[None] * p_re[:, :, :, None, :]
    k = (jnp.einsum('tdgjp,dgpi->tdgji', e_re, bb_re, precision=HIGHEST)
         - jnp.einsum('tdgjp,dgpi->tdgji', e_im, bb_im, precision=HIGHEST))
    s_i, t_i = np.meshgrid(np.arange(L), np.arange(L), indexing='ij')
    kf = k[np.clip(t_i - s_i, 0, L), 0] * jnp.asarray(t_i >= s_i, F32)[..., None, None, None]
    kb = k[np.clip(s_i - t_i, 0, L), 1] * jnp.asarray(s_i >= t_i, F32)[..., None, None, None]
    skip = (jnp.asarray(s_i == t_i, F32)[..., None, None, None]
            * (jnp.eye(H, dtype=F32)[None] * d_skip.astype(F32).reshape(G, 1, H))[None, None])
    intra = (kf + kb + skip).transpose(2, 0, 4, 1, 3).reshape(G, L * H, L * H)

    def state_in(d, pw):
        pr, pi = p_re[pw, d], p_im[pw, d]
        re = pr[..., None] * bb_re[d][None] - pi[..., None] * bb_im[d][None]
        im = pr[..., None] * bb_im[d][None] + pi[..., None] * bb_re[d][None]
        return jnp.concatenate([re, im], axis=2).transpose(1, 0, 3, 2).reshape(G, L * H, 2 * P)

    sf = state_in(0, np.arange(L)[::-1].copy())
    sb = state_in(1, np.arange(L))
    w1 = jnp.concatenate([intra, sf, sb], axis=-1)

    def state_out(d, pw):
        er, ei = e_re[pw, d], e_im[pw, d]
        return jnp.concatenate([er, -ei], axis=-1).transpose(1, 3, 0, 2).reshape(G, 2 * P, L * H)

    w2 = jnp.concatenate([state_out(0, np.arange(1, L + 1)),
                          state_out(1, np.arange(L, 0, -1))], axis=1)
    ar, ai = p_re[L], p_im[L]
    rows = []
    for d in range(2):
        rows += [jnp.concatenate([ar[d], ar[d]], -1), jnp.concatenate([-ai[d], ai[d]], -1),
                 jnp.concatenate([ai[d], -ai[d]], -1)]
    rows += [jnp.zeros_like(rows[0])] * 2
    al = jnp.stack(rows, axis=1)
    return w1.astype(BF16), w2.astype(BF16), al


def _s5_kernel(x_ref, w1_ref, w2_ref, al_ref, y_ref, xw_ref, sw_ref, hh_ref, *, nch, nctx):
    lh = S5_L * S5_H
    p2 = 2 * S5_P
    xw_ref[...] = jnp.dot(x_ref[0], w1_ref[0], preferred_element_type=F32)
    sw_ref[:, 0:p2] = pltpu.roll(xw_ref[:, lh:lh + p2], S5_P, 1)
    sw_ref[:, p2:2 * p2] = pltpu.roll(xw_ref[:, lh + p2:lh + 2 * p2], S5_P, 1)
    al = al_ref[0]
    a1f, a2f, a3f, a1b, a2b, a3b = [al[i:i + 1] for i in range(6)]

    def body(k, carry):
        vf, wf, vb, wb = carry
        kb = jnp.where(k < nctx, nctx - 1 - k, nch + nctx - 1 - k)
        rf = pl.ds(pl.multiple_of(k * 8, 8), 8)
        rb = pl.ds(pl.multiple_of(kb * 8, 8), 8)
        hh_ref[rf, 0:p2] = vf
        hh_ref[rb, p2:2 * p2] = vb
        s_f = xw_ref[rf, lh:lh + p2]
        s_fs = sw_ref[rf, 0:p2]
        s_b = xw_ref[rb, lh + p2:lh + 2 * p2]
        s_bs = sw_ref[rb, p2:2 * p2]
        return (vf * a1f + wf * a2f + s_f, wf * a1f + vf * a3f + s_fs,
                vb * a1b + wb * a2b + s_b, wb * a1b + vb * a3b + s_bs)

    zero = jnp.zeros((8, p2), F32)
    lax.fori_loop(0, nch, body, (zero,) * 4)
    y_ref[0] = (xw_ref[:, 0:lh] + _bdot(hh_ref[...], w2_ref[0])).astype(BF16)


def _s5(u, tables):
    nb, s, _ = u.shape
    nch = s // S5_L
    lh = S5_L * S5_H
    r = nch * 8
    x = u.astype(BF16).reshape(nb, nch, S5_L, S5_G, S5_H).transpose(3, 1, 0, 2, 4)
    x = jnp.pad(x, ((0, 0), (0, 0), (0, 8 - nb), (0, 0), (0, 0))).reshape(S5_G, r, lh)
    w1, w2, al = tables
    y = pl.pallas_call(
        functools.partial(_s5_kernel, nch=nch, nctx=CTX // S5_L),
        out_shape=jax.ShapeDtypeStruct((S5_G, r, lh), BF16),
        grid=(S5_G,),
        in_specs=[pl.BlockSpec((1, r, lh), lambda g: (g, 0, 0)),
                  pl.BlockSpec((1, lh, 2 * lh), lambda g: (g, 0, 0)),
                  pl.BlockSpec((1, lh, lh), lambda g: (g, 0, 0)),
                  pl.BlockSpec((1, 8, 2 * S5_P), lambda g: (g, 0, 0))],
        out_specs=pl.BlockSpec((1, r, lh), lambda g: (g, 0, 0)),
        scratch_shapes=[pltpu.VMEM((r, 2 * lh), F32), pltpu.VMEM((r, lh), F32), pltpu.VMEM((r, lh), F32)],
        compiler_params=_cparams(("parallel",)),
        name="s5",
    )(x, w1, w2, al)
    y = y.reshape(S5_G, nch, 8, S5_L, S5_H)[:, :, :nb]
    return y.transpose(2, 1, 3, 0, 4).reshape(nb, s, HALF)


def _na_bias_tables(rpb, rows):
    w = GRID_W
    nh, ndr, ndc = rpb.shape
    v = jnp.pad(rpb.astype(F32), ((0, 0), (0, 0), (w - NA_WC, 2 * w - ndc - (w - NA_WC))))
    skew = jnp.tile(v, (1, 1, w))[:, :, :w * (2 * w - 1)].reshape(nh, ndr, w, 2 * w - 1)
    tc = skew[..., w - 1:]
    valid, dr = [], []
    for r0 in (0, NA_QR, rows - NA_QR):
        k0 = int(np.clip(r0 - NA_WR // 2, 0, rows - NA_KR))
        rq = r0 + np.arange(NA_QR)[:, None, None, None]
        wq = np.arange(w)[None, :, None, None]
        kr = k0 + np.arange(NA_KR)[None, None, :, None]
        kc = np.arange(w)[None, None, None, :]
        rs = np.clip(rq - NA_WR // 2, 0, rows - NA_WR)
        cs = np.clip(wq - NA_WC // 2, 0, w - NA_WC)
        valid.append((kr >= rs) & (kr < rs + NA_WR) & (kc >= cs) & (kc < cs + NA_WC))
        dr.append(np.clip(kr - rq + NA_WR - 1, 0, ndr - 1)[:, 0, :, 0])
    dr = np.stack(dr).reshape(-1)
    bias = jnp.take(tc, jnp.asarray(dr), axis=1)
    bias = bias.reshape(nh, 3, NA_QR, NA_KR, w, w).transpose(0, 1, 2, 4, 3, 5)
    bias = jnp.where(jnp.asarray(np.stack(valid))[None], bias, NEG)
    return bias.reshape(nh, 3, NA_QR * w, NA_KR * w)


def _natten_kernel(q_ref, k_ref, v_ref, bias_ref, o_ref, qb_ref, kb_ref, vb_ref, *, rows):
    nq = NA_QR * GRID_W
    nk = NA_KR * GRID_W
    nblk = rows // NA_QR
    scale = NA_DH ** -0.5
    qb_ref[...] = (q_ref[0] * scale).astype(BF16)
    kb_ref[...] = k_ref[0].astype(BF16)
    vb_ref[...] = v_ref[0].astype(BF16)
    lane = lax.broadcasted_iota(jnp.int32, (1, 2 * NA_DH), 1)
    head_mask = [lane < NA_DH, lane >= NA_DH]
    kc = kb_ref[0:CTX, :]
    vc = vb_ref[0:CTX, :]

    def attend(q, extra):
        outs = []
        for h in range(2):
            qm = jnp.where(head_mask[h], q, jnp.zeros_like(q))
            s_c = _bdot_nt(qm, kc)
            m = jnp.max(s_c, axis=-1, keepdims=True)
            if extra is not None:
                k_loc, v_loc, bias = extra
                s_l = _bdot_nt(qm, k_loc) + bias[h]
                m = jnp.maximum(m, jnp.max(s_l, axis=-1, keepdims=True))
                p_l = jnp.exp(s_l - m)
            p_c = jnp.exp(s_c - m)
            den = jnp.sum(p_c, axis=-1, keepdims=True)
            acc = _bdot(p_c, vc)
            if extra is not None:
                den = den + jnp.sum(p_l, axis=-1, keepdims=True)
                acc = acc + _bdot(p_l, v_loc)
            outs.append(acc / den)
        return jnp.where(head_mask[0], outs[0], outs[1])

    o_ref[0, 0:CTX, :] = attend(qb_ref[0:CTX, :], None)

    def body(i, carry):
        r0 = i * NA_QR
        k0 = jnp.clip(r0 - NA_WR // 2, 0, rows - NA_KR)
        pat = jnp.where(i == 0, 0, jnp.where(i == nblk - 1, 2, 1))
        qs = pl.multiple_of(CTX + r0 * GRID_W, GRID_W)
        ks = pl.multiple_of(CTX + k0 * GRID_W, GRID_W)
        q = qb_ref[pl.ds(qs, nq), :]
        k_loc = kb_ref[pl.ds(ks, nk), :]
        v_loc = vb_ref[pl.ds(ks, nk), :]
        bias = [bias_ref[h, pat] for h in range(2)]
        o_ref[0, pl.ds(qs, nq), :] = attend(q, (k_loc, v_loc, bias))
        return carry

    lax.fori_loop(0, nblk, body, 0)


def _natten(a, bias):
    nb, s, _ = a.shape
    rows = (s - CTX) // GRID_W
    pw = 2 * NA_DH
    nq, nk = NA_QR * GRID_W, NA_KR * GRID_W
    hb = HALF // pw
    return pl.pallas_call(
        functools.partial(_natten_kernel, rows=rows),
        out_shape=jax.ShapeDtypeStruct((nb, s, HALF), F32),
        grid=(nb, hb),
        in_specs=[pl.BlockSpec((1, s, pw), lambda b, h: (b, 0, hb + h)),
                  pl.BlockSpec((1, s, pw), lambda b, h: (b, 0, 2 * hb + h)),
                  pl.BlockSpec((1, s, pw), lambda b, h: (b, 0, 3 * hb + h)),
                  pl.BlockSpec((2, 3, nq, nk), lambda b, h: (h, 0, 0, 0))],
        out_specs=pl.BlockSpec((1, s, pw), lambda b, h: (b, 0, h)),
        scratch_shapes=[pltpu.VMEM((s, pw), BF16)] * 3,
        compiler_params=_cparams(("parallel", "parallel")),
        name="natten",
    )(a, a, a, bias)


def _gelu_tanh(x):
    return 0.5 * x * (1.0 + jnp.tanh(math.sqrt(2.0 / math.pi) * (x + 0.044715 * (x * x * x))))


def _out_kernel(ya_ref, yb_ref, h_ref, g_ref, e0_ref, e1_ref, wo_ref, lng_ref, lnb_ref, o_ref, *, glu):
    if glu:
        z = _bdot(_gelu_tanh(ya_ref[0].astype(F32)), e0_ref[...]) + e1_ref[...]
        ya = z[:, :HALF] * jax.nn.sigmoid(z[:, HALF:])
    else:
        parts = []
        for hd in range(GLA_HEADS):
            cols = slice(hd * GLA_DV, (hd + 1) * GLA_DV)
            o = ya_ref[0, :, cols]
            o = o * lax.rsqrt(jnp.mean(o * o, axis=-1, keepdims=True) + EPS) * e1_ref[:, cols]
            parts.append(o * _silu(e0_ref[0, :, cols].astype(F32)))
        ya = jnp.concatenate(parts, axis=1)
    y = _bdot(ya, wo_ref[0:HALF, :]) + _bdot(yb_ref[0], wo_ref[HALF:, :])
    o_ref[0] = _layer_norm(ALPHA * h_ref[0] + g_ref[0] * y, lng_ref[...], lnb_ref[...])


def _outproj(ya, yb, h, mods_l, w_out, ln_g, ln_b, *, glu_w=None, glu_b=None, proj=None, norm_g=None):
    nb, s, _ = h.shape
    row = _mod_row(nb)
    glu = glu_w is not None
    tile = lambda b, j: (b, j, 0)
    const = lambda b, j: (0, 0)
    if glu:
        extra_specs = [pl.BlockSpec((HALF, D_MODEL), const), pl.BlockSpec((1, D_MODEL), const)]
        extra = [glu_w.astype(BF16), glu_b.astype(F32).reshape(1, D_MODEL)]
    else:
        rblk = (2 * GLA_HEADS * GLA_DK + HALF) // HALF
        extra_specs = [pl.BlockSpec((1, ROW_TILE, HALF), lambda b, j: (b, j, rblk)),
                       pl.BlockSpec((1, HALF), const)]
        extra = [proj, norm_g.astype(F32).reshape(1, HALF)]
    in_specs = [pl.BlockSpec((1, ROW_TILE, HALF), tile),
                pl.BlockSpec((1, ROW_TILE, HALF), tile),
                pl.BlockSpec((1, ROW_TILE, D_MODEL), tile),
                pl.BlockSpec((1, 1, D_MODEL), lambda b, j: (row(b, j), 0, 2))] + extra_specs + [
                pl.BlockSpec((D_MODEL, D_MODEL), const), pl.BlockSpec((1, D_MODEL), const),
                pl.BlockSpec((1, D_MODEL), const)]
    return pl.pallas_call(
        functools.partial(_out_kernel, glu=glu),
        out_shape=jax.ShapeDtypeStruct((nb, s, D_MODEL), F32),
        grid=(nb, s // ROW_TILE),
        in_specs=in_specs,
        out_specs=pl.BlockSpec((1, ROW_TILE, D_MODEL), tile),
        compiler_params=_cparams(("parallel", "parallel")),
        name="outproj_glu" if glu else "outproj_gate",
    )(ya, yb, h, mods_l, *extra, w_out.astype(BF16), ln_g.astype(F32).reshape(1, D_MODEL),
      ln_b.astype(F32).reshape(1, D_MODEL))


def _log_sigmoid(x):
    return jnp.minimum(x, 0.0) - jnp.log1p(jnp.exp(-jnp.abs(x)))


def _gla_kernel(q_ref, k_ref, v_ref, z_ref, wg_ref, bg_ref, o_ref, gf_ref, gb_ref, ob_ref, st_ref, *, nch):
    c = GLA_CHUNK
    dk2 = 2 * GLA_DK
    nctx = CTX // c
    z = z_ref[0]
    gf_ref[...] = _log_sigmoid(_hdot(z, wg_ref[0, 0]) + bg_ref[0, 0]) * (1.0 / GLA_TAU)
    gb_ref[...] = _log_sigmoid(_hdot(z, wg_ref[1, 0]) + bg_ref[1, 0]) * (1.0 / GLA_TAU)
    ri = lax.broadcasted_iota(jnp.int32, (c, c), 0)
    ci = lax.broadcasted_iota(jnp.int32, (c, c), 1)
    lane = lax.broadcasted_iota(jnp.int32, (1, dk2), 1)
    head_mask = [lane < GLA_DK, lane >= GLA_DK]
    scale = GLA_DK ** -0.5

    st_ref[...] = jnp.zeros_like(st_ref)

    def chunk(direction, n):
        g_ref = gf_ref if direction == 0 else gb_ref
        keep = (ci <= ri) if direction == 0 else (ci >= ri)
        tri = jnp.where(keep, 1.0, 0.0).astype(BF16)
        rs = pl.ds(pl.multiple_of(n * c, c), c)
        g = g_ref[rs, :]
        g_hi = g.astype(BF16)
        g_lo = (g - g_hi.astype(F32)).astype(BF16)
        gc = (jnp.dot(tri, g_hi, preferred_element_type=F32)
              + jnp.dot(tri, g_lo, preferred_element_type=F32))
        gend = gc[c - 1:c, :] if direction == 0 else gc[0:1, :]
        q = q_ref[0, rs, :] * scale
        k = k_ref[0, rs, :]
        qd = q * jnp.exp(gc)
        kd = k * jnp.exp(-gc)
        kdec = k * jnp.exp(gend - gc)
        dcol = jnp.exp(jnp.broadcast_to(gend, (dk2, dk2)).T)
        dcol = jnp.concatenate([dcol, dcol], axis=1)
        qs = jnp.concatenate([jnp.where(head_mask[0], qd, 0.0), jnp.where(head_mask[1], qd, 0.0)],
                             axis=0).astype(BF16)
        att = jnp.where(jnp.concatenate([keep, keep], axis=0), _bdot_nt(qs, kd), 0.0)
        v = v_ref[0, rs, :].astype(BF16)
        st = st_ref[direction]
        o2 = _bdot(att, v) + _bdot(qs, st)
        st_ref[direction] = dcol * st + jnp.dot(kdec.T.astype(BF16), v, preferred_element_type=F32)
        o = jnp.concatenate([o2[0:c, 0:GLA_DV], o2[c:2 * c, GLA_DV:2 * GLA_DV]], axis=1)
        if direction == 0:
            o_ref[0, rs, :] = o
        else:
            ob_ref[rs, :] = o

    def body(i, carry):
        chunk(0, i)
        chunk(1, jnp.where(i < nctx, nctx - 1 - i, nch + nctx - 1 - i))
        return carry

    lax.fori_loop(0, nch, body, 0, unroll=2)
    o_ref[0] += ob_ref[...]


def _gla(a, z, wg, bg):
    nb, s, _ = a.shape
    nch = s // GLA_CHUNK
    hp = GLA_HEADS // 2
    dk2, dv2 = 2 * GLA_DK, 2 * GLA_DV
    return pl.pallas_call(
        functools.partial(_gla_kernel, nch=nch),
        out_shape=jax.ShapeDtypeStruct((nb, s, HALF), F32),
        grid=(nb, hp),
        in_specs=[pl.BlockSpec((1, s, dk2), lambda b, h: (b, 0, h)),
                  pl.BlockSpec((1, s, dk2), lambda b, h: (b, 0, hp + h)),
                  pl.BlockSpec((1, s, dv2), lambda b, h: (b, 0, hp + h)),
                  pl.BlockSpec((1, s, 128), lambda b, h: (b, 0, 0)),
                  pl.BlockSpec((2, 1, 128, dk2), lambda b, h: (0, h, 0, 0)),
                  pl.BlockSpec((2, 1, 1, dk2), lambda b, h: (0, h, 0, 0))],
        out_specs=pl.BlockSpec((1, s, dv2), lambda b, h: (b, 0, h)),
        scratch_shapes=[pltpu.VMEM((s, dk2), F32), pltpu.VMEM((s, dk2), F32), pltpu.VMEM((s, dv2), F32),
                        pltpu.VMEM((2, dk2, dv2), F32)],
        compiler_params=_cparams(("parallel", "parallel")),
        name="gla",
    )(a, a, a, z, wg, bg)


def _rope_tables(s):
    t = np.arange(s - CTX)
    n = DIFF_DH // 4
    inv = ROPE_BASE ** (-np.arange(n, dtype=np.float64) / n)
    ang_r = (t // GRID_W)[:, None] * inv[None]
    ang_c = (t % GRID_W)[:, None] * inv[None]
    ang = np.concatenate([ang_r, ang_r, ang_c, ang_c], axis=1)
    ang = np.concatenate([ang, ang], axis=1)
    cos = np.concatenate([np.ones((CTX, 128)), np.cos(ang)], axis=0)
    sin = np.concatenate([np.zeros((CTX, 128)), np.sin(ang)], axis=0)
    first = ((np.arange(128) // n) % 2 == 0)[None]
    sin_up = np.where(first, -sin, 0.0)
    sin_dn = np.where(first, 0.0, sin)
    f = lambda z: jnp.asarray(z, F32)
    return f(cos), f(sin_up), f(sin_dn)


def _rope_kernel(q_ref, k_ref, cos_ref, su_ref, sd_ref, qo_ref, ko_ref):
    n = DIFF_DH // 4
    cos = jnp.concatenate([cos_ref[...]] * DIFF_HEADS, axis=1)
    su = jnp.concatenate([su_ref[...]] * DIFF_HEADS, axis=1)
    sd = jnp.concatenate([sd_ref[...]] * DIFF_HEADS, axis=1)

    def rot(x):
        w = x.shape[1]
        return x * cos + pltpu.roll(x, w - n, 1) * su + pltpu.roll(x, n, 1) * sd

    qo_ref[0] = rot(q_ref[0].astype(F32) * (DIFF_DH ** -0.5 * LOG2E)).astype(BF16)
    ko_ref[0] = rot(k_ref[0].astype(F32)).astype(BF16)


def _rope(a, tables):
    nb, s, _ = a.shape
    qblk = (2 * GLA_HEADS * GLA_DK + 2 * HALF) // HALF
    tile = lambda b, j: (b, j, 0)
    tab = pl.BlockSpec((ROW_TILE, 128), lambda b, j: (j, 0))
    return pl.pallas_call(
        _rope_kernel,
        out_shape=[jax.ShapeDtypeStruct((nb, s, HALF), BF16)] * 2,
        grid=(nb, s // ROW_TILE),
        in_specs=[pl.BlockSpec((1, ROW_TILE, HALF), lambda b, j: (b, j, qblk)),
                  pl.BlockSpec((1, ROW_TILE, HALF), lambda b, j: (b, j, qblk + 1)),
                  tab, tab, tab],
        out_specs=[pl.BlockSpec((1, ROW_TILE, HALF), tile)] * 2,
        compiler_params=_cparams(("parallel", "parallel")),
        name="rope",
    )(a, a, *tables)


def _diff_kernel(q_ref, k_ref, v_ref, lam_ref, ng_ref, o_ref, vb_ref, *, lam_init):
    j = pl.program_id(2)
    lane = lax.broadcasted_iota(jnp.int32, (1, 2 * DIFF_DH), 1)
    map_mask = [lane < DIFF_DH, lane >= DIFF_DH]
    lv = lam_ref[...]
    lam = (jnp.exp(jnp.sum(lv[0:1] * lv[1:2], axis=-1, keepdims=True))
           - jnp.exp(jnp.sum(lv[2:3] * lv[3:4], axis=-1, keepdims=True)) + lam_init)

    @pl.when(j == 0)
    def _():
        vb_ref[...] = v_ref[0].astype(BF16)

    def attend(nkeys):
        q = q_ref[0]
        k = k_ref[0, 0:nkeys, :]
        w = None
        for m in range(2):
            s = _bdot_nt(jnp.where(map_mask[m], q, jnp.zeros_like(q)), k)
            p = jnp.exp2(s - jnp.max(s, axis=-1, keepdims=True))
            inv = 1.0 / jnp.sum(p, axis=-1, keepdims=True)
            w = p * inv if m == 0 else w - p * (lam * inv)
        o = _bdot(w, vb_ref[0:nkeys, :])
        o = o * lax.rsqrt(jnp.mean(o * o, axis=-1, keepdims=True) + EPS) * ng_ref[...]
        o_ref[0] = o * (1.0 - lam_init)

    @pl.when(j == 0)
    def _():
        attend(CTX)

    @pl.when(j > 0)
    def _():
        attend(k_ref.shape[1])


def _diff(qr, kr, a, lam_vecs, ng, lam_init):
    nb, s, _ = qr.shape
    dw = 2 * DIFF_DH
    vblk = (2 * GLA_HEADS * GLA_DK + 4 * HALF) // dw
    return pl.pallas_call(
        functools.partial(_diff_kernel, lam_init=lam_init),
        out_shape=jax.ShapeDtypeStruct((nb, s, HALF), F32),
        grid=(nb, DIFF_HEADS, s // ROW_TILE),
        in_specs=[pl.BlockSpec((1, ROW_TILE, dw), lambda b, h, j: (b, j, h)),
                  pl.BlockSpec((1, s, dw), lambda b, h, j: (b, 0, h)),
                  pl.BlockSpec((1, s, DIFF_DV), lambda b, h, j: (b, 0, vblk + h)),
                  pl.BlockSpec((8, DIFF_DH), lambda b, h, j: (0, 0)),
                  pl.BlockSpec((1, DIFF_DV), lambda b, h, j: (0, 0))],
        out_specs=pl.BlockSpec((1, ROW_TILE, DIFF_DV), lambda b, h, j: (b, j, h)),
        scratch_shapes=[pltpu.VMEM((s, DIFF_DV), BF16)],
        compiler_params=_cparams(("parallel", "parallel", "arbitrary")),
        name="diff_attn",
    )(qr, kr, a, lam_vecs, ng)


def _moe_kernel(h_ref, m0_ref, m1_ref, mc_ref, wr_ref, br_ref, wg_ref, wu_ref, wd_ref, lng_ref, lnb_ref, o_ref,
                as_ref, cs_ref, ps_ref, ys_ref, win_ref, *, seq, win):
    t = pl.program_id(0)
    e = pl.program_id(1)
    tm = h_ref.shape[0]
    lane = lax.broadcasted_iota(jnp.int32, (1, 128), 1).astype(F32)
    d = D_MODEL

    def mod(col):
        b0 = (t * tm) // seq
        grow = t * tm + lax.broadcasted_iota(jnp.int32, (tm, 1), 0)
        in_next = grow >= (b0 + 1) * seq
        local = grow - jnp.where(in_next, b0 + 1, b0) * seq
        cols = slice(col * d, (col + 1) * d)
        return jnp.where(local < CTX, mc_ref[0, :, cols],
                         jnp.where(in_next, m1_ref[0, :, cols], m0_ref[0, :, cols]))

    @pl.when(e == 0)
    def _():
        a = h_ref[...] * (1.0 + mod(1)) + mod(0)
        logits = _hdot(a, wr_ref[...]) + br_ref[...]
        is_g = lane < N_GROUPS
        lg = jnp.where(is_g, logits, NEG)
        mg = jnp.max(lg, axis=-1, keepdims=True)
        g_sel = jnp.min(jnp.where(lg == mg, lane, 128.0), axis=-1, keepdims=True)
        p_grp = 1.0 / jnp.sum(jnp.where(is_g, jnp.exp(lg - mg), 0.0), axis=-1, keepdims=True)
        lo = N_GROUPS + g_sel * EPG
        in_grp = (lane >= lo) & (lane < lo + EPG)
        le = jnp.where(in_grp, logits, NEG)
        v1 = jnp.max(le, axis=-1, keepdims=True)
        i1 = jnp.min(jnp.where(le == v1, lane, 128.0), axis=-1, keepdims=True)
        le2 = jnp.where(lane == i1, NEG, le)
        v2 = jnp.max(le2, axis=-1, keepdims=True)
        i2 = jnp.min(jnp.where(le2 == v2, lane, 128.0), axis=-1, keepdims=True)
        e2 = jnp.exp(v2 - v1)
        w1 = p_grp / (1.0 + e2)
        w2 = p_grp * e2 / (1.0 + e2)
        comb = jnp.where(lane == i1, w1, jnp.where(lane == i2, w2, 0.0))

        onehot = jnp.where(lane == g_sel, 1.0, 0.0).astype(BF16)
        ri = lax.broadcasted_iota(jnp.int32, (tm, tm), 0)
        ci = lax.broadcasted_iota(jnp.int32, (tm, tm), 1)
        ranks = jnp.dot(jnp.where(ci <= ri, 1.0, 0.0).astype(BF16), onehot, preferred_element_type=F32)
        cnt = ranks[tm - 1:tm, :]
        n = [jnp.sum(jnp.where(lane == float(g), cnt, 0.0), axis=-1, keepdims=True) for g in range(N_GROUPS)]
        off = [jnp.zeros_like(n[0]), n[0], n[0] + n[1], n[0] + n[1] + n[2]]
        off_vec = sum(jnp.where(lane == float(g), off[g], 0.0) for g in range(1, N_GROUPS))
        dest = jnp.sum(jnp.where(lane == g_sel, ranks + off_vec, 0.0), axis=-1, keepdims=True) - 1.0

        pos = lax.broadcasted_iota(jnp.int32, (1, tm), 1).astype(F32)
        ps_ref[...] = jnp.where(dest == pos, 1.0, 0.0).astype(BF16)
        hi = jnp.floor(dest * (1.0 / 32.0))
        digits = jnp.where(lane == 0.0, hi, jnp.where(lane == 1.0, dest - 32.0 * hi, 0.0)).astype(BF16)
        ri8 = lax.broadcasted_iota(jnp.int32, (8, 128), 0)
        ci8 = lax.broadcasted_iota(jnp.int32, (8, 128), 1)
        dig_t = _bdot_nt(jnp.where(ri8 == ci8, 1.0, 0.0), digits)
        dest_row = 32.0 * dig_t[0:1, :] + dig_t[1:2, :]
        rows = lax.broadcasted_iota(jnp.int32, (tm, 1), 0).astype(F32)
        pg = jnp.where(dest_row == rows, 1.0, 0.0).astype(BF16)
        as_ref[...] = jnp.dot(pg, a.astype(BF16), preferred_element_type=F32).astype(BF16)
        c_hi = comb.astype(BF16)
        c_lo = (comb - c_hi.astype(F32)).astype(BF16)
        packed = jnp.where(lane < 64.0, c_hi.astype(F32), pltpu.roll(c_lo.astype(F32), 64, 1)).astype(BF16)
        cg = jnp.dot(pg, packed, preferred_element_type=F32)
        cs_ref[...] = cg + pltpu.roll(cg, 64, 1)
        ys_ref[...] = jnp.zeros_like(ys_ref)
        for g in range(N_GROUPS):
            start = jnp.minimum((jnp.sum(off[g]).astype(jnp.int32) // 16) * 16, tm - win)
            end = jnp.sum(off[g] + n[g]).astype(jnp.int32)
            win_ref[g] = start
            win_ref[N_GROUPS + g] = jnp.where(end > start, (end - start + win - 1) // win, 0)

    g = (e * MOE_EB) // EPG
    start = win_ref[g]

    def window(w, carry):
        lo = start + w * win
        rs = pl.ds(pl.multiple_of(jnp.minimum(lo, tm - win), 16), win)
        x = as_ref[rs, :]
        row = jnp.minimum(lo, tm - win) + lax.broadcasted_iota(jnp.int32, (win, 1), 0)
        cs = jnp.where(row >= lo, cs_ref[rs, :], 0.0)
        upd = None
        for j in range(MOE_EB):
            sel = lane == (e * MOE_EB + j + N_GROUPS).astype(F32)
            cw = jnp.sum(jnp.where(sel, cs, 0.0), axis=-1, keepdims=True)
            hg = jnp.dot(x, wg_ref[j], preferred_element_type=F32)
            hu = jnp.dot(x, wu_ref[j], preferred_element_type=F32)
            dn = jnp.dot((_silu(hg) * hu * cw).astype(BF16), wd_ref[j], preferred_element_type=F32)
            upd = dn if upd is None else upd + dn
        ys_ref[rs, :] += upd
        return carry

    lax.fori_loop(0, win_ref[N_GROUPS + g], window, 0)

    @pl.when(e == N_EXPERTS // MOE_EB - 1)
    def _():
        y = jnp.dot(ps_ref[...], ys_ref[...].astype(BF16), preferred_element_type=F32)
        o_ref[...] = _layer_norm(ALPHA * h_ref[...] + mod(2) * y, lng_ref[...], lnb_ref[...])


def _moe(h, mods_l, w_router, b_router, w_gate, w_up, w_down, ln_g, ln_b):
    nb, s, d = h.shape
    rows = nb * s
    tm = 1024 if rows % 1024 == 0 else 256
    win = tm // N_GROUPS + tm // 16
    nt = rows // tm
    const = lambda t, e: (0, 0)
    b0 = lambda t: (t * tm) // s
    neb = N_EXPERTS // MOE_EB
    w_gate, w_up, w_down = w_gate.astype(BF16), w_up.astype(BF16), w_down.astype(BF16)
    out = pl.pallas_call(
        functools.partial(_moe_kernel, seq=s, win=win),
        out_shape=jax.ShapeDtypeStruct((rows, d), F32),
        grid=(nt, neb),
        in_specs=[pl.BlockSpec((tm, d), lambda t, e: (t, 0)),
                  pl.BlockSpec((1, 1, 3 * d), lambda t, e: (b0(t), 0, 1)),
                  pl.BlockSpec((1, 1, 3 * d), lambda t, e: (jnp.minimum(b0(t) + 1, nb - 1), 0, 1)),
                  pl.BlockSpec((1, 1, 3 * d), lambda t, e: (nb, 0, 1)),
                  pl.BlockSpec((d, 128), const),
                  pl.BlockSpec((1, 128), const),
                  pl.BlockSpec((MOE_EB, d, D_EXPERT), lambda t, e: (e, 0, 0)),
                  pl.BlockSpec((MOE_EB, d, D_EXPERT), lambda t, e: (e, 0, 0)),
                  pl.BlockSpec((MOE_EB, D_EXPERT, d), lambda t, e: (e, 0, 0)),
                  pl.BlockSpec((1, d), const),
                  pl.BlockSpec((1, d), const)],
        out_specs=pl.BlockSpec((tm, d), lambda t, e: (t, 0)),
        scratch_shapes=[pltpu.VMEM((tm, d), BF16), pltpu.VMEM((tm, 128), F32), pltpu.VMEM((tm, tm), BF16),
                        pltpu.VMEM((tm, d), F32), pltpu.SMEM((2 * N_GROUPS,), jnp.int32)],
        compiler_params=_cparams(("parallel", "arbitrary")),
        name="moe",
    )(h.reshape(rows, d), mods_l, mods_l, mods_l, w_router, b_router, w_gate, w_up, w_down,
      ln_g.astype(F32).reshape(1, d), ln_b.astype(F32).reshape(1, d))
    return out.reshape(nb, s, d)


def _pad_lanes(x, n):
    return jnp.pad(x, [(0, 0)] * (x.ndim - 1) + [(0, n - x.shape[-1])])


def kernel(x, c, ctx, c_ctx, mod_w, mod_b, ln_g, ln_b, even_w_in, even_w_out, s5_lam_re, s5_lam_im, s5_log_dt, s5_b_re, s5_b_im, s5_c_re, s5_c_im, s5_d, s5_glu_w, s5_glu_b, na_rpb, odd_w_in, odd_w_out, gla_gate_w, gla_gate_b, gla_norm_g, diff_lq1, diff_lk1, diff_lq2, diff_lk2, diff_norm_g, moe_w_grp, moe_b_grp, moe_w_exp, moe_b_exp, moe_w_gate, moe_w_up, moe_w_down):
    nb, t, d = x.shape
    assert d == D_MODEL and ctx.shape[1] == CTX and nb < 8 and t % (NA_QR * GRID_W) == 0
    s = CTX + t
    rows = t // GRID_W
    h = jnp.concatenate([ctx, x], axis=1).astype(F32)
    cc = jnp.zeros((8, d), F32).at[:nb].set(c.astype(F32)).at[nb].set(c_ctx.astype(F32))
    mods = _mods(cc, mod_w.astype(F32), mod_b.astype(F32)).reshape(DEPTH, 8, 1, 6 * d)
    rope_tabs = _rope_tables(s)
    for i in range(DEPTH):
        j = i // 2
        mods_l = mods[i]
        if i % 2 == 0:
            a = _inproj(h, mods_l, even_w_in[j].astype(BF16))
            tables = _s5_tables(s5_lam_re[j], s5_lam_im[j], s5_log_dt[j], s5_b_re[j], s5_b_im[j],
                                s5_c_re[j], s5_c_im[j], s5_d[j])
            y_a = _s5(a[..., :HALF], tables)
            y_b = _natten(a, _na_bias_tables(na_rpb[j], rows))
            h = _outproj(y_a, y_b, h, mods_l, even_w_out[j], ln_g[i, 0], ln_b[i, 0],
                         glu_w=s5_glu_w[j], glu_b=s5_glu_b[j])
        else:
            lam_init = 0.8 - 0.6 * math.exp(-0.3 * i)
            w = odd_w_in[j]
            nqk = 2 * GLA_HEADS * GLA_DK + 2 * HALF
            w = jnp.concatenate([w[:, :nqk], w[:, nqk + 2 * GLA_RANK:],
                                 _pad_lanes(w[:, nqk:nqk + 2 * GLA_RANK], 128)], axis=1).astype(BF16)
            a, z = _inproj(h, mods_l, w, f32_tail=True)
            gw = gla_gate_w[j].astype(F32)
            wg = jnp.zeros((2, 128, GLA_HEADS * GLA_DK), F32)
            for dd in range(2):
                wg = wg.at[dd, dd * GLA_RANK:(dd + 1) * GLA_RANK].set(gw[dd])
            wg = wg.reshape(2, 128, 2, 2 * GLA_DK).transpose(0, 2, 1, 3)
            bg = gla_gate_b[j].astype(F32).reshape(2, 2, 1, 2 * GLA_DK)
            y_a = _gla(a, z, wg, bg)
            qr, kr = _rope(a, rope_tabs)
            lam_vecs = jnp.zeros((8, DIFF_DH), F32).at[0].set(diff_lq1[j]).at[1].set(diff_lk1[j]) \
                .at[2].set(diff_lq2[j]).at[3].set(diff_lk2[j])
            y_b = _diff(qr, kr, a, lam_vecs, diff_norm_g[j].astype(F32).reshape(1, DIFF_DV), lam_init)
            h = _outproj(y_a, y_b, h, mods_l, odd_w_out[j], ln_g[i, 0], ln_b[i, 0],
                         proj=a, norm_g=gla_norm_g[j])
        w_router = _pad_lanes(jnp.concatenate([moe_w_grp[i], moe_w_exp[i]], axis=1).astype(F32), 128)
        b_router = _pad_lanes(jnp.concatenate([moe_b_grp[i], moe_b_exp[i]]).astype(F32)[None], 128)
        h = _moe(h, mods_l, w_router, b_router, moe_w_gate[i], moe_w_up[i], moe_w_down[i], ln_g[i, 1], ln_b[i, 1])
    return h[:, CTX:].astype(x.dtype)
```

```python
import functools
import math

import numpy as np
import jax
import jax.numpy as jnp
from jax import lax
from jax.experimental import pallas as pl
from jax.experimental.pallas import tpu as pltpu

F32 = jnp.float32
BF16 = jnp.bfloat16
HIGHEST = lax.Precision.HIGHEST

D_MODEL = 1024
DEPTH = 4
GRID_W = 64
CTX = 256
HALF = D_MODEL // 2
S5_H = 16
S5_G = HALF // S5_H
S5_P = 64
S5_L = 16
NA_HEADS = 8
NA_DH = 64
NA_WR = 8
NA_WC = 16
NA_QR = 4
NA_KR = 12
GLA_HEADS = 4
GLA_DK = 64
GLA_DV = 128
GLA_RANK = 16
GLA_TAU = 16.0
GLA_CHUNK = 64
DIFF_HEADS = 4
DIFF_DV = 128
DIFF_DH = 64
N_GROUPS = 4
EPG = 8
N_EXPERTS = N_GROUPS * EPG
D_EXPERT = D_MODEL // 4
MOE_EB = 4
ROPE_BASE = 10000.0
EPS = 1e-5
ALPHA = (2.0 * DEPTH) ** 0.25
EVEN_IN = 4 * HALF
ODD_IN_PAD = 3200
NEG = -1e30
LOG2E = math.log2(math.e)
ROW_TILE = 256
VMEM_LIMIT = 56 * 1024 * 1024


def _cparams(sem):
    return pltpu.CompilerParams(dimension_semantics=sem, vmem_limit_bytes=VMEM_LIMIT)


def _bdot(a, b):
    return jnp.dot(a.astype(BF16), b.astype(BF16), preferred_element_type=F32)


def _bdot_nt(a, b):
    return lax.dot_general(a.astype(BF16), b.astype(BF16), (((1,), (1,)), ((), ())),
                           preferred_element_type=F32)


def _hdot(a, b):
    return jnp.dot(a, b, precision=HIGHEST, preferred_element_type=F32)


def _layer_norm(x, g, b):
    mu = jnp.mean(x, axis=-1, keepdims=True)
    xc = x - mu
    var = jnp.mean(xc * xc, axis=-1, keepdims=True)
    return xc * lax.rsqrt(var + EPS) * g + b


def _silu(x):
    return x * jax.nn.sigmoid(x)


def _mods_kernel(cc_ref, w_ref, b_ref, o_ref):
    o_ref[0] = _hdot(_silu(cc_ref[...]), w_ref[0]) + b_ref[0]


def _mods(cc, mod_w, mod_b):
    nblk = 6
    return pl.pallas_call(
        _mods_kernel,
        out_shape=jax.ShapeDtypeStruct((DEPTH, 8, 6 * D_MODEL), F32),
        grid=(DEPTH, nblk),
        in_specs=[pl.BlockSpec((8, D_MODEL), lambda l, j: (0, 0)),
                  pl.BlockSpec((1, D_MODEL, D_MODEL), lambda l, j: (l, 0, j)),
                  pl.BlockSpec((1, 1, D_MODEL), lambda l, j: (l, 0, j))],
        out_specs=pl.BlockSpec((1, 8, D_MODEL), lambda l, j: (l, 0, j)),
        compiler_params=_cparams(("arbitrary", "arbitrary")),
        name="mods",
    )(cc, mod_w, mod_b.reshape(DEPTH, 1, 6 * D_MODEL))


def _mod_row(nbatch):
    return lambda b, j: jnp.where(j == 0, nbatch, b)


def _inproj_kernel(h_ref, sh_ref, sc_ref, w_ref, o_ref, *tail_ref):
    a = h_ref[0] * (1.0 + sc_ref[0]) + sh_ref[0]
    y = _bdot(a, w_ref[...])
    o_ref[0] = y.astype(BF16)
    if tail_ref:
        tail_ref[0][0] = y[:, y.shape[1] - 128:]


def _inproj(h, mods_l, w, f32_tail=False):
    nb, s, _ = h.shape
    n = w.shape[1]
    row = _mod_row(nb)
    tile = lambda b, j: (b, j, 0)
    out_shape = [jax.ShapeDtypeStruct((nb, s, n), BF16)]
    out_specs = [pl.BlockSpec((1, ROW_TILE, n), tile)]
    if f32_tail:
        out_shape.append(jax.ShapeDtypeStruct((nb, s, 128), F32))
        out_specs.append(pl.BlockSpec((1, ROW_TILE, 128), tile))
    out = pl.pallas_call(
        _inproj_kernel,
        out_shape=out_shape,
        grid=(nb, s // ROW_TILE),
        in_specs=[pl.BlockSpec((1, ROW_TILE, D_MODEL), tile),
                  pl.BlockSpec((1, 1, D_MODEL), lambda b, j: (row(b, j), 0, 0)),
                  pl.BlockSpec((1, 1, D_MODEL), lambda b, j: (row(b, j), 0, 1)),
                  pl.BlockSpec((D_MODEL, n), lambda b, j: (0, 0))],
        out_specs=out_specs,
        compiler_params=_cparams(("parallel", "parallel")),
        name="inproj",
    )(h, mods_l, mods_l, w)
    return out if f32_tail else out[0]


def _s5_tables(lam_re, lam_im, log_dt, b_re, b_im, c_re, c_im, d_skip):
    L, H, P, G = S5_L, S5_H, S5_P, S5_G
    lam_re, lam_im = lam_re.astype(F32), lam_im.astype(F32)
    dt = jnp.exp(log_dt.astype(F32))[..., None]
    mag = jnp.exp(lam_re * dt)
    a_re = mag * jnp.cos(lam_im * dt)
    a_im = mag * jnp.sin(lam_im * dt)
    den = lam_re * lam_re + lam_im * lam_im
    f_re = ((a_re - 1.0) * lam_re + a_im * lam_im) / den
    f_im = (a_im * lam_re - (a_re - 1.0) * lam_im) / den
    b_re, b_im = b_re.astype(F32), b_im.astype(F32)
    bb_re = f_re[..., None] * b_re - f_im[..., None] * b_im
    bb_im = f_re[..., None] * b_im + f_im[..., None] * b_re
    tau = jnp.arange(L + 1, dtype=F32)[:, None, None, None]
    mag_t = jnp.exp(lam_re[None] * dt[None] * tau)
    ang_t = lam_im[None] * dt[None] * tau
    p_re = mag_t * jnp.cos(ang_t)
    p_im = mag_t * jnp.sin(ang_t)
    c_re, c_im = c_re.astype(F32), c_im.astype(F32)
    e_re = c_re[None] * p_re[:, :, :, None, :] - c_im[None] * p_im[:, :, :, None, :]
    e_im = c_re[None] * p_im[:, :, :, None, :] + c_im[None] * p_re[:, :, :, None, :]
    k = (jnp.einsum('tdgjp,dgpi->tdgji', e_re, bb_re, precision=HIGHEST)
         - jnp.einsum('tdgjp,dgpi->tdgji', e_im, bb_im, precision=HIGHEST))
    s_i, t_i = np.meshgrid(np.arange(L), np.arange(L), indexing='ij')
    kf = k[np.clip(t_i - s_i, 0, L), 0] * jnp.asarray(t_i >= s_i, F32)[..., None, None, None]
    kb = k[np.clip(s_i - t_i, 0, L), 1] * jnp.asarray(s_i >= t_i, F32)[..., None, None, None]
    skip = (jnp.asarray(s_i == t_i, F32)[..., None, None, None]
            * (jnp.eye(H, dtype=F32)[None] * d_skip.astype(F32).reshape(G, 1, H))[None, None])
    intra = (kf + kb + skip).transpose(2, 0, 4, 1, 3).reshape(G, L * H, L * H)

    def state_in(d, pw):
        pr, pi = p_re[pw, d], p_im[pw, d]
        re = pr[..., None] * bb_re[d][None] - pi[..., None] * bb_im[d][None]
        im = pr[..., None] * bb_im[d][None] + pi[..., None] * bb_re[d][None]
        return jnp.concatenate([re, im], axis=2).transpose(1, 0, 3, 2).reshape(G, L * H, 2 * P)

    sf = state_in(0, np.arange(L)[::-1].copy())
    sb = state_in(1, np.arange(L))
    w1 = jnp.concatenate([intra, sf, sb], axis=-1)

    def state_out(d, pw):
        er, ei = e_re[pw, d], e_im[pw, d]
        return jnp.concatenate([er, -ei], axis=-1).transpose(1, 3, 0, 2).reshape(G, 2 * P, L * H)

    w2 = jnp.concatenate([state_out(0, np.arange(1, L + 1)),
                          state_out(1, np.arange(L, 0, -1))], axis=1)
    ar, ai = p_re[L], p_im[L]
    rows = []
    for d in range(2):
        rows += [jnp.concatenate([ar[d], ar[d]], -1), jnp.concatenate([-ai[d], ai[d]], -1),
                 jnp.concatenate([ai[d], -ai[d]], -1)]
    rows += [jnp.zeros_like(rows[0])] * 2
    al = jnp.stack(rows, axis=1)
    return w1.astype(BF16), w2.astype(BF16), al


def _s5_kernel(x_ref, w1_ref, w2_ref, al_ref, y_ref, xw_ref, sw_ref, hh_ref, *, nch, nctx):
    lh = S5_L * S5_H
    p2 = 2 * S5_P
    xw_ref[...] = jnp.dot(x_ref[0], w1_ref[0], preferred_element_type=F32)
    sw_ref[:, 0:p2] = pltpu.roll(xw_ref[:, lh:lh + p2], S5_P, 1)
    sw_ref[:, p2:2 * p2] = pltpu.roll(xw_ref[:, lh + p2:lh + 2 * p2], S5_P, 1)
    al = al_ref[0]
    a1f, a2f, a3f, a1b, a2b, a3b = [al[i:i + 1] for i in range(6)]

    def body(k, carry):
        vf, wf, vb, wb = carry
        kb = jnp.where(k < nctx, nctx - 1 - k, nch + nctx - 1 - k)
        rf = pl.ds(pl.multiple_of(k * 8, 8), 8)
        rb = pl.ds(pl.multiple_of(kb * 8, 8), 8)
        hh_ref[rf, 0:p2] = vf
        hh_ref[rb, p2:2 * p2] = vb
        s_f = xw_ref[rf, lh:lh + p2]
        s_fs = sw_ref[rf, 0:p2]
        s_b = xw_ref[rb, lh + p2:lh + 2 * p2]
        s_bs = sw_ref[rb, p2:2 * p2]
        return (vf * a1f + wf * a2f + s_f, wf * a1f + vf * a3f + s_fs,
                vb * a1b + wb * a2b + s_b, wb * a1b + vb * a3b + s_bs)

    zero = jnp.zeros((8, p2), F32)
    lax.fori_loop(0, nch, body, (zero,) * 4)
    y_ref[0] = (xw_ref[:, 0:lh] + _bdot(hh_ref[...], w2_ref[0])).astype(BF16)


def _s5(u, tables):
    nb, s, _ = u.shape
    nch = s // S5_L
    lh = S5_L * S5_H
    r = nch * 8
    x = u.astype(BF16).reshape(nb, nch, S5_L, S5_G, S5_H).transpose(3, 1, 0, 2, 4)
    x = jnp.pad(x, ((0, 0), (0, 0), (0, 8 - nb), (0, 0), (0, 0))).reshape(S5_G, r, lh)
    w1, w2, al = tables
    y = pl.pallas_call(
        functools.partial(_s5_kernel, nch=nch, nctx=CTX // S5_L),
        out_shape=jax.ShapeDtypeStruct((S5_G, r, lh), BF16),
        grid=(S5_G,),
        in_specs=[pl.BlockSpec((1, r, lh), lambda g: (g, 0, 0)),
                  pl.BlockSpec((1, lh, 2 * lh), lambda g: (g, 0, 0)),
                  pl.BlockSpec((1, lh, lh), lambda g: (g, 0, 0)),
                  pl.BlockSpec((1, 8, 2 * S5_P), lambda g: (g, 0, 0))],
        out_specs=pl.BlockSpec((1, r, lh), lambda g: (g, 0, 0)),
        scratch_shapes=[pltpu.VMEM((r, 2 * lh), F32), pltpu.VMEM((r, lh), F32), pltpu.VMEM((r, lh), F32)],
        compiler_params=_cparams(("parallel",)),
        name="s5",
    )(x, w1, w2, al)
    y = y.reshape(S5_G, nch, 8, S5_L, S5_H)[:, :, :nb]
    return y.transpose(2, 1, 3, 0, 4).reshape(nb, s, HALF)


def _na_bias_tables(rpb, rows):
    w = GRID_W
    nh, ndr, ndc = rpb.shape
    v = jnp.pad(rpb.astype(F32), ((0, 0), (0, 0), (w - NA_WC, 2 * w - ndc - (w - NA_WC))))
    skew = jnp.tile(v, (1, 1, w))[:, :, :w * (2 * w - 1)].reshape(nh, ndr, w, 2 * w - 1)
    tc = skew[..., w - 1:]
    valid, dr = [], []
    for r0 in (0, NA_QR, rows - NA_QR):
        k0 = int(np.clip(r0 - NA_WR // 2, 0, rows - NA_KR))
        rq = r0 + np.arange(NA_QR)[:, None, None, None]
        wq = np.arange(w)[None, :, None, None]
        kr = k0 + np.arange(NA_KR)[None, None, :, None]
        kc = np.arange(w)[None, None, None, :]
        rs = np.clip(rq - NA_WR // 2, 0, rows - NA_WR)
        cs = np.clip(wq - NA_WC // 2, 0, w - NA_WC)
        valid.append((kr >= rs) & (kr < rs + NA_WR) & (kc >= cs) & (kc < cs + NA_WC))
        dr.append(np.clip(kr - rq + NA_WR - 1, 0, ndr - 1)[:, 0, :, 0])
    dr = np.stack(dr).reshape(-1)
    bias = jnp.take(tc, jnp.asarray(dr), axis=1)
    bias = bias.reshape(nh, 3, NA_QR, NA_KR, w, w).transpose(0, 1, 2, 4, 3, 5)
    bias = jnp.where(jnp.asarray(np.stack(valid))[None], bias, NEG)
    return bias.reshape(nh, 3, NA_QR * w, NA_KR * w)


def _natten_kernel(q_ref, k_ref, v_ref, bias_ref, o_ref, qb_ref, kb_ref, vb_ref, *, rows):
    nq = NA_QR * GRID_W
    nk = NA_KR * GRID_W
    nblk = rows // NA_QR
    scale = NA_DH ** -0.5
    qb_ref[...] = (q_ref[0] * scale).astype(BF16)
    kb_ref[...] = k_ref[0].astype(BF16)
    vb_ref[...] = v_ref[0].astype(BF16)
    lane = lax.broadcasted_iota(jnp.int32, (1, 2 * NA_DH), 1)
    head_mask = [lane < NA_DH, lane >= NA_DH]
    kc = kb_ref[0:CTX, :]
    vc = vb_ref[0:CTX, :]

    def attend(q, extra):
        outs = []
        for h in range(2):
            qm = jnp.where(head_mask[h], q, jnp.zeros_like(q))
            s_c = _bdot_nt(qm, kc)
            m = jnp.max(s_c, axis=-1, keepdims=True)
            if extra is not None:
                k_loc, v_loc, bias = extra
                s_l = _bdot_nt(qm, k_loc) + bias[h]
                m = jnp.maximum(m, jnp.max(s_l, axis=-1, keepdims=True))
                p_l = jnp.exp(s_l - m)
            p_c = jnp.exp(s_c - m)
            den = jnp.sum(p_c, axis=-1, keepdims=True)
            acc = _bdot(p_c, vc)
            if extra is not None:
                den = den + jnp.sum(p_l, axis=-1, keepdims=True)
                acc = acc + _bdot(p_l, v_loc)
            outs.append(acc / den)
        return jnp.where(head_mask[0], outs[0], outs[1])

    o_ref[0, 0:CTX, :] = attend(qb_ref[0:CTX, :], None)

    def body(i, carry):
        r0 = i * NA_QR
        k0 = jnp.clip(r0 - NA_WR // 2, 0, rows - NA_KR)
        pat = jnp.where(i == 0, 0, jnp.where(i == nblk - 1, 2, 1))
        qs = pl.multiple_of(CTX + r0 * GRID_W, GRID_W)
        ks = pl.multiple_of(CTX + k0 * GRID_W, GRID_W)
        q = qb_ref[pl.ds(qs, nq), :]
        k_loc = kb_ref[pl.ds(ks, nk), :]
        v_loc = vb_ref[pl.ds(ks, nk), :]
        bias = [bias_ref[h, pat] for h in range(2)]
        o_ref[0, pl.ds(qs, nq), :] = attend(q, (k_loc, v_loc, bias))
        return carry

    lax.fori_loop(0, nblk, body, 0)


def _natten(a, bias):
    nb, s, _ = a.shape
    rows = (s - CTX) // GRID_W
    pw = 2 * NA_DH
    nq, nk = NA_QR * GRID_W, NA_KR * GRID_W
    hb = HALF // pw
    return pl.pallas_call(
        functools.partial(_natten_kernel, rows=rows),
        out_shape=jax.ShapeDtypeStruct((nb, s, HALF), F32),
        grid=(nb, hb),
        in_specs=[pl.BlockSpec((1, s, pw), lambda b, h: (b, 0, hb + h)),
                  pl.BlockSpec((1, s, pw), lambda b, h: (b, 0, 2 * hb + h)),
                  pl.BlockSpec((1, s, pw), lambda b, h: (b, 0, 3 * hb + h)),
                  pl.BlockSpec((2, 3, nq, nk), lambda b, h: (h, 0, 0, 0))],
        out_specs=pl.BlockSpec((1, s, pw), lambda b, h: (b, 0, h)),
        scratch_shapes=[pltpu.VMEM((s, pw), BF16)] * 3,
        compiler_params=_cparams(("parallel", "parallel")),
        name="natten",
    )(a, a, a, bias)


def _gelu_tanh(x):
    return 0.5 * x * (1.0 + jnp.tanh(math.sqrt(2.0 / math.pi) * (x + 0.044715 * (x * x * x))))


def _out_kernel(ya_ref, yb_ref, h_ref, g_ref, e0_ref, e1_ref, wo_ref, lng_ref, lnb_ref, o_ref, *, glu):
    if glu:
        z = _bdot(_gelu_tanh(ya_ref[0].astype(F32)), e0_ref[...]) + e1_ref[...]
        ya = z[:, :HALF] * jax.nn.sigmoid(z[:, HALF:])
    else:
        parts = []
        for hd in range(GLA_HEADS):
            cols = slice(hd * GLA_DV, (hd + 1) * GLA_DV)
            o = ya_ref[0, :, cols]
            o = o * lax.rsqrt(jnp.mean(o * o, axis=-1, keepdims=True) + EPS) * e1_ref[:, cols]
            parts.append(o * _silu(e0_ref[0, :, cols].astype(F32)))
        ya = jnp.concatenate(parts, axis=1)
    y = _bdot(ya, wo_ref[0:HALF, :]) + _bdot(yb_ref[0], wo_ref[HALF:, :])
    o_ref[0] = _layer_norm(ALPHA * h_ref[0] + g_ref[0] * y, lng_ref[...], lnb_ref[...])


def _outproj(ya, yb, h, mods_l, w_out, ln_g, ln_b, *, glu_w=None, glu_b=None, proj=None, norm_g=None):
    nb, s, _ = h.shape
    row = _mod_row(nb)
    glu = glu_w is not None
    tile = lambda b, j: (b, j, 0)
    const = lambda b, j: (0, 0)
    if glu:
        extra_specs = [pl.BlockSpec((HALF, D_MODEL), const), pl.BlockSpec((1, D_MODEL), const)]
        extra = [glu_w.astype(BF16), glu_b.astype(F32).reshape(1, D_MODEL)]
    else:
        rblk = (2 * GLA_HEADS * GLA_DK + HALF) // HALF
        extra_specs = [pl.BlockSpec((1, ROW_TILE, HALF), lambda b, j: (b, j, rblk)),
                       pl.BlockSpec((1, HALF), const)]
        extra = [proj, norm_g.astype(F32).reshape(1, HALF)]
    in_specs = [pl.BlockSpec((1, ROW_TILE, HALF), tile),
                pl.BlockSpec((1, ROW_TILE, HALF), tile),
                pl.BlockSpec((1, ROW_TILE, D_MODEL), tile),
                pl.BlockSpec((1, 1, D_MODEL), lambda b, j: (row(b, j), 0, 2))] + extra_specs + [
                pl.BlockSpec((D_MODEL, D_MODEL), const), pl.BlockSpec((1, D_MODEL), const),
                pl.BlockSpec((1, D_MODEL), const)]
    return pl.pallas_call(
        functools.partial(_out_kernel, glu=glu),
        out_shape=jax.ShapeDtypeStruct((nb, s, D_MODEL), F32),
        grid=(nb, s // ROW_TILE),
        in_specs=in_specs,
        out_specs=pl.BlockSpec((1, ROW_TILE, D_MODEL), tile),
        compiler_params=_cparams(("parallel", "parallel")),
        name="outproj_glu" if glu else "outproj_gate",
    )(ya, yb, h, mods_l, *extra, w_out.astype(BF16), ln_g.astype(F32).reshape(1, D_MODEL),
      ln_b.astype(F32).reshape(1, D_MODEL))


def _log_sigmoid(x):
    return jnp.minimum(x, 0.0) - jnp.log1p(jnp.exp(-jnp.abs(x)))


def _gla_kernel(q_ref, k_ref, v_ref, z_ref, wg_ref, bg_ref, o_ref, gf_ref, gb_ref, ob_ref, st_ref, *, nch):
    c = GLA_CHUNK
    dk2 = 2 * GLA_DK
    nctx = CTX // c
    z = z_ref[0]
    gf_ref[...] = _log_sigmoid(_hdot(z, wg_ref[0, 0]) + bg_ref[0, 0]) * (1.0 / GLA_TAU)
    gb_ref[...] = _log_sigmoid(_hdot(z, wg_ref[1, 0]) + bg_ref[1, 0]) * (1.0 / GLA_TAU)
    ri = lax.broadcasted_iota(jnp.int32, (c, c), 0)
    ci = lax.broadcasted_iota(jnp.int32, (c, c), 1)
    lane = lax.broadcasted_iota(jnp.int32, (1, dk2), 1)
    head_mask = [lane < GLA_DK, lane >= GLA_DK]
    scale = GLA_DK ** -0.5

    st_ref[...] = jnp.zeros_like(st_ref)

    def chunk(direction, n):
        g_ref = gf_ref if direction == 0 else gb_ref
        keep = (ci <= ri) if direction == 0 else (ci >= ri)
        tri = jnp.where(keep, 1.0, 0.0).astype(BF16)
        rs = pl.ds(pl.multiple_of(n * c, c), c)
        g = g_ref[rs, :]
        g_hi = g.astype(BF16)
        g_lo = (g - g_hi.astype(F32)).astype(BF16)
        gc = (jnp.dot(tri, g_hi, preferred_element_type=F32)
              + jnp.dot(tri, g_lo, preferred_element_type=F32))
        gend = gc[c - 1:c, :] if direction == 0 else gc[0:1, :]
        q = q_ref[0, rs, :] * scale
        k = k_ref[0, rs, :]
        qd = q * jnp.exp(gc)
        kd = k * jnp.exp(-gc)
        kdec = k * jnp.exp(gend - gc)
        dcol = jnp.exp(jnp.broadcast_to(gend, (dk2, dk2)).T)
        dcol = jnp.concatenate([dcol, dcol], axis=1)
        qs = jnp.concatenate([jnp.where(head_mask[0], qd, 0.0), jnp.where(head_mask[1], qd, 0.0)],
                             axis=0).astype(BF16)
        att = jnp.where(jnp.concatenate([keep, keep], axis=0), _bdot_nt(qs, kd), 0.0)
        v = v_ref[0, rs, :].astype(BF16)
        st = st_ref[direction]
        o2 = _bdot(att, v) + _bdot(qs, st)
        st_ref[direction] = dcol * st + jnp.dot(kdec.T.astype(BF16), v, preferred_element_type=F32)
        o = jnp.concatenate([o2[0:c, 0:GLA_DV], o2[c:2 * c, GLA_DV:2 * GLA_DV]], axis=1)
        if direction == 0:
            o_ref[0, rs, :] = o
        else:
            ob_ref[rs, :] = o

    def body(i, carry):
        chunk(0, i)
        chunk(1, jnp.where(i < nctx, nctx - 1 - i, nch + nctx - 1 - i))
        return carry

    lax.fori_loop(0, nch, body, 0, unroll=4)
    o_ref[0] += ob_ref[...]


def _gla(a, z, wg, bg):
    nb, s, _ = a.shape
    nch = s // GLA_CHUNK
    hp = GLA_HEADS // 2
    dk2, dv2 = 2 * GLA_DK, 2 * GLA_DV
    return pl.pallas_call(
        functools.partial(_gla_kernel, nch=nch),
        out_shape=jax.ShapeDtypeStruct((nb, s, HALF), F32),
        grid=(nb, hp),
        in_specs=[pl.BlockSpec((1, s, dk2), lambda b, h: (b, 0, h)),
                  pl.BlockSpec((1, s, dk2), lambda b, h: (b, 0, hp + h)),
                  pl.BlockSpec((1, s, dv2), lambda b, h: (b, 0, hp + h)),
                  pl.BlockSpec((1, s, 128), lambda b, h: (b, 0, 0)),
                  pl.BlockSpec((2, 1, 128, dk2), lambda b, h: (0, h, 0, 0)),
                  pl.BlockSpec((2, 1, 1, dk2), lambda b, h: (0, h, 0, 0))],
        out_specs=pl.BlockSpec((1, s, dv2), lambda b, h: (b, 0, h)),
        scratch_shapes=[pltpu.VMEM((s, dk2), F32), pltpu.VMEM((s, dk2), F32), pltpu.VMEM((s, dv2), F32),
                        pltpu.VMEM((2, dk2, dv2), F32)],
        compiler_params=_cparams(("parallel", "parallel")),
        name="gla",
    )(a, a, a, z, wg, bg)


def _rope_tables(s):
    t = np.arange(s - CTX)
    n = DIFF_DH // 4
    inv = ROPE_BASE ** (-np.arange(n, dtype=np.float64) / n)
    ang_r = (t // GRID_W)[:, None] * inv[None]
    ang_c = (t % GRID_W)[:, None] * inv[None]
    ang = np.concatenate([ang_r, ang_r, ang_c, ang_c], axis=1)
    ang = np.concatenate([ang, ang], axis=1)
    cos = np.concatenate([np.ones((CTX, 128)), np.cos(ang)], axis=0)
    sin = np.concatenate([np.zeros((CTX, 128)), np.sin(ang)], axis=0)
    first = ((np.arange(128) // n) % 2 == 0)[None]
    sin_up = np.where(first, -sin, 0.0)
    sin_dn = np.where(first, 0.0, sin)
    f = lambda z: jnp.asarray(z, F32)
    return f(cos), f(sin_up), f(sin_dn)


def _rope_kernel(q_ref, k_ref, cos_ref, su_ref, sd_ref, qo_ref, ko_ref):
    n = DIFF_DH // 4
    cos = jnp.concatenate([cos_ref[...]] * DIFF_HEADS, axis=1)
    su = jnp.concatenate([su_ref[...]] * DIFF_HEADS, axis=1)
    sd = jnp.concatenate([sd_ref[...]] * DIFF_HEADS, axis=1)

    def rot(x):
        w = x.shape[1]
        return x * cos + pltpu.roll(x, w - n, 1) * su + pltpu.roll(x, n, 1) * sd

    qo_ref[0] = rot(q_ref[0].astype(F32) * (DIFF_DH ** -0.5 * LOG2E)).astype(BF16)
    ko_ref[0] = rot(k_ref[0].astype(F32)).astype(BF16)


def _rope(a, tables):
    nb, s, _ = a.shape
    qblk = (2 * GLA_HEADS * GLA_DK + 2 * HALF) // HALF
    tile = lambda b, j: (b, j, 0)
    tab = pl.BlockSpec((ROW_TILE, 128), lambda b, j: (j, 0))
    return pl.pallas_call(
        _rope_kernel,
        out_shape=[jax.ShapeDtypeStruct((nb, s, HALF), BF16)] * 2,
        grid=(nb, s // ROW_TILE),
        in_specs=[pl.BlockSpec((1, ROW_TILE, HALF), lambda b, j: (b, j, qblk)),
                  pl.BlockSpec((1, ROW_TILE, HALF), lambda b, j: (b, j, qblk + 1)),
                  tab, tab, tab],
        out_specs=[pl.BlockSpec((1, ROW_TILE, HALF), tile)] * 2,
        compiler_params=_cparams(("parallel", "parallel")),
        name="rope",
    )(a, a, *tables)


def _diff_kernel(q_ref, k_ref, v_ref, lam_ref, ng_ref, o_ref, vb_ref, *, lam_init):
    j = pl.program_id(2)
    lane = lax.broadcasted_iota(jnp.int32, (1, 2 * DIFF_DH), 1)
    map_mask = [lane < DIFF_DH, lane >= DIFF_DH]
    lv = lam_ref[...]
    lam = (jnp.exp(jnp.sum(lv[0:1] * lv[1:2], axis=-1, keepdims=True))
           - jnp.exp(jnp.sum(lv[2:3] * lv[3:4], axis=-1, keepdims=True)) + lam_init)

    @pl.when(j == 0)
    def _():
        vb_ref[...] = v_ref[0].astype(BF16)

    def attend(nkeys):
        q = q_ref[0]
        k = k_ref[0, 0:nkeys, :]
        sc, mx, inv = [], [], []
        for m in range(2):
            s = _bdot_nt(jnp.where(map_mask[m], q, jnp.zeros_like(q)), k)
            sc.append(s)
            mx.append(jnp.max(s, axis=-1, keepdims=True))
            inv.append(1.0 / jnp.sum(jnp.exp2(s - mx[m]), axis=-1, keepdims=True))
        w = (jnp.exp2(sc[0] - mx[0]) * inv[0] - jnp.exp2(sc[1] - mx[1]) * (lam * inv[1])).astype(BF16)
        o = jnp.dot(w, vb_ref[0:nkeys, :], preferred_element_type=F32)
        o = o * lax.rsqrt(jnp.mean(o * o, axis=-1, keepdims=True) + EPS) * ng_ref[...]
        o_ref[0] = o * (1.0 - lam_init)

    @pl.when(j == 0)
    def _():
        attend(CTX)

    @pl.when(j > 0)
    def _():
        attend(k_ref.shape[1])


def _diff(qr, kr, a, lam_vecs, ng, lam_init):
    nb, s, _ = qr.shape
    dw = 2 * DIFF_DH
    vblk = (2 * GLA_HEADS * GLA_DK + 4 * HALF) // dw
    return pl.pallas_call(
        functools.partial(_diff_kernel, lam_init=lam_init),
        out_shape=jax.ShapeDtypeStruct((nb, s, HALF), F32),
        grid=(nb, DIFF_HEADS, s // ROW_TILE),
        in_specs=[pl.BlockSpec((1, ROW_TILE, dw), lambda b, h, j: (b, j, h)),
                  pl.BlockSpec((1, s, dw), lambda b, h, j: (b, 0, h)),
                  pl.BlockSpec((1, s, DIFF_DV), lambda b, h, j: (b, 0, vblk + h)),
                  pl.BlockSpec((8, DIFF_DH), lambda b, h, j: (0, 0)),
                  pl.BlockSpec((1, DIFF_DV), lambda b, h, j: (0, 0))],
        out_specs=pl.BlockSpec((1, ROW_TILE, DIFF_DV), lambda b, h, j: (b, j, h)),
        scratch_shapes=[pltpu.VMEM((s, DIFF_DV), BF16)],
        compiler_params=_cparams(("parallel", "parallel", "arbitrary")),
        name="diff_attn",
    )(qr, kr, a, lam_vecs, ng)


def _moe_kernel(h_ref, m0_ref, m1_ref, mc_ref, wr_ref, br_ref, wg_ref, wu_ref, wd_ref, lng_ref, lnb_ref, o_ref,
                as_ref, cs_ref, ps_ref, ys_ref, win_ref, *, seq, win):
    t = pl.program_id(0)
    e = pl.program_id(1)
    tm = h_ref.shape[0]
    lane = lax.broadcasted_iota(jnp.int32, (1, 128), 1).astype(F32)
    d = D_MODEL

    def mod(col):
        b0 = (t * tm) // seq
        grow = t * tm + lax.broadcasted_iota(jnp.int32, (tm, 1), 0)
        in_next = grow >= (b0 + 1) * seq
        local = grow - jnp.where(in_next, b0 + 1, b0) * seq
        cols = slice(col * d, (col + 1) * d)
        return jnp.where(local < CTX, mc_ref[0, :, cols],
                         jnp.where(in_next, m1_ref[0, :, cols], m0_ref[0, :, cols]))

    @pl.when(e == 0)
    def _():
        a = h_ref[...] * (1.0 + mod(1)) + mod(0)
        logits = _hdot(a, wr_ref[...]) + br_ref[...]
        is_g = lane < N_GROUPS
        lg = jnp.where(is_g, logits, NEG)
        mg = jnp.max(lg, axis=-1, keepdims=True)
        g_sel = jnp.min(jnp.where(lg == mg, lane, 128.0), axis=-1, keepdims=True)
        p_grp = 1.0 / jnp.sum(jnp.where(is_g, jnp.exp(lg - mg), 0.0), axis=-1, keepdims=True)
        lo = N_GROUPS + g_sel * EPG
        in_grp = (lane >= lo) & (lane < lo + EPG)
        le = jnp.where(in_grp, logits, NEG)
        v1 = jnp.max(le, axis=-1, keepdims=True)
        i1 = jnp.min(jnp.where(le == v1, lane, 128.0), axis=-1, keepdims=True)
        le2 = jnp.where(lane == i1, NEG, le)
        v2 = jnp.max(le2, axis=-1, keepdims=True)
        i2 = jnp.min(jnp.where(le2 == v2, lane, 128.0), axis=-1, keepdims=True)
        e2 = jnp.exp(v2 - v1)
        w1 = p_grp / (1.0 + e2)
        w2 = p_grp * e2 / (1.0 + e2)
        comb = jnp.where(lane == i1, w1, jnp.where(lane == i2, w2, 0.0))

        onehot = jnp.where(lane == g_sel, 1.0, 0.0).astype(BF16)
        ri = lax.broadcasted_iota(jnp.int32, (tm, tm), 0)
        ci = lax.broadcasted_iota(jnp.int32, (tm, tm), 1)
        ranks = jnp.dot(jnp.where(ci <= ri, 1.0, 0.0).astype(BF16), onehot, preferred_element_type=F32)
        cnt = ranks[tm - 1:tm, :]
        n = [jnp.sum(jnp.where(lane == float(g), cnt, 0.0), axis=-1, keepdims=True) for g in range(N_GROUPS)]
        off = [jnp.zeros_like(n[0]), n[0], n[0] + n[1], n[0] + n[1] + n[2]]
        off_vec = sum(jnp.where(lane == float(g), off[g], 0.0) for g in range(1, N_GROUPS))
        dest = jnp.sum(jnp.where(lane == g_sel, ranks + off_vec, 0.0), axis=-1, keepdims=True) - 1.0

        pos = lax.broadcasted_iota(jnp.int32, (1, tm), 1).astype(F32)
        ps_ref[...] = jnp.where(dest == pos, 1.0, 0.0).astype(BF16)
        hi = jnp.floor(dest * (1.0 / 32.0))
        digits = jnp.where(lane == 0.0, hi, jnp.where(lane == 1.0, dest - 32.0 * hi, 0.0)).astype(BF16)
        ri8 = lax.broadcasted_iota(jnp.int32, (8, 128), 0)
        ci8 = lax.broadcasted_iota(jnp.int32, (8, 128), 1)
        dig_t = _bdot_nt(jnp.where(ri8 == ci8, 1.0, 0.0), digits)
        dest_row = 32.0 * dig_t[0:1, :] + dig_t[1:2, :]
        rows = lax.broadcasted_iota(jnp.int32, (tm, 1), 0).astype(F32)
        pg = jnp.where(dest_row == rows, 1.0, 0.0).astype(BF16)
        as_ref[...] = jnp.dot(pg, a.astype(BF16), preferred_element_type=F32).astype(BF16)
        c_hi = comb.astype(BF16)
        c_lo = (comb - c_hi.astype(F32)).astype(BF16)
        packed = jnp.where(lane < 64.0, c_hi.astype(F32), pltpu.roll(c_lo.astype(F32), 64, 1)).astype(BF16)
        cg = jnp.dot(pg, packed, preferred_element_type=F32)
        cs_ref[...] = cg + pltpu.roll(cg, 64, 1)
        ys_ref[...] = jnp.zeros_like(ys_ref)
        for g in range(N_GROUPS):
            start = jnp.minimum((jnp.sum(off[g]).astype(jnp.int32) // 16) * 16, tm - win)
            end = jnp.sum(off[g] + n[g]).astype(jnp.int32)
            win_ref[g] = start
            win_ref[N_GROUPS + g] = jnp.where(end > start, (end - start + win - 1) // win, 0)

    g = (e * MOE_EB) // EPG
    start = win_ref[g]

    def window(w, carry):
        lo = start + w * win
        rs = pl.ds(pl.multiple_of(jnp.minimum(lo, tm - win), 16), win)
        x = as_ref[rs, :]
        row = jnp.minimum(lo, tm - win) + lax.broadcasted_iota(jnp.int32, (win, 1), 0)
        cs = jnp.where(row >= lo, cs_ref[rs, :], 0.0)
        upd = None
        for j in range(MOE_EB):
            sel = lane == (e * MOE_EB + j + N_GROUPS).astype(F32)
            cw = jnp.sum(jnp.where(sel, cs, 0.0), axis=-1, keepdims=True)
            hg = jnp.dot(x, wg_ref[j], preferred_element_type=F32)
            hu = jnp.dot(x, wu_ref[j], preferred_element_type=F32)
            dn = jnp.dot((_silu(hg) * hu * cw).astype(BF16), wd_ref[j], preferred_element_type=F32)
            upd = dn if upd is None else upd + dn
        ys_ref[rs, :] += upd
        return carry

    lax.fori_loop(0, win_ref[N_GROUPS + g], window, 0)

    @pl.when(e == N_EXPERTS // MOE_EB - 1)
    def _():
        y = jnp.dot(ps_ref[...], ys_ref[...].astype(BF16), preferred_element_type=F32)
        o_ref[...] = _layer_norm(ALPHA * h_ref[...] + mod(2) * y, lng_ref[...], lnb_ref[...])


def _moe(h, mods_l, w_router, b_router, w_gate, w_up, w_down, ln_g, ln_b):
    nb, s, d = h.shape
    rows = nb * s
    tm = 1024 if rows % 1024 == 0 else 256
    win = tm // N_GROUPS + tm // 16
    nt = rows // tm
    const = lambda t, e: (0, 0)
    b0 = lambda t: (t * tm) // s
    neb = N_EXPERTS // MOE_EB
    w_gate, w_up, w_down = w_gate.astype(BF16), w_up.astype(BF16), w_down.astype(BF16)
    out = pl.pallas_call(
        functools.partial(_moe_kernel, seq=s, win=win),
        out_shape=jax.ShapeDtypeStruct((rows, d), F32),
        grid=(nt, neb),
        in_specs=[pl.BlockSpec((tm, d), lambda t, e: (t, 0)),
                  pl.BlockSpec((1, 1, 3 * d), lambda t, e: (b0(t), 0, 1)),
                  pl.BlockSpec((1, 1, 3 * d), lambda t, e: (jnp.minimum(b0(t) + 1, nb - 1), 0, 1)),
                  pl.BlockSpec((1, 1, 3 * d), lambda t, e: (nb, 0, 1)),
                  pl.BlockSpec((d, 128), const),
                  pl.BlockSpec((1, 128), const),
                  pl.BlockSpec((MOE_EB, d, D_EXPERT), lambda t, e: (e, 0, 0)),
                  pl.BlockSpec((MOE_EB, d, D_EXPERT), lambda t, e: (e, 0, 0)),
                  pl.BlockSpec((MOE_EB, D_EXPERT, d), lambda t, e: (e, 0, 0)),
                  pl.BlockSpec((1, d), const),
                  pl.BlockSpec((1, d), const)],
        out_specs=pl.BlockSpec((tm, d), lambda t, e: (t, 0)),
        scratch_shapes=[pltpu.VMEM((tm, d), BF16), pltpu.VMEM((tm, 128), F32), pltpu.VMEM((tm, tm), BF16),
                        pltpu.VMEM((tm, d), F32), pltpu.SMEM((2 * N_GROUPS,), jnp.int32)],
        compiler_params=_cparams(("parallel", "arbitrary")),
        name="moe",
    )(h.reshape(rows, d), mods_l, mods_l, mods_l, w_router, b_router, w_gate, w_up, w_down,
      ln_g.astype(F32).reshape(1, d), ln_b.astype(F32).reshape(1, d))
    return out.reshape(nb, s, d)


def _pad_lanes(x, n):
    return jnp.pad(x, [(0, 0)] * (x.ndim - 1) + [(0, n - x.shape[-1])])


def kernel(x, c, ctx, c_ctx, mod_w, mod_b, ln_g, ln_b, even_w_in, even_w_out, s5_lam_re, s5_lam_im, s5_log_dt, s5_b_re, s5_b_im, s5_c_re, s5_c_im, s5_d, s5_glu_w, s5_glu_b, na_rpb, odd_w_in, odd_w_out, gla_gate_w, gla_gate_b, gla_norm_g, diff_lq1, diff_lk1, diff_lq2, diff_lk2, diff_norm_g, moe_w_grp, moe_b_grp, moe_w_exp, moe_b_exp, moe_w_gate, moe_w_up, moe_w_down):
    nb, t, d = x.shape
    assert d == D_MODEL and ctx.shape[1] == CTX and nb < 8 and t % (NA_QR * GRID_W) == 0
    s = CTX + t
    rows = t // GRID_W
    h = jnp.concatenate([ctx, x], axis=1).astype(F32)
    cc = jnp.zeros((8, d), F32).at[:nb].set(c.astype(F32)).at[nb].set(c_ctx.astype(F32))
    mods = _mods(cc, mod_w.astype(F32), mod_b.astype(F32)).reshape(DEPTH, 8, 1, 6 * d)
    rope_tabs = _rope_tables(s)
    for i in range(DEPTH):
        j = i // 2
        mods_l = mods[i]
        if i % 2 == 0:
            a = _inproj(h, mods_l, even_w_in[j].astype(BF16))
            tables = _s5_tables(s5_lam_re[j], s5_lam_im[j], s5_log_dt[j], s5_b_re[j], s5_b_im[j],
                                s5_c_re[j], s5_c_im[j], s5_d[j])
            y_a = _s5(a[..., :HALF], tables)
            y_b = _natten(a, _na_bias_tables(na_rpb[j], rows))
            h = _outproj(y_a, y_b, h, mods_l, even_w_out[j], ln_g[i, 0], ln_b[i, 0],
                         glu_w=s5_glu_w[j], glu_b=s5_glu_b[j])
        else:
            lam_init = 0.8 - 0.6 * math.exp(-0.3 * i)
            w = odd_w_in[j]
            nqk = 2 * GLA_HEADS * GLA_DK + 2 * HALF
            w = jnp.concatenate([w[:, :nqk], w[:, nqk + 2 * GLA_RANK:],
                                 _pad_lanes(w[:, nqk:nqk + 2 * GLA_RANK], 128)], axis=1).astype(BF16)
            a, z = _inproj(h, mods_l, w, f32_tail=True)
            gw = gla_gate_w[j].astype(F32)
            wg = jnp.zeros((2, 128, GLA_HEADS * GLA_DK), F32)
            for dd in range(2):
                wg = wg.at[dd, dd * GLA_RANK:(dd + 1) * GLA_RANK].set(gw[dd])
            wg = wg.reshape(2, 128, 2, 2 * GLA_DK).transpose(0, 2, 1, 3)
            bg = gla_gate_b[j].astype(F32).reshape(2, 2, 1, 2 * GLA_DK)
            y_a = _gla(a, z, wg, bg)
            qr, kr = _rope(a, rope_tabs)
            lam_vecs = jnp.zeros((8, DIFF_DH), F32).at[0].set(diff_lq1[j]).at[1].set(diff_lk1[j]) \
                .at[2].set(diff_lq2[j]).at[3].set(diff_lk2[j])
            y_b = _diff(qr, kr, a, lam_vecs, diff_norm_g[j].astype(F32).reshape(1, DIFF_DV), lam_init)
            h = _outproj(y_a, y_b, h, mods_l, odd_w_out[j], ln_g[i, 0], ln_b[i, 0],
                         proj=a, norm_g=gla_norm_g[j])
        w_router = _pad_lanes(jnp.concatenate([moe_w_grp[i], moe_w_exp[i]], axis=1).astype(F32), 128)
        b_router = _pad_lanes(jnp.concatenate([moe_b_grp[i], moe_b_exp[i]]).astype(F32)[None], 128)
        h = _moe(h, mods_l, w_router, b_router, moe_w_gate[i], moe_w_up[i], moe_w_down[i], ln_g[i, 1], ln_b[i, 1])
    return h[:, CTX:].astype(x.dtype)
```

```python
import functools
import math

import numpy as np
import jax
import jax.numpy as jnp
from jax import lax
from jax.experimental import pallas as pl
from jax.experimental.pallas import tpu as pltpu

F32 = jnp.float32
BF16 = jnp.bfloat16
HIGHEST = lax.Precision.HIGHEST

D_MODEL = 1024
DEPTH = 4
GRID_W = 64
CTX = 256
HALF = D_MODEL // 2
S5_H = 16
S5_G = HALF // S5_H
S5_P = 64
S5_L = 16
NA_HEADS = 8
NA_DH = 64
NA_WR = 8
NA_WC = 16
NA_QR = 4
NA_KR = 12
GLA_HEADS = 4
GLA_DK = 64
GLA_DV = 128
GLA_RANK = 16
GLA_TAU = 16.0
GLA_CHUNK = 64
DIFF_HEADS = 4
DIFF_DV = 128
DIFF_DH = 64
N_GROUPS = 4
EPG = 8
N_EXPERTS = N_GROUPS * EPG
D_EXPERT = D_MODEL // 4
MOE_EB = 4
ROPE_BASE = 10000.0
EPS = 1e-5
ALPHA = (2.0 * DEPTH) ** 0.25
EVEN_IN = 4 * HALF
ODD_IN_PAD = 3200
NEG = -1e30
LOG2E = math.log2(math.e)
ROW_TILE = 256
VMEM_LIMIT = 56 * 1024 * 1024


def _cparams(sem):
    return pltpu.CompilerParams(dimension_semantics=sem, vmem_limit_bytes=VMEM_LIMIT)


def _bdot(a, b):
    return jnp.dot(a.astype(BF16), b.astype(BF16), preferred_element_type=F32)


def _bdot_nt(a, b):
    return lax.dot_general(a.astype(BF16), b.astype(BF16), (((1,), (1,)), ((), ())),
                           preferred_element_type=F32)


def _hdot(a, b):
    return jnp.dot(a, b, precision=HIGHEST, preferred_element_type=F32)


def _layer_norm(x, g, b):
    mu = jnp.mean(x, axis=-1, keepdims=True)
    xc = x - mu
    var = jnp.mean(xc * xc, axis=-1, keepdims=True)
    return xc * lax.rsqrt(var + EPS) * g + b


def _silu(x):
    return x * jax.nn.sigmoid(x)


def _mods_kernel(cc_ref, w_ref, b_ref, o_ref):
    o_ref[0] = _hdot(_silu(cc_ref[...]), w_ref[0]) + b_ref[0]


def _mods(cc, mod_w, mod_b):
    nblk = 6
    return pl.pallas_call(
        _mods_kernel,
        out_shape=jax.ShapeDtypeStruct((DEPTH, 8, 6 * D_MODEL), F32),
        grid=(DEPTH, nblk),
        in_specs=[pl.BlockSpec((8, D_MODEL), lambda l, j: (0, 0)),
                  pl.BlockSpec((1, D_MODEL, D_MODEL), lambda l, j: (l, 0, j)),
                  pl.BlockSpec((1, 1, D_MODEL), lambda l, j: (l, 0, j))],
        out_specs=pl.BlockSpec((1, 8, D_MODEL), lambda l, j: (l, 0, j)),
        compiler_params=_cparams(("arbitrary", "arbitrary")),
        name="mods",
    )(cc, mod_w, mod_b.reshape(DEPTH, 1, 6 * D_MODEL))


def _mod_row(nbatch):
    return lambda b, j: jnp.where(j == 0, nbatch, b)


def _inproj_kernel(h_ref, sh_ref, sc_ref, w_ref, o_ref, *tail_ref):
    a = h_ref[0] * (1.0 + sc_ref[0]) + sh_ref[0]
    y = _bdot(a, w_ref[...])
    o_ref[0] = y.astype(BF16)
    if tail_ref:
        tail_ref[0][0] = y[:, y.shape[1] - 128:]


def _inproj(h, mods_l, w, f32_tail=False):
    nb, s, _ = h.shape
    n = w.shape[1]
    row = _mod_row(nb)
    tile = lambda b, j: (b, j, 0)
    out_shape = [jax.ShapeDtypeStruct((nb, s, n), BF16)]
    out_specs = [pl.BlockSpec((1, ROW_TILE, n), tile)]
    if f32_tail:
        out_shape.append(jax.ShapeDtypeStruct((nb, s, 128), F32))
        out_specs.append(pl.BlockSpec((1, ROW_TILE, 128), tile))
    out = pl.pallas_call(
        _inproj_kernel,
        out_shape=out_shape,
        grid=(nb, s // ROW_TILE),
        in_specs=[pl.BlockSpec((1, ROW_TILE, D_MODEL), tile),
                  pl.BlockSpec((1, 1, D_MODEL), lambda b, j: (row(b, j), 0, 0)),
                  pl.BlockSpec((1, 1, D_MODEL), lambda b, j: (row(b, j), 0, 1)),
                  pl.BlockSpec((D_MODEL, n), lambda b, j: (0, 0))],
        out_specs=out_specs,
        compiler_params=_cparams(("parallel", "parallel")),
        name="inproj",
    )(h, mods_l, mods_l, w)
    return out if f32_tail else out[0]


def _s5_tables(lam_re, lam_im, log_dt, b_re, b_im, c_re, c_im, d_skip):
    L, H, P, G = S5_L, S5_H, S5_P, S5_G
    lam_re, lam_im = lam_re.astype(F32), lam_im.astype(F32)
    dt = jnp.exp(log_dt.astype(F32))[..., None]
    mag = jnp.exp(lam_re * dt)
    a_re = mag * jnp.cos(lam_im * dt)
    a_im = mag * jnp.sin(lam_im * dt)
    den = lam_re * lam_re + lam_im * lam_im
    f_re = ((a_re - 1.0) * lam_re + a_im * lam_im) / den
    f_im = (a_im * lam_re - (a_re - 1.0) * lam_im) / den
    b_re, b_im = b_re.astype(F32), b_im.astype(F32)
    bb_re = f_re[..., None] * b_re - f_im[..., None] * b_im
    bb_im = f_re[..., None] * b_im + f_im[..., None] * b_re
    tau = jnp.arange(L + 1, dtype=F32)[:, None, None, None]
    mag_t = jnp.exp(lam_re[None] * dt[None] * tau)
    ang_t = lam_im[None] * dt[None] * tau
    p_re = mag_t * jnp.cos(ang_t)
    p_im = mag_t * jnp.sin(ang_t)
    c_re, c_im = c_re.astype(F32), c_im.astype(F32)
    e_re = c_re[None] * p_re[:, :, :, None, :] - c_im[None] * p_im[:, :, :, None, :]
    e_im = c_re[None] * p_im[:, :, :, None, :] + c_im[None] * p_re[:, :, :, None, :]
    k = (jnp.einsum('tdgjp,dgpi->tdgji', e_re, bb_re, precision=HIGHEST)
         - jnp.einsum('tdgjp,dgpi->tdgji', e_im, bb_im, precision=HIGHEST))
    s_i, t_i = np.meshgrid(np.arange(L), np.arange(L), indexing='ij')
    kf = k[np.clip(t_i - s_i, 0, L), 0] * jnp.asarray(t_i >= s_i, F32)[..., None, None, None]
    kb = k[np.clip(s_i - t_i, 0, L), 1] * jnp.asarray(s_i >= t_i, F32)[..., None, None, None]
    skip = (jnp.asarray(s_i == t_i, F32)[..., None, None, None]
            * (jnp.eye(H, dtype=F32)[None] * d_skip.astype(F32).reshape(G, 1, H))[None, None])
    intra = (kf + kb + skip).transpose(2, 0, 4, 1, 3).reshape(G, L * H, L * H)

    def state_in(d, pw):
        pr, pi = p_re[pw, d], p_im[pw, d]
        re = pr[..., None] * bb_re[d][None] - pi[..., None] * bb_im[d][None]
        im = pr[..., None] * bb_im[d][None] + pi[..., None] * bb_re[d][None]
        return jnp.concatenate([re, im], axis=2).transpose(1, 0, 3, 2).reshape(G, L * H, 2 * P)

    sf = state_in(0, np.arange(L)[::-1].copy())
    sb = state_in(1, np.arange(L))
    w1 = jnp.concatenate([intra, sf, sb], axis=-1)

    def state_out(d, pw):
        er, ei = e_re[pw, d], e_im[pw, d]
        return jnp.concatenate([er, -ei], axis=-1).transpose(1, 3, 0, 2).reshape(G, 2 * P, L * H)

    w2 = jnp.concatenate([state_out(0, np.arange(1, L + 1)),
                          state_out(1, np.arange(L, 0, -1))], axis=1)
    ar, ai = p_re[L], p_im[L]
    rows = []
    for d in range(2):
        rows += [jnp.concatenate([ar[d], ar[d]], -1), jnp.concatenate([-ai[d], ai[d]], -1),
                 jnp.concatenate([ai[d], -ai[d]], -1)]
    rows += [jnp.zeros_like(rows[0])] * 2
    al = jnp.stack(rows, axis=1)
    return w1.astype(BF16), w2.astype(BF16), al


def _s5_kernel(x_ref, w1_ref, w2_ref, al_ref, y_ref, xw_ref, sw_ref, hh_ref, *, nch, nctx):
    lh = S5_L * S5_H
    p2 = 2 * S5_P
    xw_ref[...] = jnp.dot(x_ref[0], w1_ref[0], preferred_element_type=F32)
    sw_ref[:, 0:p2] = pltpu.roll(xw_ref[:, lh:lh + p2], S5_P, 1)
    sw_ref[:, p2:2 * p2] = pltpu.roll(xw_ref[:, lh + p2:lh + 2 * p2], S5_P, 1)
    al = al_ref[0]
    a1f, a2f, a3f, a1b, a2b, a3b = [al[i:i + 1] for i in range(6)]

    def body(k, carry):
        vf, wf, vb, wb = carry
        kb = jnp.where(k < nctx, nctx - 1 - k, nch + nctx - 1 - k)
        rf = pl.ds(pl.multiple_of(k * 8, 8), 8)
        rb = pl.ds(pl.multiple_of(kb * 8, 8), 8)
        hh_ref[rf, 0:p2] = vf
        hh_ref[rb, p2:2 * p2] = vb
        s_f = xw_ref[rf, lh:lh + p2]
        s_fs = sw_ref[rf, 0:p2]
        s_b = xw_ref[rb, lh + p2:lh + 2 * p2]
        s_bs = sw_ref[rb, p2:2 * p2]
        return (vf * a1f + wf * a2f + s_f, wf * a1f + vf * a3f + s_fs,
                vb * a1b + wb * a2b + s_b, wb * a1b + vb * a3b + s_bs)

    zero = jnp.zeros((8, p2), F32)
    lax.fori_loop(0, nch, body, (zero,) * 4)
    y_ref[0] = (xw_ref[:, 0:lh] + _bdot(hh_ref[...], w2_ref[0])).astype(BF16)


def _s5(u, tables):
    nb, s, _ = u.shape
    nch = s // S5_L
    lh = S5_L * S5_H
    r = nch * 8
    x = u.astype(BF16).reshape(nb, nch, S5_L, S5_G, S5_H).transpose(3, 1, 0, 2, 4)
    x = jnp.pad(x, ((0, 0), (0, 0), (0, 8 - nb), (0, 0), (0, 0))).reshape(S5_G, r, lh)
    w1, w2, al = tables
    y = pl.pallas_call(
        functools.partial(_s5_kernel, nch=nch, nctx=CTX // S5_L),
        out_shape=jax.ShapeDtypeStruct((S5_G, r, lh), BF16),
        grid=(S5_G,),
        in_specs=[pl.BlockSpec((1, r, lh), lambda g: (g, 0, 0)),
                  pl.BlockSpec((1, lh, 2 * lh), lambda g: (g, 0, 0)),
                  pl.BlockSpec((1, lh, lh), lambda g: (g, 0, 0)),
                  pl.BlockSpec((1, 8, 2 * S5_P), lambda g: (g, 0, 0))],
        out_specs=pl.BlockSpec((1, r, lh), lambda g: (g, 0, 0)),
        scratch_shapes=[pltpu.VMEM((r, 2 * lh), F32), pltpu.VMEM((r, lh), F32), pltpu.VMEM((r, lh), F32)],
        compiler_params=_cparams(("parallel",)),
        name="s5",
    )(x, w1, w2, al)
    y = y.reshape(S5_G, nch, 8, S5_L, S5_H)[:, :, :nb]
    return y.transpose(2, 1, 3, 0, 4).reshape(nb, s, HALF)


def _na_bias_tables(rpb, rows):
    w = GRID_W
    nh, ndr, ndc = rpb.shape
    v = jnp.pad(rpb.astype(F32), ((0, 0), (0, 0), (w - NA_WC, 2 * w - ndc - (w - NA_WC))))
    skew = jnp.tile(v, (1, 1, w))[:, :, :w * (2 * w - 1)].reshape(nh, ndr, w, 2 * w - 1)
    tc = skew[..., w - 1:]
    valid, dr = [], []
    for r0 in (0, NA_QR, rows - NA_QR):
        k0 = int(np.clip(r0 - NA_WR // 2, 0, rows - NA_KR))
        rq = r0 + np.arange(NA_QR)[:, None, None, None]
        wq = np.arange(w)[None, :, None, None]
        kr = k0 + np.arange(NA_KR)[None, None, :, None]
        kc = np.arange(w)[None, None, None, :]
        rs = np.clip(rq - NA_WR // 2, 0, rows - NA_WR)
        cs = np.clip(wq - NA_WC // 2, 0, w - NA_WC)
        valid.append((kr >= rs) & (kr < rs + NA_WR) & (kc >= cs) & (kc < cs + NA_WC))
        dr.append(np.clip(kr - rq + NA_WR - 1, 0, ndr - 1)[:, 0, :, 0])
    dr = np.stack(dr).reshape(-1)
    bias = jnp.take(tc, jnp.asarray(dr), axis=1)
    bias = bias.reshape(nh, 3, NA_QR, NA_KR, w, w).transpose(0, 1, 2, 4, 3, 5)
    bias = jnp.where(jnp.asarray(np.stack(valid))[None], bias, NEG)
    return bias.reshape(nh, 3, NA_QR * w, NA_KR * w)


def _natten_kernel(q_ref, k_ref, v_ref, bias_ref, o_ref, qb_ref, kb_ref, vb_ref, *, rows):
    nq = NA_QR * GRID_W
    nk = NA_KR * GRID_W
    nblk = rows // NA_QR
    scale = NA_DH ** -0.5
    qb_ref[...] = (q_ref[0] * scale).astype(BF16)
    kb_ref[...] = k_ref[0].astype(BF16)
    vb_ref[...] = v_ref[0].astype(BF16)
    lane = lax.broadcasted_iota(jnp.int32, (1, 2 * NA_DH), 1)
    head_mask = [lane < NA_DH, lane >= NA_DH]
    kc = kb_ref[0:CTX, :]
    vc = vb_ref[0:CTX, :]

    def attend(q, extra):
        outs = []
        for h in range(2):
            qm = jnp.where(head_mask[h], q, jnp.zeros_like(q))
            s_c = _bdot_nt(qm, kc)
            m = jnp.max(s_c, axis=-1, keepdims=True)
            if extra is not None:
                k_loc, v_loc, bias = extra
                s_l = _bdot_nt(qm, k_loc) + bias[h]
                m = jnp.maximum(m, jnp.max(s_l, axis=-1, keepdims=True))
                p_l = jnp.exp(s_l - m)
            p_c = jnp.exp(s_c - m)
            den = jnp.sum(p_c, axis=-1, keepdims=True)
            acc = _bdot(p_c, vc)
            if extra is not None:
                den = den + jnp.sum(p_l, axis=-1, keepdims=True)
                acc = acc + _bdot(p_l, v_loc)
            outs.append(acc / den)
        return jnp.where(head_mask[0], outs[0], outs[1])

    o_ref[0, 0:CTX, :] = attend(qb_ref[0:CTX, :], None)

    def body(i, carry):
        r0 = i * NA_QR
        k0 = jnp.clip(r0 - NA_WR // 2, 0, rows - NA_KR)
        pat = jnp.where(i == 0, 0, jnp.where(i == nblk - 1, 2, 1))
        qs = pl.multiple_of(CTX + r0 * GRID_W, GRID_W)
        ks = pl.multiple_of(CTX + k0 * GRID_W, GRID_W)
        q = qb_ref[pl.ds(qs, nq), :]
        k_loc = kb_ref[pl.ds(ks, nk), :]
        v_loc = vb_ref[pl.ds(ks, nk), :]
        bias = [bias_ref[h, pat] for h in range(2)]
        o_ref[0, pl.ds(qs, nq), :] = attend(q, (k_loc, v_loc, bias))
        return carry

    lax.fori_loop(0, nblk, body, 0)


def _natten(a, bias):
    nb, s, _ = a.shape
    rows = (s - CTX) // GRID_W
    pw = 2 * NA_DH
    nq, nk = NA_QR * GRID_W, NA_KR * GRID_W
    hb = HALF // pw
    return pl.pallas_call(
        functools.partial(_natten_kernel, rows=rows),
        out_shape=jax.ShapeDtypeStruct((nb, s, HALF), F32),
        grid=(nb, hb),
        in_specs=[pl.BlockSpec((1, s, pw), lambda b, h: (b, 0, hb + h)),
                  pl.BlockSpec((1, s, pw), lambda b, h: (b, 0, 2 * hb + h)),
                  pl.BlockSpec((1, s, pw), lambda b, h: (b, 0, 3 * hb + h)),
                  pl.BlockSpec((2, 3, nq, nk), lambda b, h: (h, 0, 0, 0))],
        out_specs=pl.BlockSpec((1, s, pw), lambda b, h: (b, 0, h)),
        scratch_shapes=[pltpu.VMEM((s, pw), BF16)] * 3,
        compiler_params=_cparams(("parallel", "parallel")),
        name="natten",
    )(a, a, a, bias)


def _gelu_tanh(x):
    return 0.5 * x * (1.0 + jnp.tanh(math.sqrt(2.0 / math.pi) * (x + 0.044715 * (x * x * x))))


def _out_kernel(ya_ref, yb_ref, h_ref, g_ref, e0_ref, e1_ref, wo_ref, lng_ref, lnb_ref, o_ref, *, glu):
    if glu:
        z = _bdot(_gelu_tanh(ya_ref[0].astype(F32)), e0_ref[...]) + e1_ref[...]
        ya = z[:, :HALF] * jax.nn.sigmoid(z[:, HALF:])
    else:
        parts = []
        for hd in range(GLA_HEADS):
            cols = slice(hd * GLA_DV, (hd + 1) * GLA_DV)
            o = ya_ref[0, :, cols]
            o = o * lax.rsqrt(jnp.mean(o * o, axis=-1, keepdims=True) + EPS) * e1_ref[:, cols]
            parts.append(o * _silu(e0_ref[0, :, cols].astype(F32)))
        ya = jnp.concatenate(parts, axis=1)
    y = _bdot(ya, wo_ref[0:HALF, :]) + _bdot(yb_ref[0], wo_ref[HALF:, :])
    o_ref[0] = _layer_norm(ALPHA * h_ref[0] + g_ref[0] * y, lng_ref[...], lnb_ref[...])


def _outproj(ya, yb, h, mods_l, w_out, ln_g, ln_b, *, glu_w=None, glu_b=None, proj=None, norm_g=None):
    nb, s, _ = h.shape
    row = _mod_row(nb)
    glu = glu_w is not None
    tile = lambda b, j: (b, j, 0)
    const = lambda b, j: (0, 0)
    if glu:
        extra_specs = [pl.BlockSpec((HALF, D_MODEL), const), pl.BlockSpec((1, D_MODEL), const)]
        extra = [glu_w.astype(BF16), glu_b.astype(F32).reshape(1, D_MODEL)]
    else:
        rblk = (2 * GLA_HEADS * GLA_DK + HALF) // HALF
        extra_specs = [pl.BlockSpec((1, ROW_TILE, HALF), lambda b, j: (b, j, rblk)),
                       pl.BlockSpec((1, HALF), const)]
        extra = [proj, norm_g.astype(F32).reshape(1, HALF)]
    in_specs = [pl.BlockSpec((1, ROW_TILE, HALF), tile),
                pl.BlockSpec((1, ROW_TILE, HALF), tile),
                pl.BlockSpec((1, ROW_TILE, D_MODEL), tile),
                pl.BlockSpec((1, 1, D_MODEL), lambda b, j: (row(b, j), 0, 2))] + extra_specs + [
                pl.BlockSpec((D_MODEL, D_MODEL), const), pl.BlockSpec((1, D_MODEL), const),
                pl.BlockSpec((1, D_MODEL), const)]
    return pl.pallas_call(
        functools.partial(_out_kernel, glu=glu),
        out_shape=jax.ShapeDtypeStruct((nb, s, D_MODEL), F32),
        grid=(nb, s // ROW_TILE),
        in_specs=in_specs,
        out_specs=pl.BlockSpec((1, ROW_TILE, D_MODEL), tile),
        compiler_params=_cparams(("parallel", "parallel")),
        name="outproj_glu" if glu else "outproj_gate",
    )(ya, yb, h, mods_l, *extra, w_out.astype(BF16), ln_g.astype(F32).reshape(1, D_MODEL),
      ln_b.astype(F32).reshape(1, D_MODEL))


def _log_sigmoid(x):
    return jnp.minimum(x, 0.0) - jnp.log1p(jnp.exp(-jnp.abs(x)))


def _gla_kernel(q_ref, k_ref, v_ref, z_ref, wg_ref, bg_ref, o_ref, gf_ref, gb_ref, ob_ref, st_ref, *, nch):
    c = GLA_CHUNK
    dk2 = 2 * GLA_DK
    nctx = CTX // c
    z = z_ref[0]
    gf_ref[...] = _log_sigmoid(_hdot(z, wg_ref[0, 0]) + bg_ref[0, 0]) * (1.0 / GLA_TAU)
    gb_ref[...] = _log_sigmoid(_hdot(z, wg_ref[1, 0]) + bg_ref[1, 0]) * (1.0 / GLA_TAU)
    ri = lax.broadcasted_iota(jnp.int32, (c, c), 0)
    ci = lax.broadcasted_iota(jnp.int32, (c, c), 1)
    lane = lax.broadcasted_iota(jnp.int32, (1, dk2), 1)
    head_mask = [lane < GLA_DK, lane >= GLA_DK]
    scale = GLA_DK ** -0.5

    st_ref[...] = jnp.zeros_like(st_ref)

    def chunk(direction, n):
        g_ref = gf_ref if direction == 0 else gb_ref
        keep = (ci <= ri) if direction == 0 else (ci >= ri)
        tri = jnp.where(keep, 1.0, 0.0).astype(BF16)
        rs = pl.ds(pl.multiple_of(n * c, c), c)
        g = g_ref[rs, :]
        g_hi = g.astype(BF16)
        g_lo = (g - g_hi.astype(F32)).astype(BF16)
        gc = (jnp.dot(tri, g_hi, preferred_element_type=F32)
              + jnp.dot(tri, g_lo, preferred_element_type=F32))
        gend = gc[c - 1:c, :] if direction == 0 else gc[0:1, :]
        q = q_ref[0, rs, :] * scale
        k = k_ref[0, rs, :]
        qd = q * jnp.exp(gc)
        kd = k * jnp.exp(-gc)
        kdec = k * jnp.exp(gend - gc)
        dcol = jnp.exp(jnp.broadcast_to(gend, (dk2, dk2)).T)
        dcol = jnp.concatenate([dcol, dcol], axis=1)
        qs = jnp.concatenate([jnp.where(head_mask[0], qd, 0.0), jnp.where(head_mask[1], qd, 0.0)],
                             axis=0).astype(BF16)
        att = jnp.where(jnp.concatenate([keep, keep], axis=0), _bdot_nt(qs, kd), 0.0)
        v = v_ref[0, rs, :].astype(BF16)
        st = st_ref[direction]
        o2 = _bdot(att, v) + _bdot(qs, st)
        st_ref[direction] = dcol * st + jnp.dot(kdec.T.astype(BF16), v, preferred_element_type=F32)
        o = jnp.concatenate([o2[0:c, 0:GLA_DV], o2[c:2 * c, GLA_DV:2 * GLA_DV]], axis=1)
        if direction == 0:
            o_ref[0, rs, :] = o
        else:
            ob_ref[rs, :] = o

    def body(i, carry):
        chunk(0, i)
        chunk(1, jnp.where(i < nctx, nctx - 1 - i, nch + nctx - 1 - i))
        return carry

    lax.fori_loop(0, nch, body, 0, unroll=4)
    o_ref[0] += ob_ref[...]


def _gla(a, z, wg, bg):
    nb, s, _ = a.shape
    nch = s // GLA_CHUNK
    hp = GLA_HEADS // 2
    dk2, dv2 = 2 * GLA_DK, 2 * GLA_DV
    return pl.pallas_call(
        functools.partial(_gla_kernel, nch=nch),
        out_shape=jax.ShapeDtypeStruct((nb, s, HALF), F32),
        grid=(nb, hp),
        in_specs=[pl.BlockSpec((1, s, dk2), lambda b, h: (b, 0, h)),
                  pl.BlockSpec((1, s, dk2), lambda b, h: (b, 0, hp + h)),
                  pl.BlockSpec((1, s, dv2), lambda b, h: (b, 0, hp + h)),
                  pl.BlockSpec((1, s, 128), lambda b, h: (b, 0, 0)),
                  pl.BlockSpec((2, 1, 128, dk2), lambda b, h: (0, h, 0, 0)),
                  pl.BlockSpec((2, 1, 1, dk2), lambda b, h: (0, h, 0, 0))],
        out_specs=pl.BlockSpec((1, s, dv2), lambda b, h: (b, 0, h)),
        scratch_shapes=[pltpu.VMEM((s, dk2), F32), pltpu.VMEM((s, dk2), F32), pltpu.VMEM((s, dv2), F32),
                        pltpu.VMEM((2, dk2, dv2), F32)],
        compiler_params=_cparams(("parallel", "parallel")),
        name="gla",
    )(a, a, a, z, wg, bg)


def _rope_tables(s):
    t = np.arange(s - CTX)
    n = DIFF_DH // 4
    inv = ROPE_BASE ** (-np.arange(n, dtype=np.float64) / n)
    ang_r = (t // GRID_W)[:, None] * inv[None]
    ang_c = (t % GRID_W)[:, None] * inv[None]
    ang = np.concatenate([ang_r, ang_r, ang_c, ang_c], axis=1)
    ang = np.concatenate([ang, ang], axis=1)
    cos = np.concatenate([np.ones((CTX, 128)), np.cos(ang)], axis=0)
    sin = np.concatenate([np.zeros((CTX, 128)), np.sin(ang)], axis=0)
    first = ((np.arange(128) // n) % 2 == 0)[None]
    sin_up = np.where(first, -sin, 0.0)
    sin_dn = np.where(first, 0.0, sin)
    f = lambda z: jnp.asarray(z, F32)
    return f(cos), f(sin_up), f(sin_dn)


def _rope_kernel(q_ref, k_ref, cos_ref, su_ref, sd_ref, qo_ref, ko_ref):
    n = DIFF_DH // 4
    cos = jnp.concatenate([cos_ref[...]] * DIFF_HEADS, axis=1)
    su = jnp.concatenate([su_ref[...]] * DIFF_HEADS, axis=1)
    sd = jnp.concatenate([sd_ref[...]] * DIFF_HEADS, axis=1)

    def rot(x):
        w = x.shape[1]
        return x * cos + pltpu.roll(x, w - n, 1) * su + pltpu.roll(x, n, 1) * sd

    qo_ref[0] = rot(q_ref[0].astype(F32) * (DIFF_DH ** -0.5 * LOG2E)).astype(BF16)
    ko_ref[0] = rot(k_ref[0].astype(F32)).astype(BF16)


def _rope(a, tables):
    nb, s, _ = a.shape
    qblk = (2 * GLA_HEADS * GLA_DK + 2 * HALF) // HALF
    tile = lambda b, j: (b, j, 0)
    tab = pl.BlockSpec((ROW_TILE, 128), lambda b, j: (j, 0))
    return pl.pallas_call(
        _rope_kernel,
        out_shape=[jax.ShapeDtypeStruct((nb, s, HALF), BF16)] * 2,
        grid=(nb, s // ROW_TILE),
        in_specs=[pl.BlockSpec((1, ROW_TILE, HALF), lambda b, j: (b, j, qblk)),
                  pl.BlockSpec((1, ROW_TILE, HALF), lambda b, j: (b, j, qblk + 1)),
                  tab, tab, tab],
        out_specs=[pl.BlockSpec((1, ROW_TILE, HALF), tile)] * 2,
        compiler_params=_cparams(("parallel", "parallel")),
        name="rope",
    )(a, a, *tables)


def _diff_kernel(q_ref, k_ref, v_ref, lam_ref, ng_ref, o_ref, vb_ref, *, lam_init):
    j = pl.program_id(2)
    lane = lax.broadcasted_iota(jnp.int32, (1, 2 * DIFF_DH), 1)
    map_mask = [lane < DIFF_DH, lane >= DIFF_DH]
    lv = lam_ref[...]
    lam = (jnp.exp(jnp.sum(lv[0:1] * lv[1:2], axis=-1, keepdims=True))
           - jnp.exp(jnp.sum(lv[2:3] * lv[3:4], axis=-1, keepdims=True)) + lam_init)

    @pl.when(j == 0)
    def _():
        vb_ref[...] = v_ref[0].astype(BF16)

    def attend(nkeys):
        q = q_ref[0]
        k = k_ref[0, 0:nkeys, :]
        pb, inv = [], []
        for m in range(2):
            s = _bdot_nt(jnp.where(map_mask[m], q, jnp.zeros_like(q)), k)
            p = jnp.exp2(s - jnp.max(s, axis=-1, keepdims=True)).astype(BF16)
            pb.append(p)
            inv.append(1.0 / jnp.sum(p.astype(F32), axis=-1, keepdims=True))
        w = (pb[0].astype(F32) * inv[0] - pb[1].astype(F32) * (lam * inv[1])).astype(BF16)
        o = jnp.dot(w, vb_ref[0:nkeys, :], preferred_element_type=F32)
        o = o * lax.rsqrt(jnp.mean(o * o, axis=-1, keepdims=True) + EPS) * ng_ref[...]
        o_ref[0] = o * (1.0 - lam_init)

    @pl.when(j == 0)
    def _():
        attend(CTX)

    @pl.when(j > 0)
    def _():
        attend(k_ref.shape[1])


def _diff(qr, kr, a, lam_vecs, ng, lam_init):
    nb, s, _ = qr.shape
    dw = 2 * DIFF_DH
    vblk = (2 * GLA_HEADS * GLA_DK + 4 * HALF) // dw
    return pl.pallas_call(
        functools.partial(_diff_kernel, lam_init=lam_init),
        out_shape=jax.ShapeDtypeStruct((nb, s, HALF), F32),
        grid=(nb, DIFF_HEADS, s // ROW_TILE),
        in_specs=[pl.BlockSpec((1, ROW_TILE, dw), lambda b, h, j: (b, j, h)),
                  pl.BlockSpec((1, s, dw), lambda b, h, j: (b, 0, h)),
                  pl.BlockSpec((1, s, DIFF_DV), lambda b, h, j: (b, 0, vblk + h)),
                  pl.BlockSpec((8, DIFF_DH), lambda b, h, j: (0, 0)),
                  pl.BlockSpec((1, DIFF_DV), lambda b, h, j: (0, 0))],
        out_specs=pl.BlockSpec((1, ROW_TILE, DIFF_DV), lambda b, h, j: (b, j, h)),
        scratch_shapes=[pltpu.VMEM((s, DIFF_DV), BF16)],
        compiler_params=_cparams(("parallel", "parallel", "arbitrary")),
        name="diff_attn",
    )(qr, kr, a, lam_vecs, ng)


def _moe_kernel(h_ref, m0_ref, m1_ref, mc_ref, wr_ref, br_ref, wg_ref, wu_ref, wd_ref, lng_ref, lnb_ref, o_ref,
                as_ref, cs_ref, ps_ref, ys_ref, win_ref, *, seq, win):
    t = pl.program_id(0)
    e = pl.program_id(1)
    tm = h_ref.shape[0]
    lane = lax.broadcasted_iota(jnp.int32, (1, 128), 1).astype(F32)
    d = D_MODEL

    def mod(col):
        b0 = (t * tm) // seq
        grow = t * tm + lax.broadcasted_iota(jnp.int32, (tm, 1), 0)
        in_next = grow >= (b0 + 1) * seq
        local = grow - jnp.where(in_next, b0 + 1, b0) * seq
        cols = slice(col * d, (col + 1) * d)
        return jnp.where(local < CTX, mc_ref[0, :, cols],
                         jnp.where(in_next, m1_ref[0, :, cols], m0_ref[0, :, cols]))

    @pl.when(e == 0)
    def _():
        a = h_ref[...] * (1.0 + mod(1)) + mod(0)
        logits = _hdot(a, wr_ref[...]) + br_ref[...]
        is_g = lane < N_GROUPS
        lg = jnp.where(is_g, logits, NEG)
        mg = jnp.max(lg, axis=-1, keepdims=True)
        g_sel = jnp.min(jnp.where(lg == mg, lane, 128.0), axis=-1, keepdims=True)
        p_grp = 1.0 / jnp.sum(jnp.where(is_g, jnp.exp(lg - mg), 0.0), axis=-1, keepdims=True)
        lo = N_GROUPS + g_sel * EPG
        in_grp = (lane >= lo) & (lane < lo + EPG)
        le = jnp.where(in_grp, logits, NEG)
        v1 = jnp.max(le, axis=-1, keepdims=True)
        i1 = jnp.min(jnp.where(le == v1, lane, 128.0), axis=-1, keepdims=True)
        le2 = jnp.where(lane == i1, NEG, le)
        v2 = jnp.max(le2, axis=-1, keepdims=True)
        i2 = jnp.min(jnp.where(le2 == v2, lane, 128.0), axis=-1, keepdims=True)
        e2 = jnp.exp(v2 - v1)
        w1 = p_grp / (1.0 + e2)
        w2 = p_grp * e2 / (1.0 + e2)
        comb = jnp.where(lane == i1, w1, jnp.where(lane == i2, w2, 0.0))

        onehot = jnp.where(lane == g_sel, 1.0, 0.0).astype(BF16)
        ri = lax.broadcasted_iota(jnp.int32, (tm, tm), 0)
        ci = lax.broadcasted_iota(jnp.int32, (tm, tm), 1)
        ranks = jnp.dot(jnp.where(ci <= ri, 1.0, 0.0).astype(BF16), onehot, preferred_element_type=F32)
        cnt = ranks[tm - 1:tm, :]
        n = [jnp.sum(jnp.where(lane == float(g), cnt, 0.0), axis=-1, keepdims=True) for g in range(N_GROUPS)]
        off = [jnp.zeros_like(n[0]), n[0], n[0] + n[1], n[0] + n[1] + n[2]]
        off_vec = sum(jnp.where(lane == float(g), off[g], 0.0) for g in range(1, N_GROUPS))
        dest = jnp.sum(jnp.where(lane == g_sel, ranks + off_vec, 0.0), axis=-1, keepdims=True) - 1.0

        pos = lax.broadcasted_iota(jnp.int32, (1, tm), 1).astype(F32)
        ps_ref[...] = jnp.where(dest == pos, 1.0, 0.0).astype(BF16)
        hi = jnp.floor(dest * (1.0 / 32.0))
        digits = jnp.where(lane == 0.0, hi, jnp.where(lane == 1.0, dest - 32.0 * hi, 0.0)).astype(BF16)
        ri8 = lax.broadcasted_iota(jnp.int32, (8, 128), 0)
        ci8 = lax.broadcasted_iota(jnp.int32, (8, 128), 1)
        dig_t = _bdot_nt(jnp.where(ri8 == ci8, 1.0, 0.0), digits)
        dest_row = 32.0 * dig_t[0:1, :] + dig_t[1:2, :]
        rows = lax.broadcasted_iota(jnp.int32, (tm, 1), 0).astype(F32)
        pg = jnp.where(dest_row == rows, 1.0, 0.0).astype(BF16)
        as_ref[...] = jnp.dot(pg, a.astype(BF16), preferred_element_type=F32).astype(BF16)
        c_hi = comb.astype(BF16)
        c_lo = (comb - c_hi.astype(F32)).astype(BF16)
        packed = jnp.where(lane < 64.0, c_hi.astype(F32), pltpu.roll(c_lo.astype(F32), 64, 1)).astype(BF16)
        cg = jnp.dot(pg, packed, preferred_element_type=F32)
        cs_ref[...] = cg + pltpu.roll(cg, 64, 1)
        ys_ref[...] = jnp.zeros_like(ys_ref)
        for g in range(N_GROUPS):
            start = jnp.minimum((jnp.sum(off[g]).astype(jnp.int32) // 16) * 16, tm - win)
            end = jnp.sum(off[g] + n[g]).astype(jnp.int32)
            win_ref[g] = start
            win_ref[N_GROUPS + g] = jnp.where(end > start, (end - start + win - 1) // win, 0)

    g = (e * MOE_EB) // EPG
    start = win_ref[g]

    def window(w, carry):
        lo = start + w * win
        rs = pl.ds(pl.multiple_of(jnp.minimum(lo, tm - win), 16), win)
        x = as_ref[rs, :]
        row = jnp.minimum(lo, tm - win) + lax.broadcasted_iota(jnp.int32, (win, 1), 0)
        cs = jnp.where(row >= lo, cs_ref[rs, :], 0.0)
        upd = None
        for j in range(MOE_EB):
            sel = lane == (e * MOE_EB + j + N_GROUPS).astype(F32)
            cw = jnp.sum(jnp.where(sel, cs, 0.0), axis=-1, keepdims=True)
            hg = jnp.dot(x, wg_ref[j], preferred_element_type=F32)
            hu = jnp.dot(x, wu_ref[j], preferred_element_type=F32)
            dn = jnp.dot((_silu(hg) * hu * cw).astype(BF16), wd_ref[j], preferred_element_type=F32)
            upd = dn if upd is None else upd + dn
        ys_ref[rs, :] += upd
        return carry

    lax.fori_loop(0, win_ref[N_GROUPS + g], window, 0)

    @pl.when(e == N_EXPERTS // MOE_EB - 1)
    def _():
        y = jnp.dot(ps_ref[...], ys_ref[...].astype(BF16), preferred_element_type=F32)
        o_ref[...] = _layer_norm(ALPHA * h_ref[...] + mod(2) * y, lng_ref[...], lnb_ref[...])


def _moe(h, mods_l, w_router, b_router, w_gate, w_up, w_down, ln_g, ln_b):
    nb, s, d = h.shape
    rows = nb * s
    tm = 1024 if rows % 1024 == 0 else 256
    win = tm // N_GROUPS + tm // 16
    nt = rows // tm
    const = lambda t, e: (0, 0)
    b0 = lambda t: (t * tm) // s
    neb = N_EXPERTS // MOE_EB
    w_gate, w_up, w_down = w_gate.astype(BF16), w_up.astype(BF16), w_down.astype(BF16)
    out = pl.pallas_call(
        functools.partial(_moe_kernel, seq=s, win=win),
        out_shape=jax.ShapeDtypeStruct((rows, d), F32),
        grid=(nt, neb),
        in_specs=[pl.BlockSpec((tm, d), lambda t, e: (t, 0)),
                  pl.BlockSpec((1, 1, 3 * d), lambda t, e: (b0(t), 0, 1)),
                  pl.BlockSpec((1, 1, 3 * d), lambda t, e: (jnp.minimum(b0(t) + 1, nb - 1), 0, 1)),
                  pl.BlockSpec((1, 1, 3 * d), lambda t, e: (nb, 0, 1)),
                  pl.BlockSpec((d, 128), const),
                  pl.BlockSpec((1, 128), const),
                  pl.BlockSpec((MOE_EB, d, D_EXPERT), lambda t, e: (e, 0, 0)),
                  pl.BlockSpec((MOE_EB, d, D_EXPERT), lambda t, e: (e, 0, 0)),
                  pl.BlockSpec((MOE_EB, D_EXPERT, d), lambda t, e: (e, 0, 0)),
                  pl.BlockSpec((1, d), const),
                  pl.BlockSpec((1, d), const)],
        out_specs=pl.BlockSpec((tm, d), lambda t, e: (t, 0)),
        scratch_shapes=[pltpu.VMEM((tm, d), BF16), pltpu.VMEM((tm, 128), F32), pltpu.VMEM((tm, tm), BF16),
                        pltpu.VMEM((tm, d), F32), pltpu.SMEM((2 * N_GROUPS,), jnp.int32)],
        compiler_params=_cparams(("parallel", "arbitrary")),
        name="moe",
    )(h.reshape(rows, d), mods_l, mods_l, mods_l, w_router, b_router, w_gate, w_up, w_down,
      ln_g.astype(F32).reshape(1, d), ln_b.astype(F32).reshape(1, d))
    return out.reshape(nb, s, d)


def _pad_lanes(x, n):
    return jnp.pad(x, [(0, 0)] * (x.ndim - 1) + [(0, n - x.shape[-1])])


def kernel(x, c, ctx, c_ctx, mod_w, mod_b, ln_g, ln_b, even_w_in, even_w_out, s5_lam_re, s5_lam_im, s5_log_dt, s5_b_re, s5_b_im, s5_c_re, s5_c_im, s5_d, s5_glu_w, s5_glu_b, na_rpb, odd_w_in, odd_w_out, gla_gate_w, gla_gate_b, gla_norm_g, diff_lq1, diff_lk1, diff_lq2, diff_lk2, diff_norm_g, moe_w_grp, moe_b_grp, moe_w_exp, moe_b_exp, moe_w_gate, moe_w_up, moe_w_down):
    nb, t, d = x.shape
    assert d == D_MODEL and ctx.shape[1] == CTX and nb < 8 and t % (NA_QR * GRID_W) == 0
    s = CTX + t
    rows = t // GRID_W
    h = jnp.concatenate([ctx, x], axis=1).astype(F32)
    cc = jnp.zeros((8, d), F32).at[:nb].set(c.astype(F32)).at[nb].set(c_ctx.astype(F32))
    mods = _mods(cc, mod_w.astype(F32), mod_b.astype(F32)).reshape(DEPTH, 8, 1, 6 * d)
    rope_tabs = _rope_tables(s)
    for i in range(DEPTH):
        j = i // 2
        mods_l = mods[i]
        if i % 2 == 0:
            a = _inproj(h, mods_l, even_w_in[j].astype(BF16))
            tables = _s5_tables(s5_lam_re[j], s5_lam_im[j], s5_log_dt[j], s5_b_re[j], s5_b_im[j],
                                s5_c_re[j], s5_c_im[j], s5_d[j])
            y_a = _s5(a[..., :HALF], tables)
            y_b = _natten(a, _na_bias_tables(na_rpb[j], rows))
            h = _outproj(y_a, y_b, h, mods_l, even_w_out[j], ln_g[i, 0], ln_b[i, 0],
                         glu_w=s5_glu_w[j], glu_b=s5_glu_b[j])
        else:
            lam_init = 0.8 - 0.6 * math.exp(-0.3 * i)
            w = odd_w_in[j]
            nqk = 2 * GLA_HEADS * GLA_DK + 2 * HALF
            w = jnp.concatenate([w[:, :nqk], w[:, nqk + 2 * GLA_RANK:],
                                 _pad_lanes(w[:, nqk:nqk + 2 * GLA_RANK], 128)], axis=1).astype(BF16)
            a, z = _inproj(h, mods_l, w, f32_tail=True)
            gw = gla_gate_w[j].astype(F32)
            wg = jnp.zeros((2, 128, GLA_HEADS * GLA_DK), F32)
            for dd in range(2):
                wg = wg.at[dd, dd * GLA_RANK:(dd + 1) * GLA_RANK].set(gw[dd])
            wg = wg.reshape(2, 128, 2, 2 * GLA_DK).transpose(0, 2, 1, 3)
            bg = gla_gate_b[j].astype(F32).reshape(2, 2, 1, 2 * GLA_DK)
            y_a = _gla(a, z, wg, bg)
            qr, kr = _rope(a, rope_tabs)
            lam_vecs = jnp.zeros((8, DIFF_DH), F32).at[0].set(diff_lq1[j]).at[1].set(diff_lk1[j]) \
                .at[2].set(diff_lq2[j]).at[3].set(diff_lk2[j])
            y_b = _diff(qr, kr, a, lam_vecs, diff_norm_g[j].astype(F32).reshape(1, DIFF_DV), lam_init)
            h = _outproj(y_a, y_b, h, mods_l, odd_w_out[j], ln_g[i, 0], ln_b[i, 0],
                         proj=a, norm_g=gla_norm_g[j])
        w_router = _pad_lanes(jnp.concatenate([moe_w_grp[i], moe_w_exp[i]], axis=1).astype(F32), 128)
        b_router = _pad_lanes(jnp.concatenate([moe_b_grp[i], moe_b_exp[i]]).astype(F32)[None], 128)
        h = _moe(h, mods_l, w_router, b_router, moe_w_gate[i], moe_w_up[i], moe_w_down[i], ln_g[i, 1], ln_b[i, 1])
    return h[:, CTX:].astype(x.dtype)
```

```python
import functools
import math

import numpy as np
import jax
import jax.numpy as jnp
from jax import lax
from jax.experimental import pallas as pl
from jax.experimental.pallas import tpu as pltpu

F32 = jnp.float32
BF16 = jnp.bfloat16
HIGHEST = lax.Precision.HIGHEST

D_MODEL = 1024
DEPTH = 4
GRID_W = 64
CTX = 256
HALF = D_MODEL // 2
S5_H = 16
S5_G = HALF // S5_H
S5_P = 64
S5_L = 16
NA_HEADS = 8
NA_DH = 64
NA_WR = 8
NA_WC = 16
NA_QR = 4
NA_KR = 12
GLA_HEADS = 4
GLA_DK = 64
GLA_DV = 128
GLA_RANK = 16
GLA_TAU = 16.0
GLA_CHUNK = 64
DIFF_HEADS = 4
DIFF_DV = 128
DIFF_DH = 64
N_GROUPS = 4
EPG = 8
N_EXPERTS = N_GROUPS * EPG
D_EXPERT = D_MODEL // 4
MOE_EB = 4
ROPE_BASE = 10000.0
EPS = 1e-5
ALPHA = (2.0 * DEPTH) ** 0.25
EVEN_IN = 4 * HALF
ODD_IN_PAD = 3200
NEG = -1e30
LOG2E = math.log2(math.e)
ROW_TILE = 256
VMEM_LIMIT = 56 * 1024 * 1024


def _cparams(sem):
    return pltpu.CompilerParams(dimension_semantics=sem, vmem_limit_bytes=VMEM_LIMIT)


def _bdot(a, b):
    return jnp.dot(a.astype(BF16), b.astype(BF16), preferred_element_type=F32)


def _bdot_nt(a, b):
    return lax.dot_general(a.astype(BF16), b.astype(BF16), (((1,), (1,)), ((), ())),
                           preferred_element_type=F32)


def _hdot(a, b):
    return jnp.dot(a, b, precision=HIGHEST, preferred_element_type=F32)


def _layer_norm(x, g, b):
    mu = jnp.mean(x, axis=-1, keepdims=True)
    xc = x - mu
    var = jnp.mean(xc * xc, axis=-1, keepdims=True)
    return xc * lax.rsqrt(var + EPS) * g + b


def _silu(x):
    return x * jax.nn.sigmoid(x)


def _mods_kernel(cc_ref, w_ref, b_ref, o_ref):
    o_ref[0] = _hdot(_silu(cc_ref[...]), w_ref[0]) + b_ref[0]


def _mods(cc, mod_w, mod_b):
    nblk = 6
    return pl.pallas_call(
        _mods_kernel,
        out_shape=jax.ShapeDtypeStruct((DEPTH, 8, 6 * D_MODEL), F32),
        grid=(DEPTH, nblk),
        in_specs=[pl.BlockSpec((8, D_MODEL), lambda l, j: (0, 0)),
                  pl.BlockSpec((1, D_MODEL, D_MODEL), lambda l, j: (l, 0, j)),
                  pl.BlockSpec((1, 1, D_MODEL), lambda l, j: (l, 0, j))],
        out_specs=pl.BlockSpec((1, 8, D_MODEL), lambda l, j: (l, 0, j)),
        compiler_params=_cparams(("arbitrary", "arbitrary")),
        name="mods",
    )(cc, mod_w, mod_b.reshape(DEPTH, 1, 6 * D_MODEL))


def _mod_row(nbatch):
    return lambda b, j: jnp.where(j == 0, nbatch, b)


def _inproj_kernel(h_ref, sh_ref, sc_ref, w_ref, o_ref, *tail_ref):
    a = h_ref[0] * (1.0 + sc_ref[0]) + sh_ref[0]
    y = _bdot(a, w_ref[...])
    o_ref[0] = y.astype(BF16)
    if tail_ref:
        tail_ref[0][0] = y[:, y.shape[1] - 128:]


def _inproj(h, mods_l, w, f32_tail=False):
    nb, s, _ = h.shape
    n = w.shape[1]
    row = _mod_row(nb)
    tile = lambda b, j: (b, j, 0)
    out_shape = [jax.ShapeDtypeStruct((nb, s, n), BF16)]
    out_specs = [pl.BlockSpec((1, ROW_TILE, n), tile)]
    if f32_tail:
        out_shape.append(jax.ShapeDtypeStruct((nb, s, 128), F32))
        out_specs.append(pl.BlockSpec((1, ROW_TILE, 128), tile))
    out = pl.pallas_call(
        _inproj_kernel,
        out_shape=out_shape,
        grid=(nb, s // ROW_TILE),
        in_specs=[pl.BlockSpec((1, ROW_TILE, D_MODEL), tile),
                  pl.BlockSpec((1, 1, D_MODEL), lambda b, j: (row(b, j), 0, 0)),
                  pl.BlockSpec((1, 1, D_MODEL), lambda b, j: (row(b, j), 0, 1)),
                  pl.BlockSpec((D_MODEL, n), lambda b, j: (0, 0))],
        out_specs=out_specs,
        compiler_params=_cparams(("parallel", "parallel")),
        name="inproj",
    )(h, mods_l, mods_l, w)
    return out if f32_tail else out[0]


def _s5_tables(lam_re, lam_im, log_dt, b_re, b_im, c_re, c_im, d_skip):
    L, H, P, G = S5_L, S5_H, S5_P, S5_G
    lam_re, lam_im = lam_re.astype(F32), lam_im.astype(F32)
    dt = jnp.exp(log_dt.astype(F32))[..., None]
    mag = jnp.exp(lam_re * dt)
    a_re = mag * jnp.cos(lam_im * dt)
    a_im = mag * jnp.sin(lam_im * dt)
    den = lam_re * lam_re + lam_im * lam_im
    f_re = ((a_re - 1.0) * lam_re + a_im * lam_im) / den
    f_im = (a_im * lam_re - (a_re - 1.0) * lam_im) / den
    b_re, b_im = b_re.astype(F32), b_im.astype(F32)
    bb_re = f_re[..., None] * b_re - f_im[..., None] * b_im
    bb_im = f_re[..., None] * b_im + f_im[..., None] * b_re
    tau = jnp.arange(L + 1, dtype=F32)[:, None, None, None]
    mag_t = jnp.exp(lam_re[None] * dt[None] * tau)
    ang_t = lam_im[None] * dt[None] * tau
    p_re = mag_t * jnp.cos(ang_t)
    p_im = mag_t * jnp.sin(ang_t)
    c_re, c_im = c_re.astype(F32), c_im.astype(F32)
    e_re = c_re[None] * p_re[:, :, :, None, :] - c_im[None] * p_im[:, :, :, None, :]
    e_im = c_re[None] * p_im[:, :, :, None, :] + c_im[None] * p_re[:, :, :, None, :]
    k = (jnp.einsum('tdgjp,dgpi->tdgji', e_re, bb_re, precision=HIGHEST)
         - jnp.einsum('tdgjp,dgpi->tdgji', e_im, bb_im, precision=HIGHEST))
    s_i, t_i = np.meshgrid(np.arange(L), np.arange(L), indexing='ij')
    kf = k[np.clip(t_i - s_i, 0, L), 0] * jnp.asarray(t_i >= s_i, F32)[..., None, None, None]
    kb = k[np.clip(s_i - t_i, 0, L), 1] * jnp.asarray(s_i >= t_i, F32)[..., None, None, None]
    skip = (jnp.asarray(s_i == t_i, F32)[..., None, None, None]
            * (jnp.eye(H, dtype=F32)[None] * d_skip.astype(F32).reshape(G, 1, H))[None, None])
    intra = (kf + kb + skip).transpose(2, 0, 4, 1, 3).reshape(G, L * H, L * H)

    def state_in(d, pw):
        pr, pi = p_re[pw, d], p_im[pw, d]
        re = pr[..., None] * bb_re[d][None] - pi[..., None] * bb_im[d][None]
        im = pr[..., None] * bb_im[d][None] + pi[..., None] * bb_re[d][None]
        return jnp.concatenate([re, im], axis=2).transpose(1, 0, 3, 2).reshape(G, L * H, 2 * P)

    sf = state_in(0, np.arange(L)[::-1].copy())
    sb = state_in(1, np.arange(L))
    w1 = jnp.concatenate([intra, sf, sb], axis=-1)

    def state_out(d, pw):
        er, ei = e_re[pw, d], e_im[pw, d]
        return jnp.concatenate([er, -ei], axis=-1).transpose(1, 3, 0, 2).reshape(G, 2 * P, L * H)

    w2 = jnp.concatenate([state_out(0, np.arange(1, L + 1)),
                          state_out(1, np.arange(L, 0, -1))], axis=1)
    ar, ai = p_re[L], p_im[L]
    rows = []
    for d in range(2):
        rows += [jnp.concatenate([ar[d], ar[d]], -1), jnp.concatenate([-ai[d], ai[d]], -1),
                 jnp.concatenate([ai[d], -ai[d]], -1)]
    rows += [jnp.zeros_like(rows[0])] * 2
    al = jnp.stack(rows, axis=1)
    return w1.astype(BF16), w2.astype(BF16), al


def _s5_kernel(x_ref, w1_ref, w2_ref, al_ref, y_ref, xw_ref, sw_ref, hh_ref, *, nch, nctx):
    lh = S5_L * S5_H
    p2 = 2 * S5_P
    xw_ref[...] = jnp.dot(x_ref[0], w1_ref[0], preferred_element_type=F32)
    sw_ref[:, 0:p2] = pltpu.roll(xw_ref[:, lh:lh + p2], S5_P, 1)
    sw_ref[:, p2:2 * p2] = pltpu.roll(xw_ref[:, lh + p2:lh + 2 * p2], S5_P, 1)
    al = al_ref[0]
    a1f, a2f, a3f, a1b, a2b, a3b = [al[i:i + 1] for i in range(6)]

    def body(k, carry):
        vf, wf, vb, wb = carry
        kb = jnp.where(k < nctx, nctx - 1 - k, nch + nctx - 1 - k)
        rf = pl.ds(pl.multiple_of(k * 8, 8), 8)
        rb = pl.ds(pl.multiple_of(kb * 8, 8), 8)
        hh_ref[rf, 0:p2] = vf
        hh_ref[rb, p2:2 * p2] = vb
        s_f = xw_ref[rf, lh:lh + p2]
        s_fs = sw_ref[rf, 0:p2]
        s_b = xw_ref[rb, lh + p2:lh + 2 * p2]
        s_bs = sw_ref[rb, p2:2 * p2]
        return (vf * a1f + wf * a2f + s_f, wf * a1f + vf * a3f + s_fs,
                vb * a1b + wb * a2b + s_b, wb * a1b + vb * a3b + s_bs)

    zero = jnp.zeros((8, p2), F32)
    lax.fori_loop(0, nch, body, (zero,) * 4)
    y_ref[0] = (xw_ref[:, 0:lh] + _bdot(hh_ref[...], w2_ref[0])).astype(BF16)


def _s5(u, tables):
    nb, s, _ = u.shape
    nch = s // S5_L
    lh = S5_L * S5_H
    r = nch * 8
    x = u.astype(BF16).reshape(nb, nch, S5_L, S5_G, S5_H).transpose(3, 1, 0, 2, 4)
    x = jnp.pad(x, ((0, 0), (0, 0), (0, 8 - nb), (0, 0), (0, 0))).reshape(S5_G, r, lh)
    w1, w2, al = tables
    y = pl.pallas_call(
        functools.partial(_s5_kernel, nch=nch, nctx=CTX // S5_L),
        out_shape=jax.ShapeDtypeStruct((S5_G, r, lh), BF16),
        grid=(S5_G,),
        in_specs=[pl.BlockSpec((1, r, lh), lambda g: (g, 0, 0)),
                  pl.BlockSpec((1, lh, 2 * lh), lambda g: (g, 0, 0)),
                  pl.BlockSpec((1, lh, lh), lambda g: (g, 0, 0)),
                  pl.BlockSpec((1, 8, 2 * S5_P), lambda g: (g, 0, 0))],
        out_specs=pl.BlockSpec((1, r, lh), lambda g: (g, 0, 0)),
        scratch_shapes=[pltpu.VMEM((r, 2 * lh), F32), pltpu.VMEM((r, lh), F32), pltpu.VMEM((r, lh), F32)],
        compiler_params=_cparams(("parallel",)),
        name="s5",
    )(x, w1, w2, al)
    y = y.reshape(S5_G, nch, 8, S5_L, S5_H)[:, :, :nb]
    return y.transpose(2, 1, 3, 0, 4).reshape(nb, s, HALF)


def _na_bias_tables(rpb, rows):
    w = GRID_W
    nh, ndr, ndc = rpb.shape
    v = jnp.pad(rpb.astype(F32), ((0, 0), (0, 0), (w - NA_WC, 2 * w - ndc - (w - NA_WC))))
    skew = jnp.tile(v, (1, 1, w))[:, :, :w * (2 * w - 1)].reshape(nh, ndr, w, 2 * w - 1)
    tc = skew[..., w - 1:]
    valid, dr = [], []
    for r0 in (0, NA_QR, rows - NA_QR):
        k0 = int(np.clip(r0 - NA_WR // 2, 0, rows - NA_KR))
        rq = r0 + np.arange(NA_QR)[:, None, None, None]
        wq = np.arange(w)[None, :, None, None]
        kr = k0 + np.arange(NA_KR)[None, None, :, None]
        kc = np.arange(w)[None, None, None, :]
        rs = np.clip(rq - NA_WR // 2, 0, rows - NA_WR)
        cs = np.clip(wq - NA_WC // 2, 0, w - NA_WC)
        valid.append((kr >= rs) & (kr < rs + NA_WR) & (kc >= cs) & (kc < cs + NA_WC))
        dr.append(np.clip(kr - rq + NA_WR - 1, 0, ndr - 1)[:, 0, :, 0])
    dr = np.stack(dr).reshape(-1)
    bias = jnp.take(tc, jnp.asarray(dr), axis=1)
    bias = bias.reshape(nh, 3, NA_QR, NA_KR, w, w).transpose(0, 1, 2, 4, 3, 5)
    bias = jnp.where(jnp.asarray(np.stack(valid))[None], bias, NEG)
    return bias.reshape(nh, 3, NA_QR * w, NA_KR * w)


def _natten_kernel(q_ref, k_ref, v_ref, bias_ref, o_ref, qb_ref, kb_ref, vb_ref, *, rows):
    nq = NA_QR * GRID_W
    nk = NA_KR * GRID_W
    nblk = rows // NA_QR
    scale = NA_DH ** -0.5
    qb_ref[...] = (q_ref[0] * scale).astype(BF16)
    kb_ref[...] = k_ref[0].astype(BF16)
    vb_ref[...] = v_ref[0].astype(BF16)
    lane = lax.broadcasted_iota(jnp.int32, (1, 2 * NA_DH), 1)
    head_mask = [lane < NA_DH, lane >= NA_DH]
    kc = kb_ref[0:CTX, :]
    vc = vb_ref[0:CTX, :]

    def attend(q, extra):
        outs = []
        for h in range(2):
            qm = jnp.where(head_mask[h], q, jnp.zeros_like(q))
            s_c = _bdot_nt(qm, kc)
            m = jnp.max(s_c, axis=-1, keepdims=True)
            if extra is not None:
                k_loc, v_loc, bias = extra
                s_l = _bdot_nt(qm, k_loc) + bias[h]
                m = jnp.maximum(m, jnp.max(s_l, axis=-1, keepdims=True))
                p_l = jnp.exp(s_l - m)
            p_c = jnp.exp(s_c - m)
            den = jnp.sum(p_c, axis=-1, keepdims=True)
            acc = _bdot(p_c, vc)
            if extra is not None:
                den = den + jnp.sum(p_l, axis=-1, keepdims=True)
                acc = acc + _bdot(p_l, v_loc)
            outs.append(acc / den)
        return jnp.where(head_mask[0], outs[0], outs[1])

    o_ref[0, 0:CTX, :] = attend(qb_ref[0:CTX, :], None)

    def body(i, carry):
        r0 = i * NA_QR
        k0 = jnp.clip(r0 - NA_WR // 2, 0, rows - NA_KR)
        pat = jnp.where(i == 0, 0, jnp.where(i == nblk - 1, 2, 1))
        qs = pl.multiple_of(CTX + r0 * GRID_W, GRID_W)
        ks = pl.multiple_of(CTX + k0 * GRID_W, GRID_W)
        q = qb_ref[pl.ds(qs, nq), :]
        k_loc = kb_ref[pl.ds(ks, nk), :]
        v_loc = vb_ref[pl.ds(ks, nk), :]
        bias = [bias_ref[h, pat] for h in range(2)]
        o_ref[0, pl.ds(qs, nq), :] = attend(q, (k_loc, v_loc, bias))
        return carry

    lax.fori_loop(0, nblk, body, 0)


def _natten(a, bias):
    nb, s, _ = a.shape
    rows = (s - CTX) // GRID_W
    pw = 2 * NA_DH
    nq, nk = NA_QR * GRID_W, NA_KR * GRID_W
    hb = HALF // pw
    return pl.pallas_call(
        functools.partial(_natten_kernel, rows=rows),
        out_shape=jax.ShapeDtypeStruct((nb, s, HALF), F32),
        grid=(nb, hb),
        in_specs=[pl.BlockSpec((1, s, pw), lambda b, h: (b, 0, hb + h)),
                  pl.BlockSpec((1, s, pw), lambda b, h: (b, 0, 2 * hb + h)),
                  pl.BlockSpec((1, s, pw), lambda b, h: (b, 0, 3 * hb + h)),
                  pl.BlockSpec((2, 3, nq, nk), lambda b, h: (h, 0, 0, 0))],
        out_specs=pl.BlockSpec((1, s, pw), lambda b, h: (b, 0, h)),
        scratch_shapes=[pltpu.VMEM((s, pw), BF16)] * 3,
        compiler_params=_cparams(("parallel", "parallel")),
        name="natten",
    )(a, a, a, bias)


def _gelu_tanh(x):
    return 0.5 * x * (1.0 + jnp.tanh(math.sqrt(2.0 / math.pi) * (x + 0.044715 * (x * x * x))))


def _out_kernel(ya_ref, yb_ref, h_ref, g_ref, e0_ref, e1_ref, wo_ref, lng_ref, lnb_ref, o_ref, *, glu):
    if glu:
        z = _bdot(_gelu_tanh(ya_ref[0].astype(F32)), e0_ref[...]) + e1_ref[...]
        ya = z[:, :HALF] * jax.nn.sigmoid(z[:, HALF:])
    else:
        parts = []
        for hd in range(GLA_HEADS):
            cols = slice(hd * GLA_DV, (hd + 1) * GLA_DV)
            o = ya_ref[0, :, cols]
            o = o * lax.rsqrt(jnp.mean(o * o, axis=-1, keepdims=True) + EPS) * e1_ref[:, cols]
            parts.append(o * _silu(e0_ref[0, :, cols].astype(F32)))
        ya = jnp.concatenate(parts, axis=1)
    y = _bdot(ya, wo_ref[0:HALF, :]) + _bdot(yb_ref[0], wo_ref[HALF:, :])
    o_ref[0] = _layer_norm(ALPHA * h_ref[0] + g_ref[0] * y, lng_ref[...], lnb_ref[...])


def _outproj(ya, yb, h, mods_l, w_out, ln_g, ln_b, *, glu_w=None, glu_b=None, proj=None, norm_g=None):
    nb, s, _ = h.shape
    row = _mod_row(nb)
    glu = glu_w is not None
    tile = lambda b, j: (b, j, 0)
    const = lambda b, j: (0, 0)
    if glu:
        extra_specs = [pl.BlockSpec((HALF, D_MODEL), const), pl.BlockSpec((1, D_MODEL), const)]
        extra = [glu_w.astype(BF16), glu_b.astype(F32).reshape(1, D_MODEL)]
    else:
        rblk = (2 * GLA_HEADS * GLA_DK + HALF) // HALF
        extra_specs = [pl.BlockSpec((1, ROW_TILE, HALF), lambda b, j: (b, j, rblk)),
                       pl.BlockSpec((1, HALF), const)]
        extra = [proj, norm_g.astype(F32).reshape(1, HALF)]
    in_specs = [pl.BlockSpec((1, ROW_TILE, HALF), tile),
                pl.BlockSpec((1, ROW_TILE, HALF), tile),
                pl.BlockSpec((1, ROW_TILE, D_MODEL), tile),
                pl.BlockSpec((1, 1, D_MODEL), lambda b, j: (row(b, j), 0, 2))] + extra_specs + [
                pl.BlockSpec((D_MODEL, D_MODEL), const), pl.BlockSpec((1, D_MODEL), const),
                pl.BlockSpec((1, D_MODEL), const)]
    return pl.pallas_call(
        functools.partial(_out_kernel, glu=glu),
        out_shape=jax.ShapeDtypeStruct((nb, s, D_MODEL), F32),
        grid=(nb, s // ROW_TILE),
        in_specs=in_specs,
        out_specs=pl.BlockSpec((1, ROW_TILE, D_MODEL), tile),
        compiler_params=_cparams(("parallel", "parallel")),
        name="outproj_glu" if glu else "outproj_gate",
    )(ya, yb, h, mods_l, *extra, w_out.astype(BF16), ln_g.astype(F32).reshape(1, D_MODEL),
      ln_b.astype(F32).reshape(1, D_MODEL))


def _log_sigmoid(x):
    return jnp.minimum(x, 0.0) - jnp.log1p(jnp.exp(-jnp.abs(x)))


def _gla_kernel(q_ref, k_ref, v_ref, z_ref, wg_ref, bg_ref, o_ref, gf_ref, gb_ref, ob_ref, st_ref, *, nch):
    c = GLA_CHUNK
    dk2 = 2 * GLA_DK
    nctx = CTX // c
    z = z_ref[0]
    gf_ref[...] = _log_sigmoid(_hdot(z, wg_ref[0, 0]) + bg_ref[0, 0]) * (1.0 / GLA_TAU)
    gb_ref[...] = _log_sigmoid(_hdot(z, wg_ref[1, 0]) + bg_ref[1, 0]) * (1.0 / GLA_TAU)
    ri = lax.broadcasted_iota(jnp.int32, (c, c), 0)
    ci = lax.broadcasted_iota(jnp.int32, (c, c), 1)
    lane = lax.broadcasted_iota(jnp.int32, (1, dk2), 1)
    head_mask = [lane < GLA_DK, lane >= GLA_DK]
    scale = GLA_DK ** -0.5

    st_ref[...] = jnp.zeros_like(st_ref)

    def chunk(direction, n):
        g_ref = gf_ref if direction == 0 else gb_ref
        keep = (ci <= ri) if direction == 0 else (ci >= ri)
        tri = jnp.where(keep, 1.0, 0.0).astype(BF16)
        rs = pl.ds(pl.multiple_of(n * c, c), c)
        g = g_ref[rs, :]
        g_hi = g.astype(BF16)
        g_lo = (g - g_hi.astype(F32)).astype(BF16)
        gc = (jnp.dot(tri, g_hi, preferred_element_type=F32)
              + jnp.dot(tri, g_lo, preferred_element_type=F32))
        gend = gc[c - 1:c, :] if direction == 0 else gc[0:1, :]
        q = q_ref[0, rs, :] * scale
        k = k_ref[0, rs, :]
        qd = q * jnp.exp(gc)
        kd = k * jnp.exp(-gc)
        kdec = k * jnp.exp(gend - gc)
        dcol = jnp.exp(jnp.broadcast_to(gend, (dk2, dk2)).T)
        dcol = jnp.concatenate([dcol, dcol], axis=1)
        qs = jnp.concatenate([jnp.where(head_mask[0], qd, 0.0), jnp.where(head_mask[1], qd, 0.0)],
                             axis=0).astype(BF16)
        att = jnp.where(jnp.concatenate([keep, keep], axis=0), _bdot_nt(qs, kd), 0.0)
        v = v_ref[0, rs, :].astype(BF16)
        st = st_ref[direction]
        o2 = _bdot(att, v) + _bdot(qs, st)
        st_ref[direction] = dcol * st + jnp.dot(kdec.T.astype(BF16), v, preferred_element_type=F32)
        o = jnp.concatenate([o2[0:c, 0:GLA_DV], o2[c:2 * c, GLA_DV:2 * GLA_DV]], axis=1)
        if direction == 0:
            o_ref[0, rs, :] = o
        else:
            ob_ref[rs, :] = o

    def body(i, carry):
        chunk(0, i)
        chunk(1, jnp.where(i < nctx, nctx - 1 - i, nch + nctx - 1 - i))
        return carry

    lax.fori_loop(0, nch, body, 0, unroll=4)
    o_ref[0] += ob_ref[...]


def _gla(a, z, wg, bg):
    nb, s, _ = a.shape
    nch = s // GLA_CHUNK
    hp = GLA_HEADS // 2
    dk2, dv2 = 2 * GLA_DK, 2 * GLA_DV
    return pl.pallas_call(
        functools.partial(_gla_kernel, nch=nch),
        out_shape=jax.ShapeDtypeStruct((nb, s, HALF), F32),
        grid=(nb, hp),
        in_specs=[pl.BlockSpec((1, s, dk2), lambda b, h: (b, 0, h)),
                  pl.BlockSpec((1, s, dk2), lambda b, h: (b, 0, hp + h)),
                  pl.BlockSpec((1, s, dv2), lambda b, h: (b, 0, hp + h)),
                  pl.BlockSpec((1, s, 128), lambda b, h: (b, 0, 0)),
                  pl.BlockSpec((2, 1, 128, dk2), lambda b, h: (0, h, 0, 0)),
                  pl.BlockSpec((2, 1, 1, dk2), lambda b, h: (0, h, 0, 0))],
        out_specs=pl.BlockSpec((1, s, dv2), lambda b, h: (b, 0, h)),
        scratch_shapes=[pltpu.VMEM((s, dk2), F32), pltpu.VMEM((s, dk2), F32), pltpu.VMEM((s, dv2), F32),
                        pltpu.VMEM((2, dk2, dv2), F32)],
        compiler_params=_cparams(("parallel", "parallel")),
        name="gla",
    )(a, a, a, z, wg, bg)


def _rope_tables(s):
    t = np.arange(s - CTX)
    n = DIFF_DH // 4
    inv = ROPE_BASE ** (-np.arange(n, dtype=np.float64) / n)
    ang_r = (t // GRID_W)[:, None] * inv[None]
    ang_c = (t % GRID_W)[:, None] * inv[None]
    ang = np.concatenate([ang_r, ang_r, ang_c, ang_c], axis=1)
    ang = np.concatenate([ang, ang], axis=1)
    cos = np.concatenate([np.ones((CTX, 128)), np.cos(ang)], axis=0)
    sin = np.concatenate([np.zeros((CTX, 128)), np.sin(ang)], axis=0)
    first = ((np.arange(128) // n) % 2 == 0)[None]
    sin_up = np.where(first, -sin, 0.0)
    sin_dn = np.where(first, 0.0, sin)
    f = lambda z: jnp.asarray(z, F32)
    return f(cos), f(sin_up), f(sin_dn)


def _rope_kernel(q_ref, k_ref, cos_ref, su_ref, sd_ref, qo_ref, ko_ref):
    n = DIFF_DH // 4
    cos = jnp.concatenate([cos_ref[...]] * DIFF_HEADS, axis=1)
    su = jnp.concatenate([su_ref[...]] * DIFF_HEADS, axis=1)
    sd = jnp.concatenate([sd_ref[...]] * DIFF_HEADS, axis=1)

    def rot(x):
        w = x.shape[1]
        return x * cos + pltpu.roll(x, w - n, 1) * su + pltpu.roll(x, n, 1) * sd

    qo_ref[0] = rot(q_ref[0].astype(F32) * (DIFF_DH ** -0.5 * LOG2E)).astype(BF16)
    ko_ref[0] = rot(k_ref[0].astype(F32)).astype(BF16)


def _rope(a, tables):
    nb, s, _ = a.shape
    qblk = (2 * GLA_HEADS * GLA_DK + 2 * HALF) // HALF
    tile = lambda b, j: (b, j, 0)
    tab = pl.BlockSpec((ROW_TILE, 128), lambda b, j: (j, 0))
    return pl.pallas_call(
        _rope_kernel,
        out_shape=[jax.ShapeDtypeStruct((nb, s, HALF), BF16)] * 2,
        grid=(nb, s // ROW_TILE),
        in_specs=[pl.BlockSpec((1, ROW_TILE, HALF), lambda b, j: (b, j, qblk)),
                  pl.BlockSpec((1, ROW_TILE, HALF), lambda b, j: (b, j, qblk + 1)),
                  tab, tab, tab],
        out_specs=[pl.BlockSpec((1, ROW_TILE, HALF), tile)] * 2,
        compiler_params=_cparams(("parallel", "parallel")),
        name="rope",
    )(a, a, *tables)


def _diff_kernel(q_ref, k_ref, v_ref, lam_ref, ng_ref, o_ref, vb_ref, *, lam_init):
    j = pl.program_id(2)
    lane = lax.broadcasted_iota(jnp.int32, (1, 2 * DIFF_DH), 1)
    map_mask = [lane < DIFF_DH, lane >= DIFF_DH]
    lv = lam_ref[...]
    lam = (jnp.exp(jnp.sum(lv[0:1] * lv[1:2], axis=-1, keepdims=True))
           - jnp.exp(jnp.sum(lv[2:3] * lv[3:4], axis=-1, keepdims=True)) + lam_init)

    @pl.when(j == 0)
    def _():
        vb_ref[...] = v_ref[0].astype(BF16)

    def attend(nkeys):
        q = q_ref[0]
        k = k_ref[0, 0:nkeys, :]
        sc, mx, inv = [], [], []
        for m in range(2):
            s = _bdot_nt(jnp.where(map_mask[m], q, jnp.zeros_like(q)), k)
            sc.append(s)
            mx.append(jnp.max(s, axis=-1, keepdims=True))
            inv.append(1.0 / jnp.sum(jnp.exp2(s - mx[m]), axis=-1, keepdims=True))
        w = (jnp.exp2(sc[0] - mx[0]) * inv[0] - jnp.exp2(sc[1] - mx[1]) * (lam * inv[1])).astype(BF16)
        o = jnp.dot(w, vb_ref[0:nkeys, :], preferred_element_type=F32)
        o = o * lax.rsqrt(jnp.mean(o * o, axis=-1, keepdims=True) + EPS) * ng_ref[...]
        o_ref[0] = o * (1.0 - lam_init)

    @pl.when(j == 0)
    def _():
        attend(CTX)

    @pl.when(j > 0)
    def _():
        attend(k_ref.shape[1])


def _diff(qr, kr, a, lam_vecs, ng, lam_init):
    nb, s, _ = qr.shape
    dw = 2 * DIFF_DH
    vblk = (2 * GLA_HEADS * GLA_DK + 4 * HALF) // dw
    return pl.pallas_call(
        functools.partial(_diff_kernel, lam_init=lam_init),
        out_shape=jax.ShapeDtypeStruct((nb, s, HALF), F32),
        grid=(nb, DIFF_HEADS, s // ROW_TILE),
        in_specs=[pl.BlockSpec((1, ROW_TILE, dw), lambda b, h, j: (b, j, h)),
                  pl.BlockSpec((1, s, dw), lambda b, h, j: (b, 0, h)),
                  pl.BlockSpec((1, s, DIFF_DV), lambda b, h, j: (b, 0, vblk + h)),
                  pl.BlockSpec((8, DIFF_DH), lambda b, h, j: (0, 0)),
                  pl.BlockSpec((1, DIFF_DV), lambda b, h, j: (0, 0))],
        out_specs=pl.BlockSpec((1, ROW_TILE, DIFF_DV), lambda b, h, j: (b, j, h)),
        scratch_shapes=[pltpu.VMEM((s, DIFF_DV), BF16)],
        compiler_params=_cparams(("parallel", "parallel", "arbitrary")),
        name="diff_attn",
    )(qr, kr, a, lam_vecs, ng)


def _moe_kernel(h_ref, m0_ref, m1_ref, mc_ref, wr_ref, br_ref, tri_ref, wg_ref, wu_ref, wd_ref, lng_ref, lnb_ref,
                o_ref, as_ref, cs_ref, ps_ref, ys_ref, win_ref, *, seq, win):
    t = pl.program_id(0)
    e = pl.program_id(1)
    tm = h_ref.shape[0]
    lane = lax.broadcasted_iota(jnp.int32, (1, 128), 1).astype(F32)
    d = D_MODEL

    def mod(col):
        b0 = (t * tm) // seq
        grow = t * tm + lax.broadcasted_iota(jnp.int32, (tm, 1), 0)
        in_next = grow >= (b0 + 1) * seq
        local = grow - jnp.where(in_next, b0 + 1, b0) * seq
        cols = slice(col * d, (col + 1) * d)
        return jnp.where(local < CTX, mc_ref[0, :, cols],
                         jnp.where(in_next, m1_ref[0, :, cols], m0_ref[0, :, cols]))

    @pl.when(e == 0)
    def _():
        a = h_ref[...] * (1.0 + mod(1)) + mod(0)
        logits = _hdot(a, wr_ref[...]) + br_ref[...]
        is_g = lane < N_GROUPS
        lg = jnp.where(is_g, logits, NEG)
        mg = jnp.max(lg, axis=-1, keepdims=True)
        g_sel = jnp.min(jnp.where(lg == mg, lane, 128.0), axis=-1, keepdims=True)
        p_grp = 1.0 / jnp.sum(jnp.where(is_g, jnp.exp(lg - mg), 0.0), axis=-1, keepdims=True)
        lo = N_GROUPS + g_sel * EPG
        in_grp = (lane >= lo) & (lane < lo + EPG)
        le = jnp.where(in_grp, logits, NEG)
        v1 = jnp.max(le, axis=-1, keepdims=True)
        i1 = jnp.min(jnp.where(le == v1, lane, 128.0), axis=-1, keepdims=True)
        le2 = jnp.where(lane == i1, NEG, le)
        v2 = jnp.max(le2, axis=-1, keepdims=True)
        i2 = jnp.min(jnp.where(le2 == v2, lane, 128.0), axis=-1, keepdims=True)
        e2 = jnp.exp(v2 - v1)
        w1 = p_grp / (1.0 + e2)
        w2 = p_grp * e2 / (1.0 + e2)
        comb = jnp.where(lane == i1, w1, jnp.where(lane == i2, w2, 0.0))

        onehot = jnp.where(lane == g_sel, 1.0, 0.0).astype(BF16)
        ranks = jnp.dot(tri_ref[...], onehot, preferred_element_type=F32)
        cnt = ranks[tm - 1:tm, :]
        n = [jnp.sum(jnp.where(lane == float(g), cnt, 0.0), axis=-1, keepdims=True) for g in range(N_GROUPS)]
        off = [jnp.zeros_like(n[0]), n[0], n[0] + n[1], n[0] + n[1] + n[2]]
        off_vec = sum(jnp.where(lane == float(g), off[g], 0.0) for g in range(1, N_GROUPS))
        dest = jnp.sum(jnp.where(lane == g_sel, ranks + off_vec, 0.0), axis=-1, keepdims=True) - 1.0

        pos = lax.broadcasted_iota(jnp.int32, (1, tm), 1).astype(F32)
        ps_ref[...] = jnp.where(dest == pos, 1.0, 0.0).astype(BF16)
        hi = jnp.floor(dest * (1.0 / 32.0))
        digits = jnp.where(lane == 0.0, hi, jnp.where(lane == 1.0, dest - 32.0 * hi, 0.0)).astype(BF16)
        ri8 = lax.broadcasted_iota(jnp.int32, (8, 128), 0)
        ci8 = lax.broadcasted_iota(jnp.int32, (8, 128), 1)
        dig_t = _bdot_nt(jnp.where(ri8 == ci8, 1.0, 0.0), digits)
        dest_row = 32.0 * dig_t[0:1, :] + dig_t[1:2, :]
        rows = lax.broadcasted_iota(jnp.int32, (tm, 1), 0).astype(F32)
        pg = jnp.where(dest_row == rows, 1.0, 0.0).astype(BF16)
        as_ref[...] = jnp.dot(pg, a.astype(BF16), preferred_element_type=F32).astype(BF16)
        c_hi = comb.astype(BF16)
        c_lo = (comb - c_hi.astype(F32)).astype(BF16)
        packed = jnp.where(lane < 64.0, c_hi.astype(F32), pltpu.roll(c_lo.astype(F32), 64, 1)).astype(BF16)
        cg = jnp.dot(pg, packed, preferred_element_type=F32)
        cs_ref[...] = cg + pltpu.roll(cg, 64, 1)
        ys_ref[...] = jnp.zeros_like(ys_ref)
        for g in range(N_GROUPS):
            start = jnp.minimum((jnp.sum(off[g]).astype(jnp.int32) // 16) * 16, tm - win)
            end = jnp.sum(off[g] + n[g]).astype(jnp.int32)
            win_ref[g] = start
            win_ref[N_GROUPS + g] = jnp.where(end > start, (end - start + win - 1) // win, 0)

    g = (e * MOE_EB) // EPG
    start = win_ref[g]

    def window(w, carry):
        lo = start + w * win
        rs = pl.ds(pl.multiple_of(jnp.minimum(lo, tm - win), 16), win)
        x = as_ref[rs, :]
        row = jnp.minimum(lo, tm - win) + lax.broadcasted_iota(jnp.int32, (win, 1), 0)
        cs = jnp.where(row >= lo, cs_ref[rs, :], 0.0)
        upd = None
        for j in range(MOE_EB):
            sel = lane == (e * MOE_EB + j + N_GROUPS).astype(F32)
            cw = jnp.sum(jnp.where(sel, cs, 0.0), axis=-1, keepdims=True)
            hg = jnp.dot(x, wg_ref[j], preferred_element_type=F32)
            hu = jnp.dot(x, wu_ref[j], preferred_element_type=F32)
            dn = jnp.dot((_silu(hg) * hu * cw).astype(BF16), wd_ref[j], preferred_element_type=F32)
            upd = dn if upd is None else upd + dn
        ys_ref[rs, :] += upd
        return carry

    lax.fori_loop(0, win_ref[N_GROUPS + g], window, 0)

    @pl.when(e == N_EXPERTS // MOE_EB - 1)
    def _():
        y = jnp.dot(ps_ref[...], ys_ref[...].astype(BF16), preferred_element_type=F32)
        o_ref[...] = _layer_norm(ALPHA * h_ref[...] + mod(2) * y, lng_ref[...], lnb_ref[...])


def _moe(h, mods_l, w_router, b_router, w_gate, w_up, w_down, ln_g, ln_b):
    nb, s, d = h.shape
    rows = nb * s
    tm = 1024 if rows % 1024 == 0 else 256
    win = tm // N_GROUPS + tm // 16
    nt = rows // tm
    const = lambda t, e: (0, 0)
    b0 = lambda t: (t * tm) // s
    neb = N_EXPERTS // MOE_EB
    w_gate, w_up, w_down = w_gate.astype(BF16), w_up.astype(BF16), w_down.astype(BF16)
    out = pl.pallas_call(
        functools.partial(_moe_kernel, seq=s, win=win),
        out_shape=jax.ShapeDtypeStruct((rows, d), F32),
        grid=(nt, neb),
        in_specs=[pl.BlockSpec((tm, d), lambda t, e: (t, 0)),
                  pl.BlockSpec((1, 1, 3 * d), lambda t, e: (b0(t), 0, 1)),
                  pl.BlockSpec((1, 1, 3 * d), lambda t, e: (jnp.minimum(b0(t) + 1, nb - 1), 0, 1)),
                  pl.BlockSpec((1, 1, 3 * d), lambda t, e: (nb, 0, 1)),
                  pl.BlockSpec((d, 128), const),
                  pl.BlockSpec((1, 128), const),
                  pl.BlockSpec((tm, tm), const),
                  pl.BlockSpec((MOE_EB, d, D_EXPERT), lambda t, e: (e, 0, 0)),
                  pl.BlockSpec((MOE_EB, d, D_EXPERT), lambda t, e: (e, 0, 0)),
                  pl.BlockSpec((MOE_EB, D_EXPERT, d), lambda t, e: (e, 0, 0)),
                  pl.BlockSpec((1, d), const),
                  pl.BlockSpec((1, d), const)],
        out_specs=pl.BlockSpec((tm, d), lambda t, e: (t, 0)),
        scratch_shapes=[pltpu.VMEM((tm, d), BF16), pltpu.VMEM((tm, 128), F32), pltpu.VMEM((tm, tm), BF16),
                        pltpu.VMEM((tm, d), F32), pltpu.SMEM((2 * N_GROUPS,), jnp.int32)],
        compiler_params=_cparams(("parallel", "arbitrary")),
        name="moe",
    )(h.reshape(rows, d), mods_l, mods_l, mods_l, w_router, b_router, jnp.tril(jnp.ones((tm, tm), BF16)),
      w_gate, w_up, w_down,
      ln_g.astype(F32).reshape(1, d), ln_b.astype(F32).reshape(1, d))
    return out.reshape(nb, s, d)


def _pad_lanes(x, n):
    return jnp.pad(x, [(0, 0)] * (x.ndim - 1) + [(0, n - x.shape[-1])])


def kernel(x, c, ctx, c_ctx, mod_w, mod_b, ln_g, ln_b, even_w_in, even_w_out, s5_lam_re, s5_lam_im, s5_log_dt, s5_b_re, s5_b_im, s5_c_re, s5_c_im, s5_d, s5_glu_w, s5_glu_b, na_rpb, odd_w_in, odd_w_out, gla_gate_w, gla_gate_b, gla_norm_g, diff_lq1, diff_lk1, diff_lq2, diff_lk2, diff_norm_g, moe_w_grp, moe_b_grp, moe_w_exp, moe_b_exp, moe_w_gate, moe_w_up, moe_w_down):
    nb, t, d = x.shape
    assert d == D_MODEL and ctx.shape[1] == CTX and nb < 8 and t % (NA_QR * GRID_W) == 0
    s = CTX + t
    rows = t // GRID_W
    h = jnp.concatenate([ctx, x], axis=1).astype(F32)
    cc = jnp.zeros((8, d), F32).at[:nb].set(c.astype(F32)).at[nb].set(c_ctx.astype(F32))
    mods = _mods(cc, mod_w.astype(F32), mod_b.astype(F32)).reshape(DEPTH, 8, 1, 6 * d)
    rope_tabs = _rope_tables(s)
    for i in range(DEPTH):
        j = i // 2
        mods_l = mods[i]
        if i % 2 == 0:
            a = _inproj(h, mods_l, even_w_in[j].astype(BF16))
            tables = _s5_tables(s5_lam_re[j], s5_lam_im[j], s5_log_dt[j], s5_b_re[j], s5_b_im[j],
                                s5_c_re[j], s5_c_im[j], s5_d[j])
            y_a = _s5(a[..., :HALF], tables)
            y_b = _natten(a, _na_bias_tables(na_rpb[j], rows))
            h = _outproj(y_a, y_b, h, mods_l, even_w_out[j], ln_g[i, 0], ln_b[i, 0],
                         glu_w=s5_glu_w[j], glu_b=s5_glu_b[j])
        else:
            lam_init = 0.8 - 0.6 * math.exp(-0.3 * i)
            w = odd_w_in[j]
            nqk = 2 * GLA_HEADS * GLA_DK + 2 * HALF
            w = jnp.concatenate([w[:, :nqk], w[:, nqk + 2 * GLA_RANK:],
                                 _pad_lanes(w[:, nqk:nqk + 2 * GLA_RANK], 128)], axis=1).astype(BF16)
            a, z = _inproj(h, mods_l, w, f32_tail=True)
            gw = gla_gate_w[j].astype(F32)
            wg = jnp.zeros((2, 128, GLA_HEADS * GLA_DK), F32)
            for dd in range(2):
                wg = wg.at[dd, dd * GLA_RANK:(dd + 1) * GLA_RANK].set(gw[dd])
            wg = wg.reshape(2, 128, 2, 2 * GLA_DK).transpose(0, 2, 1, 3)
            bg = gla_gate_b[j].astype(F32).reshape(2, 2, 1, 2 * GLA_DK)
            y_a = _gla(a, z, wg, bg)
            qr, kr = _rope(a, rope_tabs)
            lam_vecs = jnp.zeros((8, DIFF_DH), F32).at[0].set(diff_lq1[j]).at[1].set(diff_lk1[j]) \
                .at[2].set(diff_lq2[j]).at[3].set(diff_lk2[j])
            y_b = _diff(qr, kr, a, lam_vecs, diff_norm_g[j].astype(F32).reshape(1, DIFF_DV), lam_init)
            h = _outproj(y_a, y_b, h, mods_l, odd_w_out[j], ln_g[i, 0], ln_b[i, 0],
                         proj=a, norm_g=gla_norm_g[j])
        w_router = _pad_lanes(jnp.concatenate([moe_w_grp[i], moe_w_exp[i]], axis=1).astype(F32), 128)
        b_router = _pad_lanes(jnp.concatenate([moe_b_grp[i], moe_b_exp[i]]).astype(F32)[None], 128)
        h = _moe(h, mods_l, w_router, b_router, moe_w_gate[i], moe_w_up[i], moe_w_down[i], ln_g[i, 1], ln_b[i, 1])
    return h[:, CTX:].astype(x.dtype)
```
